```python
import math
import jax, jax.numpy as jnp
from jax import lax
import numpy as np


D_MODEL = 1024
BATCH = 16
SEQ = 2048
DEPTH = 1

D_RNN = D_MODEL
N_RNN_BLOCKS = 16
RNN_BLOCK = D_RNN // N_RNN_BLOCKS
RNN_CONV_W = 4
RNN_CONV_LEFT = 2
RG_C = 8.0
N_DIR = 2
N_HEADS = 8
HEAD_DIM = 64
V_DIM = 2 * HEAD_DIM
ATTN_WIDTH = N_HEADS * V_DIM
QK_WIDTH = N_HEADS * 2 * HEAD_DIM
ROPE_DIM = HEAD_DIM // 4
ROPE_THETA = 500000.0
Q_BLOCK = 128
N_BRANCHES = 2
D_FF = 2816
FFN_CONV_W = 3
FFN_CONV_LEFT = 1
N_MOD = 6
NORM_EPS = 1e-6

IN_SIZES = (D_RNN, D_RNN, QK_WIDTH, QK_WIDTH, ATTN_WIDTH, N_BRANCHES * D_MODEL)
IN_COLS = sum(IN_SIZES)
IN_SPLITS = tuple(int(v) for v in np.cumsum(IN_SIZES)[:-1])

kernel_name = "hybrid_rglru_diffattn_convffn_encoder_block"


def lambda_init_fn(layer_idx):
    return 0.8 - 0.6 * math.exp(-0.3 * layer_idx)


def rms_norm(x, g):
    xf = x.astype(jnp.float32)
    y = xf * lax.rsqrt(jnp.mean(xf * xf, axis=-1, keepdims=True) + NORM_EPS)
    return (y * g.astype(jnp.float32)).astype(x.dtype)


def depthwise_conv(x, w, b, left):
    K = w.shape[0]
    S = x.shape[1]
    xp = jnp.pad(x, ((0, 0), (left, K - 1 - left), (0, 0)))
    out = b
    for k in range(K):
        out = out + xp[:, k:k + S] * w[k]
    return out


def apply_partial_rope(t, cos, sin):
    half = ROPE_DIM // 2
    t1 = t[..., :half]
    t2 = t[..., half:ROPE_DIM]
    rot = jnp.concatenate([t1 * cos - t2 * sin, t2 * cos + t1 * sin], axis=-1)
    return jnp.concatenate([rot, t[..., ROPE_DIM:]], axis=-1)


def _lin_comb(left, right):
    a_l, b_l = left
    a_r, b_r = right
    return a_l * a_r, a_r * b_l + b_r


def bidir_rg_lru(xc, w_a, b_a, w_i, b_i, lam):
    B, S, _ = xc.shape
    xb = xc.reshape(B, S, N_RNN_BLOCKS, RNN_BLOCK)
    r = jax.nn.sigmoid(jnp.einsum('bsnk,dnkj->dbsnj', xb, w_a).reshape(N_DIR, B, S, D_RNN)
                       + b_a[:, None, None, :])
    i = jax.nn.sigmoid(jnp.einsum('bsnk,dnkj->dbsnj', xb, w_i).reshape(N_DIR, B, S, D_RNN)
                       + b_i[:, None, None, :])
    log_a = -RG_C * r.astype(jnp.float32) * jax.nn.softplus(-lam.astype(jnp.float32))[:, None, None, :]
    a = jnp.exp(log_a)
    u = jnp.sqrt(-jnp.expm1(2.0 * log_a)) * (i * xc[None]).astype(jnp.float32)
    _, h_fwd = lax.associative_scan(_lin_comb, (a[0], u[0]), axis=1)
    _, h_bwd = lax.associative_scan(_lin_comb, (a[1], u[1]), axis=1, reverse=True)
    return (h_fwd + h_bwd).astype(xc.dtype)


def diff_attention(q, k, v, lam, subln_g, lam_init):
    B, S = q.shape[0], q.shape[1]
    n_blk = S // Q_BLOCK
    scale = HEAD_DIM ** -0.5
    qb = q.reshape(B, n_blk, Q_BLOCK, N_HEADS, 2, HEAD_DIM).transpose(1, 0, 2, 3, 4, 5)

    def block(qblk):
        s = jnp.einsum('bqhcd,bkhcd->bhcqk', qblk, k).astype(jnp.float32) * scale
        p = jax.nn.softmax(s, axis=-1)
        w = p[:, :, 0] - lam * p[:, :, 1]
        return jnp.einsum('bhqk,bkhe->bqhe', w.astype(v.dtype), v)

    o = lax.map(block, qb)
    o = o.transpose(1, 0, 2, 3, 4).reshape(B, S, N_HEADS, V_DIM)
    o = rms_norm(o, subln_g) * (1.0 - lam_init)
    return o.reshape(B, S, ATTN_WIDTH)


def setup_inputs(seed: int = 0) -> dict:
    key = jax.random.key(seed)
    ks = iter(jax.random.split(key, 40))

    def nrm(shape, scale):
        return jax.random.normal(next(ks), shape, jnp.float32) * scale

    L = DEPTH
    x = nrm((BATCH, SEQ, D_MODEL), 1.0)
    c = nrm((BATCH, D_MODEL), 1.0)
    offs = jax.random.randint(next(ks), (BATCH, 1), 0, 4096, dtype=jnp.int32)
    positions = (jnp.arange(SEQ, dtype=jnp.int32)[None, :] + offs).astype(jnp.int32)

    u = jax.random.uniform(next(ks), (L, N_DIR, D_RNN), jnp.float32, 0.9, 0.999)
    base = u ** (1.0 / RG_C)
    rg_lambda = jnp.log(base / (1.0 - base))

    return {
        "x": x,
        "c": c,
        "positions": positions,
        "w_ada": nrm((L, D_MODEL, N_MOD * D_MODEL), 0.5 * D_MODEL ** -0.5),
        "b_ada": nrm((L, N_MOD * D_MODEL), 0.01),
        "norm1_g": 1.0 + nrm((L, D_MODEL), 0.02),
        "w_in": nrm((L, D_MODEL, IN_COLS), D_MODEL ** -0.5),
        "conv_rnn_w": nrm((L, RNN_CONV_W, D_RNN), RNN_CONV_W ** -0.5),
        "conv_rnn_b": nrm((L, D_RNN), 0.01),
        "w_rg_a": nrm((L, N_DIR, N_RNN_BLOCKS, RNN_BLOCK, RNN_BLOCK), RNN_BLOCK ** -0.5),
        "b_rg_a": nrm((L, N_DIR, D_RNN), 0.01),
        "w_rg_i": nrm((L, N_DIR, N_RNN_BLOCKS, RNN_BLOCK, RNN_BLOCK), RNN_BLOCK ** -0.5),
        "b_rg_i": nrm((L, N_DIR, D_RNN), 0.01),
        "rg_lambda": rg_lambda,
        "w_rnn_o": nrm((L, D_RNN, D_MODEL), D_RNN ** -0.5),
        "lam_q1": nrm((L, HEAD_DIM), 0.1),
        "lam_k1": nrm((L, HEAD_DIM), 0.1),
        "lam_q2": nrm((L, HEAD_DIM), 0.1),
        "lam_k2": nrm((L, HEAD_DIM), 0.1),
        "subln_g": 1.0 + nrm((L, V_DIM), 0.02),
        "w_attn_o": nrm((L, ATTN_WIDTH, D_MODEL), ATTN_WIDTH ** -0.5),
        "w_out": nrm((L, D_MODEL, D_MODEL), D_MODEL ** -0.5),
        "norm2_g": 1.0 + nrm((L, D_MODEL), 0.02),
        "w_up": nrm((L, D_MODEL, 2 * D_FF), D_MODEL ** -0.5),
        "conv_ffn_w": nrm((L, FFN_CONV_W, 2 * D_FF), FFN_CONV_W ** -0.5),
        "conv_ffn_b": nrm((L, 2 * D_FF), 0.01),
        "w_down": nrm((L, D_FF, D_MODEL), D_FF ** -0.5),
        "final_g": 1.0 + nrm((D_MODEL,), 0.02),
    }


def reference(x, c, positions, w_ada, b_ada, norm1_g, w_in, conv_rnn_w, conv_rnn_b,
              w_rg_a, b_rg_a, w_rg_i, b_rg_i, rg_lambda, w_rnn_o,
              lam_q1, lam_k1, lam_q2, lam_k2, subln_g, w_attn_o, w_out,
              norm2_g, w_up, conv_ffn_w, conv_ffn_b, w_down, final_g):
    B, S, _ = x.shape

    inv_freq = ROPE_THETA ** (-jnp.arange(0, ROPE_DIM, 2, dtype=jnp.float32) / ROPE_DIM)
    ang = positions.astype(jnp.float32)[..., None] * inv_freq
    cos = jnp.cos(ang)[:, :, None, None, :].astype(x.dtype)
    sin = jnp.sin(ang)[:, :, None, None, :].astype(x.dtype)
    c_act = jax.nn.silu(c)

    for l in range(DEPTH):
        lam_init = lambda_init_fn(l)
        mod = (c_act @ w_ada[l] + b_ada[l])[:, None, :]
        sh1, sc1, g1, sh2, sc2, g2 = jnp.split(mod, N_MOD, axis=-1)

        h = rms_norm(x, norm1_g[l]) * (1.0 + sc1) + sh1
        proj = h @ w_in[l]
        xr, yr, q, k, v, gates = jnp.split(proj, IN_SPLITS, axis=-1)

        xr = depthwise_conv(xr, conv_rnn_w[l], conv_rnn_b[l], RNN_CONV_LEFT)
        hr = bidir_rg_lru(xr, w_rg_a[l], b_rg_a[l], w_rg_i[l], b_rg_i[l], rg_lambda[l])
        branch_a = (hr * jax.nn.gelu(yr)) @ w_rnn_o[l]

        q = apply_partial_rope(q.reshape(B, S, N_HEADS, 2, HEAD_DIM), cos, sin)
        k = apply_partial_rope(k.reshape(B, S, N_HEADS, 2, HEAD_DIM), cos, sin)
        v = v.reshape(B, S, N_HEADS, V_DIM)
        lam = (jnp.exp(jnp.sum(lam_q1[l].astype(jnp.float32) * lam_k1[l].astype(jnp.float32)))
               - jnp.exp(jnp.sum(lam_q2[l].astype(jnp.float32) * lam_k2[l].astype(jnp.float32)))
               + lam_init)
        branch_b = diff_attention(q, k, v, lam, subln_g[l], lam_init) @ w_attn_o[l]

        gate_a, gate_b = jnp.split(jax.nn.sigmoid(gates), N_BRANCHES, axis=-1)
        merged = gate_a * branch_a + gate_b * branch_b
        x = x + g1 * (merged @ w_out[l])

        h = rms_norm(x, norm2_g[l]) * (1.0 + sc2) + sh2
        up = depthwise_conv(h @ w_up[l], conv_ffn_w[l], conv_ffn_b[l], FFN_CONV_LEFT)
        val, gt = jnp.split(up, 2, axis=-1)
        x = x + g2 * ((jax.nn.silu(gt) * val) @ w_down[l])

    return rms_norm(x, final_g)
```

```python
import functools
import math

import jax
import jax.numpy as jnp
from jax import lax
from jax.experimental import pallas as pl
from jax.experimental.pallas import tpu as pltpu

F32 = jnp.float32
BF16 = jnp.bfloat16

D_MODEL = 1024
BATCH = 16
SEQ = 2048
TOKENS = BATCH * SEQ
D_RNN = D_MODEL
N_RNN_BLOCKS = 16
RNN_BLOCK = D_RNN // N_RNN_BLOCKS
RNN_CONV_W = 4
RNN_CONV_LEFT = 2
RG_C = 8.0
N_HEADS = 8
HEAD_DIM = 64
V_DIM = 2 * HEAD_DIM
ROPE_DIM = HEAD_DIM // 4
ROPE_HALF = ROPE_DIM // 2
ROPE_THETA = 500000.0
D_FF = 2816
N_MOD = 6
NORM_EPS = 1e-6
IN_COLS = 7 * D_MODEL
LAM_INIT = 0.8 - 0.6 * math.exp(-0.3 * 0)

LANES = 128
VMEM_LIMIT = 52 * 1024 * 1024

ADA_TN = 1024
IN_TS = 64
IN_TN = 1024
RG_TT = 128
RG_TC = 256
AT_TQ = 512
MG_TS = 32
FF_TS = 32
FF_CH = 256
FF_NCH = D_FF // FF_CH


def _params(sem):
    return pltpu.CompilerParams(dimension_semantics=sem, vmem_limit_bytes=VMEM_LIMIT)


def _ada_kernel(c_ref, w_ref, b_ref, o_ref):
    c = c_ref[...]
    ca = c * jax.nn.sigmoid(c)
    o_ref[...] = jnp.dot(ca, w_ref[...], preferred_element_type=F32,
                         precision=lax.Precision.HIGHEST) + b_ref[...]


def _ada(c, w, b):
    n = w.shape[1]
    return pl.pallas_call(
        _ada_kernel,
        grid=(n // ADA_TN,),
        in_specs=[pl.BlockSpec((BATCH, D_MODEL), lambda j: (0, 0)),
                  pl.BlockSpec((D_MODEL, ADA_TN), lambda j: (0, j)),
                  pl.BlockSpec((1, ADA_TN), lambda j: (0, j))],
        out_specs=pl.BlockSpec((BATCH, ADA_TN), lambda j: (0, j)),
        out_shape=jax.ShapeDtypeStruct((BATCH, n), F32),
        compiler_params=_params(("arbitrary",)),
        name="adaln",
    )(c, w, b.reshape(1, n))


def _rms_mod(x, g, sc, sh):
    ms = jnp.mean(x * x, axis=-1, keepdims=True)
    y = x * lax.rsqrt(ms + NORM_EPS) * g
    return y * (1.0 + sc) + sh


def _inproj_kernel(x_ref, g_ref, sc_ref, sh_ref, pos_ref, invf_ref, w_ref, o_ref,
                   h_scr, cos_scr, sa_scr, sb_scr):
    j = pl.program_id(1)
    tm = IN_TS * BATCH

    @pl.when(j == 0)
    def _():
        h = _rms_mod(x_ref[...], g_ref[...], sc_ref[...], sh_ref[...])
        h_scr[...] = h.reshape(tm, D_MODEL).astype(BF16)

    @pl.when(j == 2)
    def _():
        ang = pos_ref[...] * invf_ref[...]
        c = jnp.cos(ang)
        s = jnp.sin(ang)
        lane = lax.broadcasted_iota(jnp.int32, ang.shape, 1) % HEAD_DIM
        cos_scr[...] = jnp.where(lane < ROPE_DIM, c, 1.0)
        sa_scr[...] = jnp.where(lane < ROPE_HALF, -s, 0.0)
        sb_scr[...] = jnp.where((lane >= ROPE_HALF) & (lane < ROPE_DIM), s, 0.0)

    acc = jnp.dot(h_scr[...], w_ref[...], preferred_element_type=F32)

    def rope_store(scale):
        ct, sa, sb = cos_scr[...], sa_scr[...], sb_scr[...]
        for cidx in range(IN_TN // LANES):
            sl = slice(cidx * LANES, (cidx + 1) * LANES)
            t = acc[:, sl]
            r = (t * ct + pltpu.roll(t, LANES - ROPE_HALF, 1) * sa
                 + pltpu.roll(t, ROPE_HALF, 1) * sb)
            o_ref[:, sl] = (r * scale).astype(BF16)

    @pl.when(j == 2)
    def _():
        rope_store(HEAD_DIM ** -0.5)

    @pl.when(j == 3)
    def _():
        rope_store(1.0)

    @pl.when(j >= 5)
    def _():
        o_ref[...] = jax.nn.sigmoid(acc).astype(BF16)

    @pl.when((j < 2) | (j == 4))
    def _():
        o_ref[...] = acc.astype(BF16)


def _inproj(x3, g, sc, sh, posf, invf, w_bf):
    tm = IN_TS * BATCH
    return pl.pallas_call(
        _inproj_kernel,
        grid=(SEQ // IN_TS, IN_COLS // IN_TN),
        in_specs=[pl.BlockSpec((IN_TS, BATCH, D_MODEL), lambda i, j: (i, 0, 0)),
                  pl.BlockSpec((1, D_MODEL), lambda i, j: (0, 0)),
                  pl.BlockSpec((BATCH, D_MODEL), lambda i, j: (0, 0)),
                  pl.BlockSpec((BATCH, D_MODEL), lambda i, j: (0, 0)),
                  pl.BlockSpec((tm, LANES), lambda i, j: (i, 0)),
                  pl.BlockSpec((1, LANES), lambda i, j: (0, 0)),
                  pl.BlockSpec((D_MODEL, IN_TN), lambda i, j: (0, j))],
        out_specs=pl.BlockSpec((tm, IN_TN), lambda i, j: (i, j)),
        out_shape=jax.ShapeDtypeStruct((TOKENS, IN_COLS), BF16),
        scratch_shapes=[pltpu.VMEM((tm, D_MODEL), BF16),
                        pltpu.VMEM((tm, LANES), F32),
                        pltpu.VMEM((tm, LANES), F32),
                        pltpu.VMEM((tm, LANES), F32)],
        compiler_params=_params(("parallel", "arbitrary")),
        name="inproj",
    )(x3, g, sc, sh, posf, invf, w_bf)


def _rglru_kernel(xp_ref, xm_ref, xn_ref, cw_ref, cb_ref, wg_ref, ba_ref, bi_ref, lam_ref,
                  o_ref, a_scr, u_scr, h_scr):
    d = pl.program_id(1)
    t = pl.program_id(2)
    nt = pl.num_programs(2)
    te = t + d * (nt - 1 - 2 * t)
    rows = RG_TT * BATCH

    pm = (te > 0).astype(F32)
    nm = (te < nt - 1).astype(F32)
    xin = jnp.concatenate([xp_ref[...].astype(F32) * pm,
                           xm_ref[...].astype(F32),
                           xn_ref[...].astype(F32) * nm], axis=0)
    cw = cw_ref[...]
    xc = cb_ref[...]
    for k in range(RNN_CONV_W):
        xc = xc + xin[k:k + RG_TT] * cw[k:k + 1]
    x2 = xc.reshape(rows, RG_TC)

    g = jnp.dot(x2.astype(BF16), wg_ref[0, 0], preferred_element_type=F32)
    r = jax.nn.sigmoid(g[:, :RG_TC] + ba_ref[0])
    gi = jax.nn.sigmoid(g[:, RG_TC:] + bi_ref[0])
    z = -lam_ref[0]
    sp = jnp.maximum(z, 0.0) + jnp.log1p(jnp.exp(-jnp.abs(z)))
    log_a = -RG_C * r * sp
    a = jnp.exp(log_a)
    th = jnp.tanh(log_a)
    one_minus_a2 = -2.0 * th / (1.0 - th)
    u = jnp.sqrt(one_minus_a2) * (gi * x2)
    a_scr[...] = a.reshape(RG_TT, BATCH, RG_TC)
    u_scr[...] = u.reshape(RG_TT, BATCH, RG_TC)

    @pl.when(t == 0)
    def _():
        h_scr[...] = jnp.zeros_like(h_scr)

    def body(s, h):
        idx = s + d * (RG_TT - 1 - 2 * s)
        h = a_scr[idx] * h + u_scr[idx]
        o_ref[0, idx] = h.astype(BF16)
        return h

    h_scr[...] = lax.fori_loop(0, RG_TT, body, h_scr[...], unroll=8)


def _rglru(proj3, cw, cb, wg, ba, bi, lam):
    nt = SEQ // RG_TT

    def te(d, t):
        return t + d * (nt - 1 - 2 * t)

    return pl.pallas_call(
        _rglru_kernel,
        grid=(D_RNN // RG_TC, 2, nt),
        in_specs=[
            pl.BlockSpec((2, BATCH, RG_TC),
                         lambda c, d, t: (jnp.maximum(te(d, t) * (RG_TT // 2) - 1, 0), 0, c)),
            pl.BlockSpec((RG_TT, BATCH, RG_TC), lambda c, d, t: (te(d, t), 0, c)),
            pl.BlockSpec((1, BATCH, RG_TC),
                         lambda c, d, t: (jnp.minimum((te(d, t) + 1) * RG_TT, SEQ - 1), 0, c)),
            pl.BlockSpec((RNN_CONV_W, RG_TC), lambda c, d, t: (0, c)),
            pl.BlockSpec((1, RG_TC), lambda c, d, t: (0, c)),
            pl.BlockSpec((1, 1, RG_TC, 2 * RG_TC), lambda c, d, t: (d, c, 0, 0)),
            pl.BlockSpec((1, 1, RG_TC), lambda c, d, t: (d, 0, c)),
            pl.BlockSpec((1, 1, RG_TC), lambda c, d, t: (d, 0, c)),
            pl.BlockSpec((1, 1, RG_TC), lambda c, d, t: (d, 0, c)),
        ],
        out_specs=pl.BlockSpec((1, RG_TT, BATCH, RG_TC), lambda c, d, t: (d, te(d, t), 0, c)),
        out_shape=jax.ShapeDtypeStruct((2, SEQ, BATCH, D_RNN), BF16),
        scratch_shapes=[pltpu.VMEM((RG_TT, BATCH, RG_TC), F32),
                        pltpu.VMEM((RG_TT, BATCH, RG_TC), F32),
                        pltpu.VMEM((BATCH, RG_TC), F32)],
        compiler_params=_params(("parallel", "arbitrary", "arbitrary")),
        name="rglru",
    )(proj3, proj3, proj3, cw, cb, wg, ba, bi, lam)


def _attn_kernel(lamv_ref, g_ref, q_ref, k_ref, v_ref, o_ref):
    q = q_ref[...]
    k = k_ref[...]
    v = v_ref[...]
    lane = lax.broadcasted_iota(jnp.int32, q.shape, 1)
    zero = jnp.zeros_like(q)
    dn = (((1,), (1,)), ((), ()))

    def branch(qm):
        s = lax.dot_general(qm, k, dn, preferred_element_type=F32)
        m = jnp.max(s, axis=-1, keepdims=True)
        p = jnp.exp(s - m)
        l = jnp.sum(p, axis=-1, keepdims=True)
        o = jnp.dot(p.astype(BF16), v, preferred_element_type=F32)
        return o / l

    o1 = branch(jnp.where(lane < HEAD_DIM, q, zero))
    o2 = branch(jnp.where(lane >= HEAD_DIM, q, zero))
    lv = lamv_ref[...]
    lam = (jnp.exp(jnp.sum(lv[0:1] * lv[1:2], axis=-1, keepdims=True))
           - jnp.exp(jnp.sum(lv[2:3] * lv[3:4], axis=-1, keepdims=True)) + LAM_INIT)
    o = o1 - lam * o2
    ms = jnp.mean(o * o, axis=-1, keepdims=True)
    y = o * lax.rsqrt(ms + NORM_EPS) * g_ref[...]
    o_ref[...] = (y * (1.0 - LAM_INIT)).astype(BF16)


def _attn(proj2, lamv, subln_g):
    cb = IN_COLS // LANES
    hb = 2 * HEAD_DIM // LANES
    return pl.pallas_call(
        _attn_kernel,
        grid=(BATCH, N_HEADS, SEQ // AT_TQ),
        in_specs=[
            pl.BlockSpec((4, HEAD_DIM), lambda b, h, i: (0, 0)),
            pl.BlockSpec((1, V_DIM), lambda b, h, i: (0, 0)),
            pl.BlockSpec((AT_TQ, LANES), lambda b, h, i: (i, b * cb + 2 * N_HEADS * hb + h)),
            pl.BlockSpec((SEQ, LANES), lambda b, h, i: (0, b * cb + 3 * N_HEADS * hb + h)),
            pl.BlockSpec((SEQ, LANES), lambda b, h, i: (0, b * cb + 4 * N_HEADS * hb + h)),
        ],
        out_specs=pl.BlockSpec((AT_TQ, LANES), lambda b, h, i: (i, b * N_HEADS + h)),
        out_shape=jax.ShapeDtypeStruct((SEQ, BATCH * N_HEADS * V_DIM), BF16),
        compiler_params=_params(("parallel", "parallel", "arbitrary")),
        name="diffattn",
    )(lamv, subln_g, proj2, proj2, proj2)


def _merge_kernel(hf_ref, hb_ref, yr_ref, ga_ref, gb_ref, at_ref, x_ref, g1_ref,
                  wr_ref, wa_ref, wo_ref, o_ref):
    rows = MG_TS * BATCH
    hr = hf_ref[0].astype(F32) + hb_ref[0].astype(F32)
    ya = (hr * jax.nn.gelu(yr_ref[...].astype(F32))).reshape(rows, D_RNN).astype(BF16)
    br_a = jnp.dot(ya, wr_ref[...], preferred_element_type=F32)
    br_b = jnp.dot(at_ref[...].reshape(rows, D_MODEL), wa_ref[...], preferred_element_type=F32)
    ga = ga_ref[...].reshape(rows, D_MODEL).astype(F32)
    gb = gb_ref[...].reshape(rows, D_MODEL).astype(F32)
    merged = (ga * br_a + gb * br_b).astype(BF16)
    m = jnp.dot(merged, wo_ref[...], preferred_element_type=F32)
    o_ref[...] = x_ref[...] + g1_ref[...] * m.reshape(MG_TS, BATCH, D_MODEL)


def _merge(hfb, proj3, attn3, x3, g1, wr, wa, wo):
    tok = lambda cidx: pl.BlockSpec((MG_TS, BATCH, D_MODEL), lambda i: (i, 0, cidx))
    wspec = pl.BlockSpec((D_MODEL, D_MODEL), lambda i: (0, 0))
    return pl.pallas_call(
        _merge_kernel,
        grid=(SEQ // MG_TS,),
        in_specs=[pl.BlockSpec((1, MG_TS, BATCH, D_RNN), lambda i: (0, i, 0, 0)),
                  pl.BlockSpec((1, MG_TS, BATCH, D_RNN), lambda i: (1, i, 0, 0)),
                  tok(1), tok(5), tok(6), tok(0), tok(0),
                  pl.BlockSpec((BATCH, D_MODEL), lambda i: (0, 0)),
                  wspec, wspec, wspec],
        out_specs=tok(0),
        out_shape=jax.ShapeDtypeStruct((SEQ, BATCH, D_MODEL), F32),
        compiler_params=_params(("parallel",)),
        name="merge",
    )(hfb, hfb, proj3, proj3, proj3, attn3, x3, g1, wr, wa, wo)


def _ffn_kernel(xp_ref, xm_ref, xn_ref, g_ref, sc_ref, sh_ref, g2_ref, fg_ref,
                wu_ref, cw_ref, cb_ref, wd_ref, o_ref, h_scr, acc_scr):
    i = pl.program_id(0)
    n = pl.num_programs(0)
    rows = FF_TS * BATCH
    pm = (i > 0).astype(F32)
    nm = (i < n - 1).astype(F32)
    g, sc, sh = g_ref[...], sc_ref[...], sh_ref[...]
    xm = xm_ref[...]
    h_scr[0:1] = (_rms_mod(xp_ref[...], g, sc, sh) * pm).astype(BF16)
    h_scr[1:FF_TS + 1] = _rms_mod(xm, g, sc, sh).astype(BF16)
    h_scr[FF_TS + 1:FF_TS + 2] = (_rms_mod(xn_ref[...], g, sc, sh) * nm).astype(BF16)
    hx = h_scr[...].reshape((FF_TS + 2) * BATCH, D_MODEL)
    acc_scr[...] = jnp.zeros_like(acc_scr)

    def chunk(ci, carry):
        up = jnp.dot(hx, wu_ref[ci], preferred_element_type=F32)
        up = up.reshape(FF_TS + 2, BATCH, 2 * FF_CH)
        cw = cw_ref[ci]
        cv = cb_ref[ci]
        for k in range(3):
            cv = cv + up[k:k + FF_TS] * cw[k:k + 1]
        cv = cv.reshape(rows, 2 * FF_CH)
        val = cv[:, :FF_CH]
        gt = cv[:, FF_CH:]
        act = (gt * jax.nn.sigmoid(gt) * val).astype(BF16)
        acc_scr[...] += jnp.dot(act, wd_ref[ci], preferred_element_type=F32)
        return carry

    lax.fori_loop(0, FF_NCH, chunk, 0)
    x2 = xm + g2_ref[...] * acc_scr[...].reshape(FF_TS, BATCH, D_MODEL)
    ms = jnp.mean(x2 * x2, axis=-1, keepdims=True)
    o_ref[...] = x2 * lax.rsqrt(ms + NORM_EPS) * fg_ref[...]


def _ffn(x1, g, sc, sh, g2, fg, wu, cw, cb, wd):
    const = lambda shape: pl.BlockSpec(shape, lambda i: (0,) * len(shape),
                                       pipeline_mode=pl.Buffered(1))
    return pl.pallas_call(
        _ffn_kernel,
        grid=(SEQ // FF_TS,),
        in_specs=[
            pl.BlockSpec((1, BATCH, D_MODEL), lambda i: (jnp.maximum(i * FF_TS - 1, 0), 0, 0)),
            pl.BlockSpec((FF_TS, BATCH, D_MODEL), lambda i: (i, 0, 0)),
            pl.BlockSpec((1, BATCH, D_MODEL),
                         lambda i: (jnp.minimum((i + 1) * FF_TS, SEQ - 1), 0, 0)),
            const((1, D_MODEL)), const((BATCH, D_MODEL)), const((BATCH, D_MODEL)),
            const((BATCH, D_MODEL)), const((1, D_MODEL)),
            const((FF_NCH, D_MODEL, 2 * FF_CH)),
            const((FF_NCH, 3, 2 * FF_CH)),
            const((FF_NCH, 1, 2 * FF_CH)),
            const((FF_NCH, FF_CH, D_MODEL)),
        ],
        out_specs=pl.BlockSpec((FF_TS, BATCH, D_MODEL), lambda i: (i, 0, 0)),
        out_shape=jax.ShapeDtypeStruct((SEQ, BATCH, D_MODEL), F32),
        scratch_shapes=[pltpu.VMEM((FF_TS + 2, BATCH, D_MODEL), BF16),
                        pltpu.VMEM((FF_TS * BATCH, D_MODEL), F32)],
        compiler_params=_params(("parallel",)),
        name="ffn",
    )(x1, x1, x1, g, sc, sh, g2, fg, wu, cw, cb, wd)


def _block_diag_tiles(w):
    eye = jnp.eye(N_RNN_BLOCKS, dtype=w.dtype)
    full = jnp.einsum('dnkj,nm->dnkmj', w, eye).reshape(2, D_RNN, D_RNN)
    nt = D_RNN // RG_TC
    return jnp.stack([full[:, c * RG_TC:(c + 1) * RG_TC, c * RG_TC:(c + 1) * RG_TC]
                      for c in range(nt)], axis=1)


def _pair_chunks(a):
    val = a[..., :D_FF].reshape(a.shape[:-1] + (FF_NCH, FF_CH))
    gt = a[..., D_FF:].reshape(a.shape[:-1] + (FF_NCH, FF_CH))
    both = jnp.concatenate([val, gt], axis=-1)
    return jnp.moveaxis(both, -2, 0)


def kernel(x, c, positions, w_ada, b_ada, norm1_g, w_in, conv_rnn_w, conv_rnn_b, w_rg_a, b_rg_a,
           w_rg_i, b_rg_i, rg_lambda, w_rnn_o, lam_q1, lam_k1, lam_q2, lam_k2, subln_g, w_attn_o,
           w_out, norm2_g, w_up, conv_ffn_w, conv_ffn_b, w_down, final_g):
    l = 0
    x3 = x.transpose(1, 0, 2)
    pos_sb = positions.T.reshape(TOKENS, 1).astype(F32)
    posf = jnp.broadcast_to(pos_sb, (TOKENS, LANES))
    inv_freq = ROPE_THETA ** (-jnp.arange(0, ROPE_DIM, 2, dtype=F32) / ROPE_DIM)
    invf = jnp.tile(inv_freq, LANES // ROPE_HALF).reshape(1, LANES)

    mod = _ada(c, w_ada[l], b_ada[l])
    sh1, sc1, g1, sh2, sc2, g2 = [mod[:, m * D_MODEL:(m + 1) * D_MODEL] for m in range(N_MOD)]

    proj = _inproj(x3, norm1_g[l].reshape(1, D_MODEL), sc1, sh1, posf, invf, w_in[l].astype(BF16))
    proj3 = proj.reshape(SEQ, BATCH, IN_COLS)
    proj2 = proj.reshape(SEQ, BATCH * IN_COLS)

    wg = jnp.concatenate([_block_diag_tiles(w_rg_a[l]), _block_diag_tiles(w_rg_i[l])],
                         axis=-1).astype(BF16)
    hfb = _rglru(proj3, conv_rnn_w[l], conv_rnn_b[l].reshape(1, D_RNN), wg,
                 b_rg_a[l].reshape(2, 1, D_RNN), b_rg_i[l].reshape(2, 1, D_RNN),
                 rg_lambda[l].reshape(2, 1, D_RNN))

    lamv = jnp.stack([lam_q1[l], lam_k1[l], lam_q2[l], lam_k2[l]]).astype(F32)
    attn2 = _attn(proj2, lamv, subln_g[l].reshape(1, V_DIM))
    attn3 = attn2.reshape(SEQ, BATCH, N_HEADS * V_DIM)

    x1 = _merge(hfb, proj3, attn3, x3, g1, w_rnn_o[l].astype(BF16), w_attn_o[l].astype(BF16),
                w_out[l].astype(BF16))

    out = _ffn(x1, norm2_g[l].reshape(1, D_MODEL), sc2, sh2, g2, final_g.reshape(1, D_MODEL),
               _pair_chunks(w_up[l]).astype(BF16), _pair_chunks(conv_ffn_w[l]),
               _pair_chunks(conv_ffn_b[l].reshape(1, 2 * D_FF)),
               w_down[l].reshape(FF_NCH, FF_CH, D_MODEL).astype(BF16))
    return out.transpose(1, 0, 2)
```

```python
import functools
import math

import jax
import jax.numpy as jnp
from jax import lax
from jax.experimental import pallas as pl
from jax.experimental.pallas import tpu as pltpu

F32 = jnp.float32
BF16 = jnp.bfloat16

D_MODEL = 1024
BATCH = 16
SEQ = 2048
TOKENS = BATCH * SEQ
D_RNN = D_MODEL
N_RNN_BLOCKS = 16
RNN_BLOCK = D_RNN // N_RNN_BLOCKS
RNN_CONV_W = 4
RNN_CONV_LEFT = 2
RG_C = 8.0
N_HEADS = 8
HEAD_DIM = 64
V_DIM = 2 * HEAD_DIM
ROPE_DIM = HEAD_DIM // 4
ROPE_HALF = ROPE_DIM // 2
ROPE_THETA = 500000.0
D_FF = 2816
N_MOD = 6
NORM_EPS = 1e-6
IN_COLS = 7 * D_MODEL
LAM_INIT = 0.8 - 0.6 * math.exp(-0.3 * 0)

LANES = 128
VMEM_LIMIT = 52 * 1024 * 1024

ADA_TN = 1024
IN_TS = 64
IN_TN = 1024
QKV_TM = 1024
RG_TT = 128
RG_TC = 256
AT_TQ = 512
MG_TS = 32
FF_TS = 32
FF_CH = 256
FF_NCH = D_FF // FF_CH


def _params(sem):
    return pltpu.CompilerParams(dimension_semantics=sem, vmem_limit_bytes=VMEM_LIMIT)


def _ada_kernel(c_ref, w_ref, b_ref, o_ref):
    c = c_ref[...]
    ca = c * jax.nn.sigmoid(c)
    o_ref[...] = jnp.dot(ca, w_ref[...], preferred_element_type=F32,
                         precision=lax.Precision.HIGHEST) + b_ref[...]


def _ada(c, w, b):
    n = w.shape[1]
    return pl.pallas_call(
        _ada_kernel,
        grid=(n // ADA_TN,),
        in_specs=[pl.BlockSpec((BATCH, D_MODEL), lambda j: (0, 0)),
                  pl.BlockSpec((D_MODEL, ADA_TN), lambda j: (0, j)),
                  pl.BlockSpec((1, ADA_TN), lambda j: (0, j))],
        out_specs=pl.BlockSpec((BATCH, ADA_TN), lambda j: (0, j)),
        out_shape=jax.ShapeDtypeStruct((BATCH, n), F32),
        compiler_params=_params(("arbitrary",)),
        name="adaln",
    )(c, w, b.reshape(1, n))


def _rms_mod(x, g, sc, sh):
    ms = jnp.mean(x * x, axis=-1, keepdims=True)
    y = x * lax.rsqrt(ms + NORM_EPS) * g
    return y * (1.0 + sc) + sh


def _inproj_sb_kernel(x_ref, g_ref, sc_ref, sh_ref, w_ref, o_ref, h_scr):
    j = pl.program_id(1)

    @pl.when(j == 0)
    def _():
        h = _rms_mod(x_ref[...], g_ref[...], sc_ref[...], sh_ref[...])
        h_scr[...] = h.reshape(IN_TS * BATCH, D_MODEL).astype(BF16)

    acc = jnp.dot(h_scr[...], w_ref[...], preferred_element_type=F32)

    @pl.when(j < 2)
    def _():
        o_ref[...] = acc.astype(BF16)

    @pl.when(j >= 2)
    def _():
        o_ref[...] = jax.nn.sigmoid(acc).astype(BF16)


def _inproj_sb(x3, g, sc, sh, w_bf):
    tm = IN_TS * BATCH
    ncol = w_bf.shape[1]
    return pl.pallas_call(
        _inproj_sb_kernel,
        grid=(SEQ // IN_TS, ncol // IN_TN),
        in_specs=[pl.BlockSpec((IN_TS, BATCH, D_MODEL), lambda i, j: (i, 0, 0)),
                  pl.BlockSpec((1, D_MODEL), lambda i, j: (0, 0)),
                  pl.BlockSpec((BATCH, D_MODEL), lambda i, j: (0, 0)),
                  pl.BlockSpec((BATCH, D_MODEL), lambda i, j: (0, 0)),
                  pl.BlockSpec((D_MODEL, IN_TN), lambda i, j: (0, j))],
        out_specs=pl.BlockSpec((tm, IN_TN), lambda i, j: (i, j)),
        out_shape=jax.ShapeDtypeStruct((TOKENS, ncol), BF16),
        scratch_shapes=[pltpu.VMEM((tm, D_MODEL), BF16)],
        compiler_params=_params(("parallel", "arbitrary")),
        name="inproj_sb",
    )(x3, g, sc, sh, w_bf)


def _inproj_bs_kernel(x_ref, g_ref, sc_ref, sh_ref, pos_ref, invf_ref, w_ref, o_ref,
                      h_scr, cos_scr, sa_scr, sb_scr):
    j = pl.program_id(2)

    @pl.when(j == 0)
    def _():
        h = _rms_mod(x_ref[0], g_ref[...], sc_ref[0], sh_ref[0])
        h_scr[...] = h.astype(BF16)
        ang = pos_ref[0] * invf_ref[...]
        c = jnp.cos(ang)
        s = jnp.sin(ang)
        lane = lax.broadcasted_iota(jnp.int32, ang.shape, 1) % HEAD_DIM
        cos_scr[...] = jnp.where(lane < ROPE_DIM, c, 1.0)
        sa_scr[...] = jnp.where(lane < ROPE_HALF, -s, 0.0)
        sb_scr[...] = jnp.where((lane >= ROPE_HALF) & (lane < ROPE_DIM), s, 0.0)

    acc = jnp.dot(h_scr[...], w_ref[...], preferred_element_type=F32)

    def rope_store(scale):
        ct, sa, sb = cos_scr[...], sa_scr[...], sb_scr[...]
        for cidx in range(IN_TN // LANES):
            sl = slice(cidx * LANES, (cidx + 1) * LANES)
            t = acc[:, sl]
            r = (t * ct + pltpu.roll(t, LANES - ROPE_HALF, 1) * sa
                 + pltpu.roll(t, ROPE_HALF, 1) * sb)
            o_ref[0, :, sl] = (r * scale).astype(BF16)

    @pl.when(j == 0)
    def _():
        rope_store(HEAD_DIM ** -0.5)

    @pl.when(j == 1)
    def _():
        rope_store(1.0)

    @pl.when(j == 2)
    def _():
        o_ref[0] = acc.astype(BF16)


def _inproj_bs(x, g, sc, sh, posf, invf, w_bf):
    ncol = w_bf.shape[1]
    return pl.pallas_call(
        _inproj_bs_kernel,
        grid=(BATCH, SEQ // QKV_TM, ncol // IN_TN),
        in_specs=[pl.BlockSpec((1, QKV_TM, D_MODEL), lambda b, i, j: (b, i, 0)),
                  pl.BlockSpec((1, D_MODEL), lambda b, i, j: (0, 0)),
                  pl.BlockSpec((1, 1, D_MODEL), lambda b, i, j: (b, 0, 0)),
                  pl.BlockSpec((1, 1, D_MODEL), lambda b, i, j: (b, 0, 0)),
                  pl.BlockSpec((1, QKV_TM, LANES), lambda b, i, j: (b, i, 0)),
                  pl.BlockSpec((1, LANES), lambda b, i, j: (0, 0)),
                  pl.BlockSpec((D_MODEL, IN_TN), lambda b, i, j: (0, j))],
        out_specs=pl.BlockSpec((1, QKV_TM, IN_TN), lambda b, i, j: (b, i, j)),
        out_shape=jax.ShapeDtypeStruct((BATCH, SEQ, ncol), BF16),
        scratch_shapes=[pltpu.VMEM((QKV_TM, D_MODEL), BF16),
                        pltpu.VMEM((QKV_TM, LANES), F32),
                        pltpu.VMEM((QKV_TM, LANES), F32),
                        pltpu.VMEM((QKV_TM, LANES), F32)],
        compiler_params=_params(("parallel", "parallel", "arbitrary")),
        name="inproj_bs",
    )(x, g, sc, sh, posf, invf, w_bf)


def _rglru_kernel(xp_ref, xm_ref, xn_ref, cw_ref, cb_ref, wg_ref, ba_ref, bi_ref, lam_ref,
                  o_ref, a_scr, u_scr, h_scr):
    d = pl.program_id(1)
    t = pl.program_id(2)
    nt = pl.num_programs(2)
    te = t + d * (nt - 1 - 2 * t)
    rows = RG_TT * BATCH

    pm = (te > 0).astype(F32)
    nm = (te < nt - 1).astype(F32)
    xin = jnp.concatenate([xp_ref[...].astype(F32) * pm,
                           xm_ref[...].astype(F32),
                           xn_ref[...].astype(F32) * nm], axis=0)
    cw = cw_ref[...]
    xc = cb_ref[...]
    for k in range(RNN_CONV_W):
        xc = xc + xin[k:k + RG_TT] * cw[k:k + 1]
    x2 = xc.reshape(rows, RG_TC)

    g = jnp.dot(x2.astype(BF16), wg_ref[0, 0], preferred_element_type=F32)
    r = jax.nn.sigmoid(g[:, :RG_TC] + ba_ref[0])
    gi = jax.nn.sigmoid(g[:, RG_TC:] + bi_ref[0])
    z = -lam_ref[0]
    sp = jnp.maximum(z, 0.0) + jnp.log1p(jnp.exp(-jnp.abs(z)))
    log_a = -RG_C * r * sp
    a = jnp.exp(log_a)
    th = jnp.tanh(log_a)
    one_minus_a2 = -2.0 * th / (1.0 - th)
    u = jnp.sqrt(one_minus_a2) * (gi * x2)
    a_scr[...] = a.reshape(RG_TT, BATCH, RG_TC)
    u_scr[...] = u.reshape(RG_TT, BATCH, RG_TC)

    @pl.when(t == 0)
    def _():
        h_scr[...] = jnp.zeros_like(h_scr)

    def body(s, h):
        idx = s + d * (RG_TT - 1 - 2 * s)
        h = a_scr[idx] * h + u_scr[idx]
        o_ref[0, idx] = h.astype(BF16)
        return h

    h_scr[...] = lax.fori_loop(0, RG_TT, body, h_scr[...], unroll=8)


def _rglru(proj3, cw, cb, wg, ba, bi, lam):
    nt = SEQ // RG_TT

    def te(d, t):
        return t + d * (nt - 1 - 2 * t)

    return pl.pallas_call(
        _rglru_kernel,
        grid=(D_RNN // RG_TC, 2, nt),
        in_specs=[
            pl.BlockSpec((2, BATCH, RG_TC),
                         lambda c, d, t: (jnp.maximum(te(d, t) * (RG_TT // 2) - 1, 0), 0, c)),
            pl.BlockSpec((RG_TT, BATCH, RG_TC), lambda c, d, t: (te(d, t), 0, c)),
            pl.BlockSpec((1, BATCH, RG_TC),
                         lambda c, d, t: (jnp.minimum((te(d, t) + 1) * RG_TT, SEQ - 1), 0, c)),
            pl.BlockSpec((RNN_CONV_W, RG_TC), lambda c, d, t: (0, c)),
            pl.BlockSpec((1, RG_TC), lambda c, d, t: (0, c)),
            pl.BlockSpec((1, 1, RG_TC, 2 * RG_TC), lambda c, d, t: (d, c, 0, 0)),
            pl.BlockSpec((1, 1, RG_TC), lambda c, d, t: (d, 0, c)),
            pl.BlockSpec((1, 1, RG_TC), lambda c, d, t: (d, 0, c)),
            pl.BlockSpec((1, 1, RG_TC), lambda c, d, t: (d, 0, c)),
        ],
        out_specs=pl.BlockSpec((1, RG_TT, BATCH, RG_TC), lambda c, d, t: (d, te(d, t), 0, c)),
        out_shape=jax.ShapeDtypeStruct((2, SEQ, BATCH, D_RNN), BF16),
        scratch_shapes=[pltpu.VMEM((RG_TT, BATCH, RG_TC), F32),
                        pltpu.VMEM((RG_TT, BATCH, RG_TC), F32),
                        pltpu.VMEM((BATCH, RG_TC), F32)],
        compiler_params=_params(("parallel", "arbitrary", "arbitrary")),
        name="rglru",
    )(proj3, proj3, proj3, cw, cb, wg, ba, bi, lam)


def _attn_kernel(lamv_ref, g_ref, q_ref, k_ref, v_ref, o_ref):
    q = q_ref[0]
    k = k_ref[0]
    v = v_ref[0]
    lane = lax.broadcasted_iota(jnp.int32, q.shape, 1)
    zero = jnp.zeros_like(q)
    dn = (((1,), (1,)), ((), ()))

    def branch(qm):
        s = lax.dot_general(qm, k, dn, preferred_element_type=F32)
        m = jnp.max(s, axis=-1, keepdims=True)
        p = jnp.exp(s - m)
        l = jnp.sum(p, axis=-1, keepdims=True)
        o = jnp.dot(p.astype(BF16), v, preferred_element_type=F32)
        return o / l

    o1 = branch(jnp.where(lane < HEAD_DIM, q, zero))
    o2 = branch(jnp.where(lane >= HEAD_DIM, q, zero))
    lv = lamv_ref[...]
    lam = (jnp.exp(jnp.sum(lv[0:1] * lv[1:2], axis=-1, keepdims=True))
           - jnp.exp(jnp.sum(lv[2:3] * lv[3:4], axis=-1, keepdims=True)) + LAM_INIT)
    o = o1 - lam * o2
    ms = jnp.mean(o * o, axis=-1, keepdims=True)
    y = o * lax.rsqrt(ms + NORM_EPS) * g_ref[...]
    o_ref[0] = (y * (1.0 - LAM_INIT)).astype(BF16)


def _attn(qkv, lamv, subln_g):
    return pl.pallas_call(
        _attn_kernel,
        grid=(BATCH, N_HEADS, SEQ // AT_TQ),
        in_specs=[
            pl.BlockSpec((4, HEAD_DIM), lambda b, h, i: (0, 0)),
            pl.BlockSpec((1, V_DIM), lambda b, h, i: (0, 0)),
            pl.BlockSpec((1, AT_TQ, LANES), lambda b, h, i: (b, i, h)),
            pl.BlockSpec((1, SEQ, LANES), lambda b, h, i: (b, 0, N_HEADS + h)),
            pl.BlockSpec((1, SEQ, LANES), lambda b, h, i: (b, 0, 2 * N_HEADS + h)),
        ],
        out_specs=pl.BlockSpec((1, AT_TQ, LANES), lambda b, h, i: (b, i, h)),
        out_shape=jax.ShapeDtypeStruct((BATCH, SEQ, N_HEADS * V_DIM), BF16),
        compiler_params=_params(("parallel", "parallel", "arbitrary")),
        name="diffattn",
    )(lamv, subln_g, qkv, qkv, qkv)


def _merge_kernel(hf_ref, hb_ref, yr_ref, ga_ref, gb_ref, at_ref, x_ref, g1_ref,
                  wr_ref, wa_ref, wo_ref, o_ref):
    rows = MG_TS * BATCH
    hr = hf_ref[0].astype(F32) + hb_ref[0].astype(F32)
    ya = (hr * jax.nn.gelu(yr_ref[...].astype(F32))).reshape(rows, D_RNN).astype(BF16)
    br_a = jnp.dot(ya, wr_ref[...], preferred_element_type=F32)
    br_b = jnp.dot(at_ref[...].reshape(rows, D_MODEL), wa_ref[...], preferred_element_type=F32)
    ga = ga_ref[...].reshape(rows, D_MODEL).astype(F32)
    gb = gb_ref[...].reshape(rows, D_MODEL).astype(F32)
    merged = (ga * br_a + gb * br_b).astype(BF16)
    m = jnp.dot(merged, wo_ref[...], preferred_element_type=F32)
    o_ref[...] = x_ref[...] + g1_ref[...] * m.reshape(MG_TS, BATCH, D_MODEL)


def _merge(hfb, proj3, attn3, x3, g1, wr, wa, wo):
    tok = lambda cidx: pl.BlockSpec((MG_TS, BATCH, D_MODEL), lambda i: (i, 0, cidx))
    wspec = pl.BlockSpec((D_MODEL, D_MODEL), lambda i: (0, 0))
    return pl.pallas_call(
        _merge_kernel,
        grid=(SEQ // MG_TS,),
        in_specs=[pl.BlockSpec((1, MG_TS, BATCH, D_RNN), lambda i: (0, i, 0, 0)),
                  pl.BlockSpec((1, MG_TS, BATCH, D_RNN), lambda i: (1, i, 0, 0)),
                  tok(1), tok(2), tok(3), tok(0), tok(0),
                  pl.BlockSpec((BATCH, D_MODEL), lambda i: (0, 0)),
                  wspec, wspec, wspec],
        out_specs=tok(0),
        out_shape=jax.ShapeDtypeStruct((SEQ, BATCH, D_MODEL), F32),
        compiler_params=_params(("parallel",)),
        name="merge",
    )(hfb, hfb, proj3, proj3, proj3, attn3, x3, g1, wr, wa, wo)


def _ffn_kernel(xp_ref, xm_ref, xn_ref, g_ref, sc_ref, sh_ref, g2_ref, fg_ref,
                wu_ref, cw_ref, cb_ref, wd_ref, o_ref, h_scr, acc_scr):
    i = pl.program_id(0)
    n = pl.num_programs(0)
    rows = FF_TS * BATCH
    pm = (i > 0).astype(F32)
    nm = (i < n - 1).astype(F32)
    g, sc, sh = g_ref[...], sc_ref[...], sh_ref[...]
    xm = xm_ref[...]
    h_scr[0:1] = (_rms_mod(xp_ref[...], g, sc, sh) * pm).astype(BF16)
    h_scr[1:FF_TS + 1] = _rms_mod(xm, g, sc, sh).astype(BF16)
    h_scr[FF_TS + 1:FF_TS + 2] = (_rms_mod(xn_ref[...], g, sc, sh) * nm).astype(BF16)
    hx = h_scr[...].reshape((FF_TS + 2) * BATCH, D_MODEL)
    acc_scr[...] = jnp.zeros_like(acc_scr)

    def chunk(ci, carry):
        up = jnp.dot(hx, wu_ref[ci], preferred_element_type=F32)
        up = up.reshape(FF_TS + 2, BATCH, 2 * FF_CH)
        cw = cw_ref[ci]
        cv = cb_ref[ci]
        for k in range(3):
            cv = cv + up[k:k + FF_TS] * cw[k:k + 1]
        cv = cv.reshape(rows, 2 * FF_CH)
        val = cv[:, :FF_CH]
        gt = cv[:, FF_CH:]
        act = (gt * jax.nn.sigmoid(gt) * val).astype(BF16)
        acc_scr[...] += jnp.dot(act, wd_ref[ci], preferred_element_type=F32)
        return carry

    lax.fori_loop(0, FF_NCH, chunk, 0)
    x2 = xm + g2_ref[...] * acc_scr[...].reshape(FF_TS, BATCH, D_MODEL)
    ms = jnp.mean(x2 * x2, axis=-1, keepdims=True)
    o_ref[...] = x2 * lax.rsqrt(ms + NORM_EPS) * fg_ref[...]


def _ffn(x1, g, sc, sh, g2, fg, wu, cw, cb, wd):
    const = lambda shape: pl.BlockSpec(shape, lambda i: (0,) * len(shape),
                                       pipeline_mode=pl.Buffered(1))
    return pl.pallas_call(
        _ffn_kernel,
        grid=(SEQ // FF_TS,),
        in_specs=[
            pl.BlockSpec((1, BATCH, D_MODEL), lambda i: (jnp.maximum(i * FF_TS - 1, 0), 0, 0)),
            pl.BlockSpec((FF_TS, BATCH, D_MODEL), lambda i: (i, 0, 0)),
            pl.BlockSpec((1, BATCH, D_MODEL),
                         lambda i: (jnp.minimum((i + 1) * FF_TS, SEQ - 1), 0, 0)),
            const((1, D_MODEL)), const((BATCH, D_MODEL)), const((BATCH, D_MODEL)),
            const((BATCH, D_MODEL)), const((1, D_MODEL)),
            const((FF_NCH, D_MODEL, 2 * FF_CH)),
            const((FF_NCH, 3, 2 * FF_CH)),
            const((FF_NCH, 1, 2 * FF_CH)),
            const((FF_NCH, FF_CH, D_MODEL)),
        ],
        out_specs=pl.BlockSpec((FF_TS, BATCH, D_MODEL), lambda i: (i, 0, 0)),
        out_shape=jax.ShapeDtypeStruct((SEQ, BATCH, D_MODEL), F32),
        scratch_shapes=[pltpu.VMEM((FF_TS + 2, BATCH, D_MODEL), BF16),
                        pltpu.VMEM((FF_TS * BATCH, D_MODEL), F32)],
        compiler_params=_params(("parallel",)),
        name="ffn",
    )(x1, x1, x1, g, sc, sh, g2, fg, wu, cw, cb, wd)


def _block_diag_tiles(w):
    eye = jnp.eye(N_RNN_BLOCKS, dtype=w.dtype)
    full = jnp.einsum('dnkj,nm->dnkmj', w, eye).reshape(2, D_RNN, D_RNN)
    nt = D_RNN // RG_TC
    return jnp.stack([full[:, c * RG_TC:(c + 1) * RG_TC, c * RG_TC:(c + 1) * RG_TC]
                      for c in range(nt)], axis=1)


def _pair_chunks(a):
    val = a[..., :D_FF].reshape(a.shape[:-1] + (FF_NCH, FF_CH))
    gt = a[..., D_FF:].reshape(a.shape[:-1] + (FF_NCH, FF_CH))
    both = jnp.concatenate([val, gt], axis=-1)
    return jnp.moveaxis(both, -2, 0)


def kernel(x, c, positions, w_ada, b_ada, norm1_g, w_in, conv_rnn_w, conv_rnn_b, w_rg_a, b_rg_a,
           w_rg_i, b_rg_i, rg_lambda, w_rnn_o, lam_q1, lam_k1, lam_q2, lam_k2, subln_g, w_attn_o,
           w_out, norm2_g, w_up, conv_ffn_w, conv_ffn_b, w_down, final_g):
    l = 0
    x3 = x.transpose(1, 0, 2)
    posf = jnp.broadcast_to(positions.astype(F32)[:, :, None], (BATCH, SEQ, LANES))
    inv_freq = ROPE_THETA ** (-jnp.arange(0, ROPE_DIM, 2, dtype=F32) / ROPE_DIM)
    invf = jnp.tile(inv_freq, LANES // ROPE_HALF).reshape(1, LANES)

    mod = _ada(c, w_ada[l], b_ada[l])
    sh1, sc1, g1, sh2, sc2, g2 = [mod[:, m * D_MODEL:(m + 1) * D_MODEL] for m in range(N_MOD)]

    w_in_bf = w_in[l].astype(BF16)
    g_n1 = norm1_g[l].reshape(1, D_MODEL)
    w_sb = jnp.concatenate([w_in_bf[:, :2 * D_MODEL], w_in_bf[:, 5 * D_MODEL:]], axis=1)
    w_bs = w_in_bf[:, 2 * D_MODEL:5 * D_MODEL]
    proj3 = _inproj_sb(x3, g_n1, sc1, sh1, w_sb).reshape(SEQ, BATCH, 4 * D_MODEL)
    qkv = _inproj_bs(x, g_n1, sc1.reshape(BATCH, 1, D_MODEL), sh1.reshape(BATCH, 1, D_MODEL),
                     posf, invf, w_bs)

    wg = jnp.concatenate([_block_diag_tiles(w_rg_a[l]), _block_diag_tiles(w_rg_i[l])],
                         axis=-1).astype(BF16)
    hfb = _rglru(proj3, conv_rnn_w[l], conv_rnn_b[l].reshape(1, D_RNN), wg,
                 b_rg_a[l].reshape(2, 1, D_RNN), b_rg_i[l].reshape(2, 1, D_RNN),
                 rg_lambda[l].reshape(2, 1, D_RNN))

    lamv = jnp.stack([lam_q1[l], lam_k1[l], lam_q2[l], lam_k2[l]]).astype(F32)
    attn3 = _attn(qkv, lamv, subln_g[l].reshape(1, V_DIM)).transpose(1, 0, 2)

    x1 = _merge(hfb, proj3, attn3, x3, g1, w_rnn_o[l].astype(BF16), w_attn_o[l].astype(BF16),
                w_out[l].astype(BF16))

    out = _ffn(x1, norm2_g[l].reshape(1, D_MODEL), sc2, sh2, g2, final_g.reshape(1, D_MODEL),
               _pair_chunks(w_up[l]).astype(BF16), _pair_chunks(conv_ffn_w[l]),
               _pair_chunks(conv_ffn_b[l].reshape(1, 2 * D_FF)),
               w_down[l].reshape(FF_NCH, FF_CH, D_MODEL).astype(BF16))
    return out.transpose(1, 0, 2)
```

```python
import functools
import math

import jax
import jax.numpy as jnp
from jax import lax
from jax.experimental import pallas as pl
from jax.experimental.pallas import tpu as pltpu

F32 = jnp.float32
BF16 = jnp.bfloat16

D_MODEL = 1024
BATCH = 16
SEQ = 2048
TOKENS = BATCH * SEQ
D_RNN = D_MODEL
N_RNN_BLOCKS = 16
RNN_BLOCK = D_RNN // N_RNN_BLOCKS
RNN_CONV_W = 4
RNN_CONV_LEFT = 2
RG_C = 8.0
N_HEADS = 8
HEAD_DIM = 64
V_DIM = 2 * HEAD_DIM
ROPE_DIM = HEAD_DIM // 4
ROPE_HALF = ROPE_DIM // 2
ROPE_THETA = 500000.0
D_FF = 2816
N_MOD = 6
NORM_EPS = 1e-6
IN_COLS = 7 * D_MODEL
LAM_INIT = 0.8 - 0.6 * math.exp(-0.3 * 0)
Q_SCALE = HEAD_DIM ** -0.5 * math.log2(math.e)

LANES = 128
VMEM_LIMIT = 52 * 1024 * 1024

ADA_TN = 1024
IN_TS = 64
IN_TN = 1024
QKV_TM = 1024
RG_TT = 128
RG_TC = 256
AT_TQ = 512
AT_KB = 256
AT_VROWS = V_DIM + 16
AT_PAIRS_PER_HEAD = SEQ // (2 * AT_TQ)
MG_TS = 32
FF_TS = 32
FF_CH = 256
FF_NCH = D_FF // FF_CH


def _params(sem, flags=None):
    return pltpu.CompilerParams(dimension_semantics=sem, vmem_limit_bytes=VMEM_LIMIT, flags=flags)


def _ada_kernel(c_ref, w_ref, b_ref, o_ref):
    c = c_ref[...]
    ca = c * jax.nn.sigmoid(c)
    o_ref[...] = jnp.dot(ca, w_ref[...], preferred_element_type=F32,
                         precision=lax.Precision.HIGHEST) + b_ref[...]


def _ada(c, w, b):
    n = w.shape[1]
    return pl.pallas_call(
        _ada_kernel,
        grid=(n // ADA_TN,),
        in_specs=[pl.BlockSpec((BATCH, D_MODEL), lambda j: (0, 0)),
                  pl.BlockSpec((D_MODEL, ADA_TN), lambda j: (0, j)),
                  pl.BlockSpec((1, ADA_TN), lambda j: (0, j))],
        out_specs=pl.BlockSpec((BATCH, ADA_TN), lambda j: (0, j)),
        out_shape=jax.ShapeDtypeStruct((BATCH, n), F32),
        compiler_params=_params(("arbitrary",)),
        name="adaln",
    )(c, w, b.reshape(1, n))


def _rms_mod(x, g, sc, sh):
    ms = jnp.mean(x * x, axis=-1, keepdims=True)
    y = x * lax.rsqrt(ms + NORM_EPS) * g
    return y * (1.0 + sc) + sh


def _inproj_sb_kernel(x_ref, g_ref, sc_ref, sh_ref, w_ref, o_ref, h_scr):
    j = pl.program_id(1)

    @pl.when(j == 0)
    def _():
        h = _rms_mod(x_ref[...], g_ref[...], sc_ref[...], sh_ref[...])
        h_scr[...] = h.reshape(IN_TS * BATCH, D_MODEL).astype(BF16)

    o_ref[...] = jnp.dot(h_scr[...], w_ref[...], preferred_element_type=F32).astype(BF16)


def _inproj_sb(x3, g, sc, sh, w_bf):
    tm = IN_TS * BATCH
    ncol = w_bf.shape[1]
    return pl.pallas_call(
        _inproj_sb_kernel,
        grid=(SEQ // IN_TS, ncol // IN_TN),
        in_specs=[pl.BlockSpec((IN_TS, BATCH, D_MODEL), lambda i, j: (i, 0, 0)),
                  pl.BlockSpec((1, D_MODEL), lambda i, j: (0, 0)),
                  pl.BlockSpec((BATCH, D_MODEL), lambda i, j: (0, 0)),
                  pl.BlockSpec((BATCH, D_MODEL), lambda i, j: (0, 0)),
                  pl.BlockSpec((D_MODEL, IN_TN), lambda i, j: (0, j))],
        out_specs=pl.BlockSpec((tm, IN_TN), lambda i, j: (i, j)),
        out_shape=jax.ShapeDtypeStruct((TOKENS, ncol), BF16),
        scratch_shapes=[pltpu.VMEM((tm, D_MODEL), BF16)],
        compiler_params=_params(("parallel", "arbitrary")),
        name="inproj_sb",
    )(x3, g, sc, sh, w_bf)


def _inproj_bs_kernel(x_ref, g_ref, sc_ref, sh_ref, pos_ref, invf_ref, w_ref, o_ref,
                      h_scr, tab_scr):
    j = pl.program_id(2)

    @pl.when(j == 0)
    def _():
        h = _rms_mod(x_ref[0], g_ref[...], sc_ref[0], sh_ref[0])
        h_scr[...] = h.astype(BF16)
        ang = pos_ref[0] * invf_ref[...]
        c = jnp.cos(ang)
        s = jnp.sin(ang)
        lane = lax.broadcasted_iota(jnp.int32, ang.shape, 1) % HEAD_DIM
        ct = jnp.where(lane < ROPE_DIM, c, 1.0)
        sa = jnp.where(lane < ROPE_HALF, -s, 0.0)
        sb = jnp.where((lane >= ROPE_HALF) & (lane < ROPE_DIM), s, 0.0)
        tab_scr[0, 0] = ct * Q_SCALE
        tab_scr[0, 1] = sa * Q_SCALE
        tab_scr[0, 2] = sb * Q_SCALE
        tab_scr[1, 0] = ct
        tab_scr[1, 1] = sa
        tab_scr[1, 2] = sb
        tab_scr[2, 0] = jnp.ones_like(ct)
        tab_scr[2, 1] = jnp.zeros_like(ct)
        tab_scr[2, 2] = jnp.zeros_like(ct)

    acc = jnp.dot(h_scr[...], w_ref[...], preferred_element_type=F32)
    ct, sa, sb = tab_scr[j, 0], tab_scr[j, 1], tab_scr[j, 2]
    for cidx in range(IN_TN // LANES):
        sl = slice(cidx * LANES, (cidx + 1) * LANES)
        t = acc[:, sl]
        r = (t * ct + pltpu.roll(t, LANES - ROPE_HALF, 1) * sa
             + pltpu.roll(t, ROPE_HALF, 1) * sb)
        o_ref[0, :, sl] = r.astype(BF16)


def _inproj_bs(x, g, sc, sh, posf, invf, w_bf):
    ncol = w_bf.shape[1]
    return pl.pallas_call(
        _inproj_bs_kernel,
        grid=(BATCH, SEQ // QKV_TM, ncol // IN_TN),
        in_specs=[pl.BlockSpec((1, QKV_TM, D_MODEL), lambda b, i, j: (b, i, 0)),
                  pl.BlockSpec((1, D_MODEL), lambda b, i, j: (0, 0)),
                  pl.BlockSpec((1, 1, D_MODEL), lambda b, i, j: (b, 0, 0)),
                  pl.BlockSpec((1, 1, D_MODEL), lambda b, i, j: (b, 0, 0)),
                  pl.BlockSpec((1, QKV_TM, LANES), lambda b, i, j: (b, i, 0)),
                  pl.BlockSpec((1, LANES), lambda b, i, j: (0, 0)),
                  pl.BlockSpec((D_MODEL, IN_TN), lambda b, i, j: (0, j))],
        out_specs=pl.BlockSpec((1, QKV_TM, IN_TN), lambda b, i, j: (b, i, j)),
        out_shape=jax.ShapeDtypeStruct((BATCH, SEQ, ncol), BF16),
        scratch_shapes=[pltpu.VMEM((QKV_TM, D_MODEL), BF16),
                        pltpu.VMEM((3, 3, QKV_TM, LANES), F32)],
        compiler_params=_params(("parallel", "parallel", "arbitrary")),
        name="inproj_bs",
    )(x, g, sc, sh, posf, invf, w_bf)


def _rglru_kernel(xp_ref, xm_ref, xn_ref, cw_ref, cb_ref, wg_ref, ba_ref, bi_ref, lam_ref,
                  o_ref, a_scr, u_scr, h_scr):
    d = pl.program_id(1)
    t = pl.program_id(2)
    nt = pl.num_programs(2)
    te = t + d * (nt - 1 - 2 * t)
    rows = RG_TT * BATCH

    pm = (te > 0).astype(F32)
    nm = (te < nt - 1).astype(F32)
    xin = jnp.concatenate([xp_ref[...].astype(F32) * pm,
                           xm_ref[...].astype(F32),
                           xn_ref[...].astype(F32) * nm], axis=0)
    cw = cw_ref[...]
    xc = cb_ref[...]
    for k in range(RNN_CONV_W):
        xc = xc + xin[k:k + RG_TT] * cw[k:k + 1]
    x2 = xc.reshape(rows, RG_TC)

    g = jnp.dot(x2.astype(BF16), wg_ref[0, 0], preferred_element_type=F32)
    r = jax.nn.sigmoid(g[:, :RG_TC] + ba_ref[0])
    gi = jax.nn.sigmoid(g[:, RG_TC:] + bi_ref[0])
    z = -lam_ref[0]
    sp = jnp.maximum(z, 0.0) + jnp.log1p(jnp.exp(-jnp.abs(z)))
    log_a = -RG_C * r * sp
    a = jnp.exp(log_a)
    th = jnp.tanh(log_a)
    one_minus_a2 = -2.0 * th / (1.0 - th)
    u = jnp.sqrt(one_minus_a2) * (gi * x2)
    a_scr[...] = a.reshape(RG_TT, BATCH, RG_TC)
    u_scr[...] = u.reshape(RG_TT, BATCH, RG_TC)

    @pl.when(t == 0)
    def _():
        h_scr[...] = jnp.zeros_like(h_scr)

    def body(s, h):
        idx = s + d * (RG_TT - 1 - 2 * s)
        h = a_scr[idx] * h + u_scr[idx]
        o_ref[0, idx] = h.astype(BF16)
        return h

    h_scr[...] = lax.fori_loop(0, RG_TT, body, h_scr[...], unroll=8)


def _rglru(proj3, cw, cb, wg, ba, bi, lam):
    nt = SEQ // RG_TT

    def te(d, t):
        return t + d * (nt - 1 - 2 * t)

    return pl.pallas_call(
        _rglru_kernel,
        grid=(D_RNN // RG_TC, 2, nt),
        in_specs=[
            pl.BlockSpec((2, BATCH, RG_TC),
                         lambda c, d, t: (jnp.maximum(te(d, t) * (RG_TT // 2) - 1, 0), 0, c)),
            pl.BlockSpec((RG_TT, BATCH, RG_TC), lambda c, d, t: (te(d, t), 0, c)),
            pl.BlockSpec((1, BATCH, RG_TC),
                         lambda c, d, t: (jnp.minimum((te(d, t) + 1) * RG_TT, SEQ - 1), 0, c)),
            pl.BlockSpec((RNN_CONV_W, RG_TC), lambda c, d, t: (0, c)),
            pl.BlockSpec((1, RG_TC), lambda c, d, t: (0, c)),
            pl.BlockSpec((1, 1, RG_TC, 2 * RG_TC), lambda c, d, t: (d, c, 0, 0)),
            pl.BlockSpec((1, 1, RG_TC), lambda c, d, t: (d, 0, c)),
            pl.BlockSpec((1, 1, RG_TC), lambda c, d, t: (d, 0, c)),
            pl.BlockSpec((1, 1, RG_TC), lambda c, d, t: (d, 0, c)),
        ],
        out_specs=pl.BlockSpec((1, RG_TT, BATCH, RG_TC), lambda c, d, t: (d, te(d, t), 0, c)),
        out_shape=jax.ShapeDtypeStruct((2, SEQ, BATCH, D_RNN), BF16),
        scratch_shapes=[pltpu.VMEM((RG_TT, BATCH, RG_TC), F32),
                        pltpu.VMEM((RG_TT, BATCH, RG_TC), F32),
                        pltpu.VMEM((BATCH, RG_TC), F32)],
        compiler_params=_params(("parallel", "arbitrary", "arbitrary")),
        name="rglru",
    )(proj3, proj3, proj3, cw, cb, wg, ba, bi, lam)


def _zero_after(x):
    bits = lax.bitcast_convert_type(x, jnp.uint32)
    half = jnp.uint32(16)
    return lax.bitcast_convert_type(
        lax.shift_right_logical(lax.shift_right_logical(bits, half), half), F32)


def _attn_kernel(lamv_ref, gt_ref, q_ref, k_ref, v_ref, o_ref,
                 vt_scr, sa_scr, sb_scr, pa_scr, pb_scr, ma_scr, mb_scr):
    g = pl.program_id(0)
    nkb = SEQ // AT_KB
    nq = 2 * AT_TQ
    dn = (((1,), (1,)), ((), ()))

    @pl.when(g == 0)
    def _():
        sa_scr[...] = jnp.zeros_like(sa_scr)
        sb_scr[...] = jnp.zeros_like(sb_scr)
        pa_scr[...] = jnp.ones_like(pa_scr)
        pb_scr[...] = jnp.ones_like(pb_scr)
        ma_scr[...] = jnp.zeros_like(ma_scr)
        mb_scr[...] = jnp.zeros_like(mb_scr)

    @pl.when((g == 0) | ((g - 1) % AT_PAIRS_PER_HEAD == 0))
    def _():
        vt_scr[0:V_DIM] = v_ref[0].astype(F32).T.astype(BF16)
        vt_scr[V_DIM:AT_VROWS] = jnp.ones((AT_VROWS - V_DIM, SEQ), BF16)

    lv = lamv_ref[...]
    lam = (jnp.exp(jnp.sum(lv[0:1] * lv[1:2], axis=-1, keepdims=True))
           - jnp.exp(jnp.sum(lv[2:3] * lv[3:4], axis=-1, keepdims=True)) + LAM_INIT)

    def half_step(half, s_new, m_new, s_old, m_old, p_new, p_old):
        q = q_ref[0, half * AT_TQ:(half + 1) * AT_TQ, :]
        lane = lax.broadcasted_iota(jnp.int32, q.shape, 1)
        zero = jnp.zeros_like(q)
        qcat = jnp.concatenate([jnp.where(lane < HEAD_DIM, q, zero),
                                jnp.where(lane >= HEAD_DIM, q, zero)], axis=0)
        m_prev = m_old[...]
        m8 = jnp.full((8, nq), -jnp.inf, F32)
        for kb in range(nkb):
            rows = slice(kb * AT_KB, (kb + 1) * AT_KB)
            s = lax.dot_general(k_ref[0, rows, :], qcat, dn, preferred_element_type=F32)
            s_new[rows, :] = s
            for r in range(AT_KB // 8):
                m8 = jnp.maximum(m8, s[r * 8:(r + 1) * 8, :])
            m_tied = m_prev + _zero_after(s[AT_KB - 8:AT_KB, :])
            e = jnp.exp2(s_old[rows, :].reshape(AT_KB // 8, 8, nq) - m_tied[None])
            p_new[rows, :] = e.reshape(AT_KB, nq).astype(BF16)
        m_new[...] = jnp.max(m8, axis=0, keepdims=True)
        acc = jnp.dot(vt_scr[...], p_old[...], preferred_element_type=F32)
        o1 = acc[0:V_DIM, :AT_TQ] / acc[V_DIM:V_DIM + 1, :AT_TQ]
        o2 = acc[0:V_DIM, AT_TQ:] / acc[V_DIM:V_DIM + 1, AT_TQ:]
        o = o1 - lam * o2
        ms = jnp.mean(o * o, axis=0, keepdims=True)
        y = o * lax.rsqrt(ms + NORM_EPS) * gt_ref[...]
        o_ref[0, half * AT_TQ:(half + 1) * AT_TQ, :] = (y * (1.0 - LAM_INIT)).T.astype(BF16)

    once = jnp.minimum(g + 1, 1)

    def first(i, c):
        half_step(0, sa_scr, ma_scr, sb_scr, mb_scr, pb_scr, pa_scr)
        return c

    def second(i, c):
        half_step(1, sb_scr, mb_scr, sa_scr, ma_scr, pa_scr, pb_scr)
        return c

    lax.fori_loop(0, once, first, 0)
    lax.fori_loop(0, once, second, 0)


def _attn(qkv, lamv, subln_gt):
    n_pairs = BATCH * N_HEADS * AT_PAIRS_PER_HEAD

    def pair_index(p, col0):
        head = p // AT_PAIRS_PER_HEAD
        return head // N_HEADS, p % AT_PAIRS_PER_HEAD, col0 + head % N_HEADS

    def head_index(p, col0):
        head = p // AT_PAIRS_PER_HEAD
        return head // N_HEADS, 0, col0 + head % N_HEADS

    front = lambda g: jnp.minimum(g, n_pairs - 1)
    back = lambda g: jnp.maximum(g - 1, 0)
    s_buf = pltpu.VMEM((SEQ, 2 * AT_TQ), F32)
    p_buf = pltpu.VMEM((SEQ, 2 * AT_TQ), BF16)
    m_buf = pltpu.VMEM((1, 2 * AT_TQ), F32)
    return pl.pallas_call(
        _attn_kernel,
        grid=(n_pairs + 1,),
        in_specs=[
            pl.BlockSpec((4, HEAD_DIM), lambda g: (0, 0)),
            pl.BlockSpec((V_DIM, 1), lambda g: (0, 0)),
            pl.BlockSpec((1, 2 * AT_TQ, LANES), lambda g: pair_index(front(g), 0)),
            pl.BlockSpec((1, SEQ, LANES), lambda g: head_index(front(g), N_HEADS)),
            pl.BlockSpec((1, SEQ, LANES), lambda g: head_index(back(g), 2 * N_HEADS)),
        ],
        out_specs=pl.BlockSpec((1, 2 * AT_TQ, LANES), lambda g: pair_index(back(g), 0)),
        out_shape=jax.ShapeDtypeStruct((BATCH, SEQ, N_HEADS * V_DIM), BF16),
        scratch_shapes=[pltpu.VMEM((AT_VROWS, SEQ), BF16),
                        s_buf, s_buf, p_buf, p_buf, m_buf, m_buf],
        compiler_params=_params(("arbitrary",)),
        name="diffattn",
    )(lamv, subln_gt, qkv, qkv, qkv)


def _merge_kernel(hf_ref, hb_ref, yr_ref, ga_ref, gb_ref, at_ref, x_ref, g1_ref,
                  wr_ref, wa_ref, wo_ref, o_ref):
    rows = MG_TS * BATCH
    hr = hf_ref[0].astype(F32) + hb_ref[0].astype(F32)
    ya = (hr * jax.nn.gelu(yr_ref[...].astype(F32))).reshape(rows, D_RNN).astype(BF16)
    br_a = jnp.dot(ya, wr_ref[...], preferred_element_type=F32)
    br_b = jnp.dot(at_ref[...].reshape(rows, D_MODEL), wa_ref[...], preferred_element_type=F32)
    ga = jax.nn.sigmoid(ga_ref[...].reshape(rows, D_MODEL).astype(F32))
    gb = jax.nn.sigmoid(gb_ref[...].reshape(rows, D_MODEL).astype(F32))
    merged = (ga * br_a + gb * br_b).astype(BF16)
    m = jnp.dot(merged, wo_ref[...], preferred_element_type=F32)
    o_ref[...] = x_ref[...] + g1_ref[...] * m.reshape(MG_TS, BATCH, D_MODEL)


def _merge(hfb, proj3, attn3, x3, g1, wr, wa, wo):
    tok = lambda cidx: pl.BlockSpec((MG_TS, BATCH, D_MODEL), lambda i: (i, 0, cidx))
    wspec = pl.BlockSpec((D_MODEL, D_MODEL), lambda i: (0, 0))
    return pl.pallas_call(
        _merge_kernel,
        grid=(SEQ // MG_TS,),
        in_specs=[pl.BlockSpec((1, MG_TS, BATCH, D_RNN), lambda i: (0, i, 0, 0)),
                  pl.BlockSpec((1, MG_TS, BATCH, D_RNN), lambda i: (1, i, 0, 0)),
                  tok(1), tok(2), tok(3), tok(0), tok(0),
                  pl.BlockSpec((BATCH, D_MODEL), lambda i: (0, 0)),
                  wspec, wspec, wspec],
        out_specs=tok(0),
        out_shape=jax.ShapeDtypeStruct((SEQ, BATCH, D_MODEL), F32),
        compiler_params=_params(("parallel",)),
        name="merge",
    )(hfb, hfb, proj3, proj3, proj3, attn3, x3, g1, wr, wa, wo)


def _ffn_kernel(xp_ref, xm_ref, xn_ref, g_ref, sc_ref, sh_ref, g2_ref, fg_ref,
                wu_ref, cw_ref, cb_ref, wd_ref, o_ref, h_scr, acc_scr):
    i = pl.program_id(0)
    n = pl.num_programs(0)
    rows = FF_TS * BATCH
    pm = (i > 0).astype(F32)
    nm = (i < n - 1).astype(F32)
    g, sc, sh = g_ref[...], sc_ref[...], sh_ref[...]
    xm = xm_ref[...]
    h_scr[0:1] = (_rms_mod(xp_ref[...], g, sc, sh) * pm).astype(BF16)
    h_scr[1:FF_TS + 1] = _rms_mod(xm, g, sc, sh).astype(BF16)
    h_scr[FF_TS + 1:FF_TS + 2] = (_rms_mod(xn_ref[...], g, sc, sh) * nm).astype(BF16)
    hx = h_scr[...].reshape((FF_TS + 2) * BATCH, D_MODEL)
    acc_scr[...] = jnp.zeros_like(acc_scr)

    def chunk(ci, carry):
        up = jnp.dot(hx, wu_ref[ci], preferred_element_type=F32)
        up = up.reshape(FF_TS + 2, BATCH, 2 * FF_CH)
        cw = cw_ref[ci]
        cv = cb_ref[ci]
        for k in range(3):
            cv = cv + up[k:k + FF_TS] * cw[k:k + 1]
        cv = cv.reshape(rows, 2 * FF_CH)
        val = cv[:, :FF_CH]
        gt = cv[:, FF_CH:]
        act = (gt * jax.nn.sigmoid(gt) * val).astype(BF16)
        acc_scr[...] += jnp.dot(act, wd_ref[ci], preferred_element_type=F32)
        return carry

    lax.fori_loop(0, FF_NCH, chunk, 0)
    x2 = xm + g2_ref[...] * acc_scr[...].reshape(FF_TS, BATCH, D_MODEL)
    ms = jnp.mean(x2 * x2, axis=-1, keepdims=True)
    o_ref[...] = x2 * lax.rsqrt(ms + NORM_EPS) * fg_ref[...]


def _ffn(x1, g, sc, sh, g2, fg, wu, cw, cb, wd):
    const = lambda shape: pl.BlockSpec(shape, lambda i: (0,) * len(shape),
                                       pipeline_mode=pl.Buffered(1))
    return pl.pallas_call(
        _ffn_kernel,
        grid=(SEQ // FF_TS,),
        in_specs=[
            pl.BlockSpec((1, BATCH, D_MODEL), lambda i: (jnp.maximum(i * FF_TS - 1, 0), 0, 0)),
            pl.BlockSpec((FF_TS, BATCH, D_MODEL), lambda i: (i, 0, 0)),
            pl.BlockSpec((1, BATCH, D_MODEL),
                         lambda i: (jnp.minimum((i + 1) * FF_TS, SEQ - 1), 0, 0)),
            const((1, D_MODEL)), const((BATCH, D_MODEL)), const((BATCH, D_MODEL)),
            const((BATCH, D_MODEL)), const((1, D_MODEL)),
            const((FF_NCH, D_MODEL, 2 * FF_CH)),
            const((FF_NCH, 3, 2 * FF_CH)),
            const((FF_NCH, 1, 2 * FF_CH)),
            const((FF_NCH, FF_CH, D_MODEL)),
        ],
        out_specs=pl.BlockSpec((FF_TS, BATCH, D_MODEL), lambda i: (i, 0, 0)),
        out_shape=jax.ShapeDtypeStruct((SEQ, BATCH, D_MODEL), F32),
        scratch_shapes=[pltpu.VMEM((FF_TS + 2, BATCH, D_MODEL), BF16),
                        pltpu.VMEM((FF_TS * BATCH, D_MODEL), F32)],
        compiler_params=_params(("parallel",)),
        name="ffn",
    )(x1, x1, x1, g, sc, sh, g2, fg, wu, cw, cb, wd)


def _block_diag_tiles(w):
    eye = jnp.eye(N_RNN_BLOCKS, dtype=w.dtype)
    full = jnp.einsum('dnkj,nm->dnkmj', w, eye).reshape(2, D_RNN, D_RNN)
    nt = D_RNN // RG_TC
    return jnp.stack([full[:, c * RG_TC:(c + 1) * RG_TC, c * RG_TC:(c + 1) * RG_TC]
                      for c in range(nt)], axis=1)


def _pair_chunks(a):
    val = a[..., :D_FF].reshape(a.shape[:-1] + (FF_NCH, FF_CH))
    gt = a[..., D_FF:].reshape(a.shape[:-1] + (FF_NCH, FF_CH))
    both = jnp.concatenate([val, gt], axis=-1)
    return jnp.moveaxis(both, -2, 0)


def kernel(x, c, positions, w_ada, b_ada, norm1_g, w_in, conv_rnn_w, conv_rnn_b, w_rg_a, b_rg_a,
           w_rg_i, b_rg_i, rg_lambda, w_rnn_o, lam_q1, lam_k1, lam_q2, lam_k2, subln_g, w_attn_o,
           w_out, norm2_g, w_up, conv_ffn_w, conv_ffn_b, w_down, final_g):
    l = 0
    x3 = x.transpose(1, 0, 2)
    posf = jnp.broadcast_to(positions.astype(F32)[:, :, None], (BATCH, SEQ, LANES))
    inv_freq = ROPE_THETA ** (-jnp.arange(0, ROPE_DIM, 2, dtype=F32) / ROPE_DIM)
    invf = jnp.tile(inv_freq, LANES // ROPE_HALF).reshape(1, LANES)

    mod = _ada(c, w_ada[l], b_ada[l])
    sh1, sc1, g1, sh2, sc2, g2 = [mod[:, m * D_MODEL:(m + 1) * D_MODEL] for m in range(N_MOD)]

    w_in_bf = w_in[l].astype(BF16)
    g_n1 = norm1_g[l].reshape(1, D_MODEL)
    w_sb = jnp.concatenate([w_in_bf[:, :2 * D_MODEL], w_in_bf[:, 5 * D_MODEL:]], axis=1)
    w_bs = w_in_bf[:, 2 * D_MODEL:5 * D_MODEL]
    proj3 = _inproj_sb(x3, g_n1, sc1, sh1, w_sb).reshape(SEQ, BATCH, 4 * D_MODEL)
    qkv = _inproj_bs(x, g_n1, sc1.reshape(BATCH, 1, D_MODEL), sh1.reshape(BATCH, 1, D_MODEL),
                     posf, invf, w_bs)

    wg = jnp.concatenate([_block_diag_tiles(w_rg_a[l]), _block_diag_tiles(w_rg_i[l])],
                         axis=-1).astype(BF16)
    hfb = _rglru(proj3, conv_rnn_w[l], conv_rnn_b[l].reshape(1, D_RNN), wg,
                 b_rg_a[l].reshape(2, 1, D_RNN), b_rg_i[l].reshape(2, 1, D_RNN),
                 rg_lambda[l].reshape(2, 1, D_RNN))

    lamv = jnp.stack([lam_q1[l], lam_k1[l], lam_q2[l], lam_k2[l]]).astype(F32)
    attn3 = _attn(qkv, lamv, subln_g[l].reshape(V_DIM, 1)).transpose(1, 0, 2)

    x1 = _merge(hfb, proj3, attn3, x3, g1, w_rnn_o[l].astype(BF16), w_attn_o[l].astype(BF16),
                w_out[l].astype(BF16))

    out = _ffn(x1, norm2_g[l].reshape(1, D_MODEL), sc2, sh2, g2, final_g.reshape(1, D_MODEL),
               _pair_chunks(w_up[l]).astype(BF16), _pair_chunks(conv_ffn_w[l]),
               _pair_chunks(conv_ffn_b[l].reshape(1, 2 * D_FF)),
               w_down[l].reshape(FF_NCH, FF_CH, D_MODEL).astype(BF16))
    return out.transpose(1, 0, 2)
```

```python
import functools
import math

import jax
import jax.numpy as jnp
from jax import lax
from jax.experimental import pallas as pl
from jax.experimental.pallas import tpu as pltpu

F32 = jnp.float32
BF16 = jnp.bfloat16

D_MODEL = 1024
BATCH = 16
SEQ = 2048
TOKENS = BATCH * SEQ
D_RNN = D_MODEL
N_RNN_BLOCKS = 16
RNN_BLOCK = D_RNN // N_RNN_BLOCKS
RNN_CONV_W = 4
RNN_CONV_LEFT = 2
RG_C = 8.0
N_HEADS = 8
HEAD_DIM = 64
V_DIM = 2 * HEAD_DIM
ROPE_DIM = HEAD_DIM // 4
ROPE_HALF = ROPE_DIM // 2
ROPE_THETA = 500000.0
D_FF = 2816
N_MOD = 6
NORM_EPS = 1e-6
IN_COLS = 7 * D_MODEL
LAM_INIT = 0.8 - 0.6 * math.exp(-0.3 * 0)
Q_SCALE = HEAD_DIM ** -0.5 * math.log2(math.e)

LANES = 128
VMEM_LIMIT = 52 * 1024 * 1024

ADA_TN = 1024
IN_TS = 64
IN_TN = 1024
QKV_TM = 1024
RG_TT = 128
RG_TC = 256
AT_TQ = 512
AT_KB = 256
AT_VROWS = V_DIM + 16
AT_PAIRS_PER_HEAD = SEQ // (2 * AT_TQ)
MG_TS = 32
FF_TS = 64
FF_CH = 256
FF_NCH = D_FF // FF_CH


def _params(sem, flags=None):
    return pltpu.CompilerParams(dimension_semantics=sem, vmem_limit_bytes=VMEM_LIMIT, flags=flags)


def _ada_kernel(c_ref, w_ref, b_ref, o_ref):
    c = c_ref[...]
    ca = c * jax.nn.sigmoid(c)
    o_ref[...] = jnp.dot(ca, w_ref[...], preferred_element_type=F32,
                         precision=lax.Precision.HIGHEST) + b_ref[...]


def _ada(c, w, b):
    n = w.shape[1]
    return pl.pallas_call(
        _ada_kernel,
        grid=(n // ADA_TN,),
        in_specs=[pl.BlockSpec((BATCH, D_MODEL), lambda j: (0, 0)),
                  pl.BlockSpec((D_MODEL, ADA_TN), lambda j: (0, j)),
                  pl.BlockSpec((1, ADA_TN), lambda j: (0, j))],
        out_specs=pl.BlockSpec((BATCH, ADA_TN), lambda j: (0, j)),
        out_shape=jax.ShapeDtypeStruct((BATCH, n), F32),
        compiler_params=_params(("arbitrary",)),
        name="adaln",
    )(c, w, b.reshape(1, n))


def _rms_mod(x, g, sc, sh):
    ms = jnp.mean(x * x, axis=-1, keepdims=True)
    y = x * lax.rsqrt(ms + NORM_EPS) * g
    return y * (1.0 + sc) + sh


def _inproj_sb_kernel(x_ref, g_ref, sc_ref, sh_ref, w_ref, o_ref, h_scr):
    j = pl.program_id(1)

    @pl.when(j == 0)
    def _():
        h = _rms_mod(x_ref[...], g_ref[...], sc_ref[...], sh_ref[...])
        h_scr[...] = h.reshape(IN_TS * BATCH, D_MODEL).astype(BF16)

    o_ref[...] = jnp.dot(h_scr[...], w_ref[...], preferred_element_type=F32).astype(BF16)


def _inproj_sb(x3, g, sc, sh, w_bf):
    tm = IN_TS * BATCH
    ncol = w_bf.shape[1]
    return pl.pallas_call(
        _inproj_sb_kernel,
        grid=(SEQ // IN_TS, ncol // IN_TN),
        in_specs=[pl.BlockSpec((IN_TS, BATCH, D_MODEL), lambda i, j: (i, 0, 0)),
                  pl.BlockSpec((1, D_MODEL), lambda i, j: (0, 0)),
                  pl.BlockSpec((BATCH, D_MODEL), lambda i, j: (0, 0)),
                  pl.BlockSpec((BATCH, D_MODEL), lambda i, j: (0, 0)),
                  pl.BlockSpec((D_MODEL, IN_TN), lambda i, j: (0, j))],
        out_specs=pl.BlockSpec((tm, IN_TN), lambda i, j: (i, j)),
        out_shape=jax.ShapeDtypeStruct((TOKENS, ncol), BF16),
        scratch_shapes=[pltpu.VMEM((tm, D_MODEL), BF16)],
        compiler_params=_params(("parallel", "arbitrary")),
        name="inproj_sb",
    )(x3, g, sc, sh, w_bf)


def _inproj_bs_kernel(x_ref, g_ref, sc_ref, sh_ref, pos_ref, invf_ref, w_ref, o_ref,
                      h_scr, tab_scr):
    j = pl.program_id(2)

    @pl.when(j == 0)
    def _():
        h = _rms_mod(x_ref[0], g_ref[...], sc_ref[0], sh_ref[0])
        h_scr[...] = h.astype(BF16)
        ang = pos_ref[0] * invf_ref[...]
        c = jnp.cos(ang)
        s = jnp.sin(ang)
        lane = lax.broadcasted_iota(jnp.int32, ang.shape, 1) % HEAD_DIM
        ct = jnp.where(lane < ROPE_DIM, c, 1.0)
        sa = jnp.where(lane < ROPE_HALF, -s, 0.0)
        sb = jnp.where((lane >= ROPE_HALF) & (lane < ROPE_DIM), s, 0.0)
        tab_scr[0, 0] = ct * Q_SCALE
        tab_scr[0, 1] = sa * Q_SCALE
        tab_scr[0, 2] = sb * Q_SCALE
        tab_scr[1, 0] = ct
        tab_scr[1, 1] = sa
        tab_scr[1, 2] = sb
        tab_scr[2, 0] = jnp.ones_like(ct)
        tab_scr[2, 1] = jnp.zeros_like(ct)
        tab_scr[2, 2] = jnp.zeros_like(ct)

    acc = jnp.dot(h_scr[...], w_ref[...], preferred_element_type=F32)
    ct, sa, sb = tab_scr[j, 0], tab_scr[j, 1], tab_scr[j, 2]
    for cidx in range(IN_TN // LANES):
        sl = slice(cidx * LANES, (cidx + 1) * LANES)
        t = acc[:, sl]
        r = (t * ct + pltpu.roll(t, LANES - ROPE_HALF, 1) * sa
             + pltpu.roll(t, ROPE_HALF, 1) * sb)
        o_ref[0, :, sl] = r.astype(BF16)


def _inproj_bs(x, g, sc, sh, posf, invf, w_bf):
    ncol = w_bf.shape[1]
    return pl.pallas_call(
        _inproj_bs_kernel,
        grid=(BATCH, SEQ // QKV_TM, ncol // IN_TN),
        in_specs=[pl.BlockSpec((1, QKV_TM, D_MODEL), lambda b, i, j: (b, i, 0)),
                  pl.BlockSpec((1, D_MODEL), lambda b, i, j: (0, 0)),
                  pl.BlockSpec((1, 1, D_MODEL), lambda b, i, j: (b, 0, 0)),
                  pl.BlockSpec((1, 1, D_MODEL), lambda b, i, j: (b, 0, 0)),
                  pl.BlockSpec((1, QKV_TM, LANES), lambda b, i, j: (b, i, 0)),
                  pl.BlockSpec((1, LANES), lambda b, i, j: (0, 0)),
                  pl.BlockSpec((D_MODEL, IN_TN), lambda b, i, j: (0, j))],
        out_specs=pl.BlockSpec((1, QKV_TM, IN_TN), lambda b, i, j: (b, i, j)),
        out_shape=jax.ShapeDtypeStruct((BATCH, SEQ, ncol), BF16),
        scratch_shapes=[pltpu.VMEM((QKV_TM, D_MODEL), BF16),
                        pltpu.VMEM((3, 3, QKV_TM, LANES), F32)],
        compiler_params=_params(("parallel", "parallel", "arbitrary")),
        name="inproj_bs",
    )(x, g, sc, sh, posf, invf, w_bf)


def _rglru_kernel(xp_ref, xm_ref, xn_ref, cw_ref, cb_ref, wg_ref, ba_ref, bi_ref, lam_ref,
                  o_ref, a_scr, u_scr, h_scr):
    d = pl.program_id(1)
    t = pl.program_id(2)
    nt = pl.num_programs(2)
    te = t + d * (nt - 1 - 2 * t)
    rows = RG_TT * BATCH

    pm = (te > 0).astype(F32)
    nm = (te < nt - 1).astype(F32)
    xin = jnp.concatenate([xp_ref[...].astype(F32) * pm,
                           xm_ref[...].astype(F32),
                           xn_ref[...].astype(F32) * nm], axis=0)
    cw = cw_ref[...]
    xc = cb_ref[...]
    for k in range(RNN_CONV_W):
        xc = xc + xin[k:k + RG_TT] * cw[k:k + 1]
    x2 = xc.reshape(rows, RG_TC)

    g = jnp.dot(x2.astype(BF16), wg_ref[0, 0], preferred_element_type=F32)
    r = jax.nn.sigmoid(g[:, :RG_TC] + ba_ref[0])
    gi = jax.nn.sigmoid(g[:, RG_TC:] + bi_ref[0])
    z = -lam_ref[0]
    sp = jnp.maximum(z, 0.0) + jnp.log1p(jnp.exp(-jnp.abs(z)))
    log_a = -RG_C * r * sp
    a = jnp.exp(log_a)
    th = jnp.tanh(log_a)
    one_minus_a2 = -2.0 * th / (1.0 - th)
    u = jnp.sqrt(one_minus_a2) * (gi * x2)
    a_scr[...] = a.reshape(RG_TT, BATCH, RG_TC)
    u_scr[...] = u.reshape(RG_TT, BATCH, RG_TC)

    @pl.when(t == 0)
    def _():
        h_scr[...] = jnp.zeros_like(h_scr)

    def body(s, h):
        idx = s + d * (RG_TT - 1 - 2 * s)
        h = a_scr[idx] * h + u_scr[idx]
        o_ref[0, idx] = h.astype(BF16)
        return h

    h_scr[...] = lax.fori_loop(0, RG_TT, body, h_scr[...], unroll=8)


def _rglru(proj3, cw, cb, wg, ba, bi, lam):
    nt = SEQ // RG_TT

    def te(d, t):
        return t + d * (nt - 1 - 2 * t)

    return pl.pallas_call(
        _rglru_kernel,
        grid=(D_RNN // RG_TC, 2, nt),
        in_specs=[
            pl.BlockSpec((2, BATCH, RG_TC),
                         lambda c, d, t: (jnp.maximum(te(d, t) * (RG_TT // 2) - 1, 0), 0, c)),
            pl.BlockSpec((RG_TT, BATCH, RG_TC), lambda c, d, t: (te(d, t), 0, c)),
            pl.BlockSpec((1, BATCH, RG_TC),
                         lambda c, d, t: (jnp.minimum((te(d, t) + 1) * RG_TT, SEQ - 1), 0, c)),
            pl.BlockSpec((RNN_CONV_W, RG_TC), lambda c, d, t: (0, c)),
            pl.BlockSpec((1, RG_TC), lambda c, d, t: (0, c)),
            pl.BlockSpec((1, 1, RG_TC, 2 * RG_TC), lambda c, d, t: (d, c, 0, 0)),
            pl.BlockSpec((1, 1, RG_TC), lambda c, d, t: (d, 0, c)),
            pl.BlockSpec((1, 1, RG_TC), lambda c, d, t: (d, 0, c)),
            pl.BlockSpec((1, 1, RG_TC), lambda c, d, t: (d, 0, c)),
        ],
        out_specs=pl.BlockSpec((1, RG_TT, BATCH, RG_TC), lambda c, d, t: (d, te(d, t), 0, c)),
        out_shape=jax.ShapeDtypeStruct((2, SEQ, BATCH, D_RNN), BF16),
        scratch_shapes=[pltpu.VMEM((RG_TT, BATCH, RG_TC), F32),
                        pltpu.VMEM((RG_TT, BATCH, RG_TC), F32),
                        pltpu.VMEM((BATCH, RG_TC), F32)],
        compiler_params=_params(("parallel", "arbitrary", "arbitrary")),
        name="rglru",
    )(proj3, proj3, proj3, cw, cb, wg, ba, bi, lam)


def _zero_after(x):
    bits = lax.bitcast_convert_type(x, jnp.uint32)
    half = jnp.uint32(16)
    return lax.bitcast_convert_type(
        lax.shift_right_logical(lax.shift_right_logical(bits, half), half), F32)


def _attn_kernel(lamv_ref, gt_ref, q_ref, k_ref, v_ref, o_ref,
                 vt_scr, sa_scr, sb_scr, pa_scr, pb_scr, ma_scr, mb_scr):
    g = pl.program_id(0)
    nkb = SEQ // AT_KB
    nq = 2 * AT_TQ
    dn = (((1,), (1,)), ((), ()))

    @pl.when(g == 0)
    def _():
        sa_scr[...] = jnp.zeros_like(sa_scr)
        sb_scr[...] = jnp.zeros_like(sb_scr)
        pa_scr[...] = jnp.ones_like(pa_scr)
        pb_scr[...] = jnp.ones_like(pb_scr)
        ma_scr[...] = jnp.zeros_like(ma_scr)
        mb_scr[...] = jnp.zeros_like(mb_scr)

    @pl.when((g == 0) | ((g - 1) % AT_PAIRS_PER_HEAD == 0))
    def _():
        vt_scr[0:V_DIM] = v_ref[0].astype(F32).T.astype(BF16)
        vt_scr[V_DIM:AT_VROWS] = jnp.ones((AT_VROWS - V_DIM, SEQ), BF16)

    lv = lamv_ref[...]
    lam = (jnp.exp(jnp.sum(lv[0:1] * lv[1:2], axis=-1, keepdims=True))
           - jnp.exp(jnp.sum(lv[2:3] * lv[3:4], axis=-1, keepdims=True)) + LAM_INIT)

    def half_step(half, s_new, m_new, s_old, m_old, p_new, p_old):
        q = q_ref[0, half * AT_TQ:(half + 1) * AT_TQ, :]
        lane = lax.broadcasted_iota(jnp.int32, q.shape, 1)
        zero = jnp.zeros_like(q)
        qcat = jnp.concatenate([jnp.where(lane < HEAD_DIM, q, zero),
                                jnp.where(lane >= HEAD_DIM, q, zero)], axis=0)
        m_prev = m_old[...]
        m8 = jnp.full((8, nq), -jnp.inf, F32)
        for kb in range(nkb):
            rows = slice(kb * AT_KB, (kb + 1) * AT_KB)
            s = lax.dot_general(k_ref[0, rows, :], qcat, dn, preferred_element_type=F32)
            s_new[rows, :] = s
            for r in range(AT_KB // 8):
                m8 = jnp.maximum(m8, s[r * 8:(r + 1) * 8, :])
            m_tied = m_prev + _zero_after(s[AT_KB - 8:AT_KB, :])
            e = jnp.exp2(s_old[rows, :].reshape(AT_KB // 8, 8, nq) - m_tied[None])
            p_new[rows, :] = e.reshape(AT_KB, nq).astype(BF16)
        m_new[...] = jnp.max(m8, axis=0, keepdims=True)
        acc = jnp.dot(vt_scr[...], p_old[...], preferred_element_type=F32)
        o1 = acc[0:V_DIM, :AT_TQ] / acc[V_DIM:V_DIM + 1, :AT_TQ]
        o2 = acc[0:V_DIM, AT_TQ:] / acc[V_DIM:V_DIM + 1, AT_TQ:]
        o = o1 - lam * o2
        ms = jnp.mean(o * o, axis=0, keepdims=True)
        y = o * lax.rsqrt(ms + NORM_EPS) * gt_ref[...]
        o_ref[0, half * AT_TQ:(half + 1) * AT_TQ, :] = (y * (1.0 - LAM_INIT)).T.astype(BF16)

    once = jnp.minimum(g + 1, 1)

    def first(i, c):
        half_step(0, sa_scr, ma_scr, sb_scr, mb_scr, pb_scr, pa_scr)
        return c

    def second(i, c):
        half_step(1, sb_scr, mb_scr, sa_scr, ma_scr, pa_scr, pb_scr)
        return c

    lax.fori_loop(0, once, first, 0)
    lax.fori_loop(0, once, second, 0)


def _attn(qkv, lamv, subln_gt):
    n_pairs = BATCH * N_HEADS * AT_PAIRS_PER_HEAD

    def pair_index(p, col0):
        head = p // AT_PAIRS_PER_HEAD
        return head // N_HEADS, p % AT_PAIRS_PER_HEAD, col0 + head % N_HEADS

    def head_index(p, col0):
        head = p // AT_PAIRS_PER_HEAD
        return head // N_HEADS, 0, col0 + head % N_HEADS

    front = lambda g: jnp.minimum(g, n_pairs - 1)
    back = lambda g: jnp.maximum(g - 1, 0)
    s_buf = pltpu.VMEM((SEQ, 2 * AT_TQ), F32)
    p_buf = pltpu.VMEM((SEQ, 2 * AT_TQ), BF16)
    m_buf = pltpu.VMEM((1, 2 * AT_TQ), F32)
    return pl.pallas_call(
        _attn_kernel,
        grid=(n_pairs + 1,),
        in_specs=[
            pl.BlockSpec((4, HEAD_DIM), lambda g: (0, 0)),
            pl.BlockSpec((V_DIM, 1), lambda g: (0, 0)),
            pl.BlockSpec((1, 2 * AT_TQ, LANES), lambda g: pair_index(front(g), 0)),
            pl.BlockSpec((1, SEQ, LANES), lambda g: head_index(front(g), N_HEADS)),
            pl.BlockSpec((1, SEQ, LANES), lambda g: head_index(back(g), 2 * N_HEADS)),
        ],
        out_specs=pl.BlockSpec((1, 2 * AT_TQ, LANES), lambda g: pair_index(back(g), 0)),
        out_shape=jax.ShapeDtypeStruct((BATCH, SEQ, N_HEADS * V_DIM), BF16),
        scratch_shapes=[pltpu.VMEM((AT_VROWS, SEQ), BF16),
                        s_buf, s_buf, p_buf, p_buf, m_buf, m_buf],
        compiler_params=_params(("arbitrary",)),
        name="diffattn",
    )(lamv, subln_gt, qkv, qkv, qkv)


def _merge_kernel(hf_ref, hb_ref, yr_ref, ga_ref, gb_ref, at_ref, x_ref, g1_ref,
                  wr_ref, wa_ref, wo_ref, o_ref):
    rows = MG_TS * BATCH
    hr = hf_ref[0].astype(F32) + hb_ref[0].astype(F32)
    ya = (hr * jax.nn.gelu(yr_ref[...].astype(F32))).reshape(rows, D_RNN).astype(BF16)
    br_a = jnp.dot(ya, wr_ref[...], preferred_element_type=F32)
    br_b = jnp.dot(at_ref[...].reshape(rows, D_MODEL), wa_ref[...], preferred_element_type=F32)
    ga = jax.nn.sigmoid(ga_ref[...].reshape(rows, D_MODEL).astype(F32))
    gb = jax.nn.sigmoid(gb_ref[...].reshape(rows, D_MODEL).astype(F32))
    merged = (ga * br_a + gb * br_b).astype(BF16)
    m = jnp.dot(merged, wo_ref[...], preferred_element_type=F32)
    o_ref[...] = x_ref[...] + g1_ref[...] * m.reshape(MG_TS, BATCH, D_MODEL)


def _merge(hfb, proj3, attn3, x3, g1, wr, wa, wo):
    tok = lambda cidx: pl.BlockSpec((MG_TS, BATCH, D_MODEL), lambda i: (i, 0, cidx))
    wspec = pl.BlockSpec((D_MODEL, D_MODEL), lambda i: (0, 0))
    return pl.pallas_call(
        _merge_kernel,
        grid=(SEQ // MG_TS,),
        in_specs=[pl.BlockSpec((1, MG_TS, BATCH, D_RNN), lambda i: (0, i, 0, 0)),
                  pl.BlockSpec((1, MG_TS, BATCH, D_RNN), lambda i: (1, i, 0, 0)),
                  tok(1), tok(2), tok(3), tok(0), tok(0),
                  pl.BlockSpec((BATCH, D_MODEL), lambda i: (0, 0)),
                  wspec, wspec, wspec],
        out_specs=tok(0),
        out_shape=jax.ShapeDtypeStruct((SEQ, BATCH, D_MODEL), F32),
        compiler_params=_params(("parallel",)),
        name="merge",
    )(hfb, hfb, proj3, proj3, proj3, attn3, x3, g1, wr, wa, wo)


def _ffn_kernel(xp_ref, xm_ref, xn_ref, g_ref, sc_ref, sh_ref, g2_ref, fg_ref,
                wu_ref, cw_ref, cb_ref, wd_ref, o_ref, h_scr, acta_scr, actb_scr, acc_scr):
    assert FF_NCH % 2 == 1
    i = pl.program_id(0)
    n = pl.num_programs(0)
    rows = FF_TS * BATCH
    pm = (i > 0).astype(F32)
    nm = (i < n - 1).astype(F32)
    g, sc, sh = g_ref[...], sc_ref[...], sh_ref[...]
    h_scr[0:1] = (_rms_mod(xp_ref[...], g, sc, sh) * pm).astype(BF16)
    h_scr[1:FF_TS + 1] = _rms_mod(xm_ref[...], g, sc, sh).astype(BF16)
    h_scr[FF_TS + 1:FF_TS + 2] = (_rms_mod(xn_ref[...], g, sc, sh) * nm).astype(BF16)

    def up_act(ci, slot):
        hx = h_scr[...].reshape((FF_TS + 2) * BATCH, D_MODEL)
        up = jnp.dot(hx, wu_ref[ci], preferred_element_type=F32)
        up = up.reshape(FF_TS + 2, BATCH, 2 * FF_CH)
        cw = cw_ref[ci]
        cv = cb_ref[ci]
        for k in range(3):
            cv = cv + up[k:k + FF_TS] * cw[k:k + 1]
        cv = cv.reshape(rows, 2 * FF_CH)
        val = cv[:, :FF_CH]
        gt = cv[:, FF_CH:]
        slot[...] = (gt * jax.nn.sigmoid(gt) * val).astype(BF16)

    def down(ci, slot):
        acc_scr[...] += jnp.dot(slot[...], wd_ref[ci], preferred_element_type=F32)

    acc_scr[...] = jnp.zeros_like(acc_scr)
    up_act(0, acta_scr)

    def chunk_pair(it, carry):
        ci = 1 + 2 * it
        up_act(ci, actb_scr)
        down(ci - 1, acta_scr)
        up_act(ci + 1, acta_scr)
        down(ci, actb_scr)
        return carry

    lax.fori_loop(0, (FF_NCH - 1) // 2, chunk_pair, 0)
    down(FF_NCH - 1, acta_scr)
    x2 = xm_ref[...] + g2_ref[...] * acc_scr[...].reshape(FF_TS, BATCH, D_MODEL)
    ms = jnp.mean(x2 * x2, axis=-1, keepdims=True)
    o_ref[...] = x2 * lax.rsqrt(ms + NORM_EPS) * fg_ref[...]


def _ffn(x1, g, sc, sh, g2, fg, wu, cw, cb, wd):
    const = lambda shape: pl.BlockSpec(shape, lambda i: (0,) * len(shape),
                                       pipeline_mode=pl.Buffered(1))
    return pl.pallas_call(
        _ffn_kernel,
        grid=(SEQ // FF_TS,),
        in_specs=[
            pl.BlockSpec((1, BATCH, D_MODEL), lambda i: (jnp.maximum(i * FF_TS - 1, 0), 0, 0)),
            pl.BlockSpec((FF_TS, BATCH, D_MODEL), lambda i: (i, 0, 0)),
            pl.BlockSpec((1, BATCH, D_MODEL),
                         lambda i: (jnp.minimum((i + 1) * FF_TS, SEQ - 1), 0, 0)),
            const((1, D_MODEL)), const((BATCH, D_MODEL)), const((BATCH, D_MODEL)),
            const((BATCH, D_MODEL)), const((1, D_MODEL)),
            const((FF_NCH, D_MODEL, 2 * FF_CH)),
            const((FF_NCH, 3, 2 * FF_CH)),
            const((FF_NCH, 1, 2 * FF_CH)),
            const((FF_NCH, FF_CH, D_MODEL)),
        ],
        out_specs=pl.BlockSpec((FF_TS, BATCH, D_MODEL), lambda i: (i, 0, 0)),
        out_shape=jax.ShapeDtypeStruct((SEQ, BATCH, D_MODEL), F32),
        scratch_shapes=[pltpu.VMEM((FF_TS + 2, BATCH, D_MODEL), BF16),
                        pltpu.VMEM((FF_TS * BATCH, FF_CH), BF16),
                        pltpu.VMEM((FF_TS * BATCH, FF_CH), BF16),
                        pltpu.VMEM((FF_TS * BATCH, D_MODEL), F32)],
        compiler_params=_params(("parallel",)),
        name="ffn",
    )(x1, x1, x1, g, sc, sh, g2, fg, wu, cw, cb, wd)


def _block_diag_tiles(w):
    eye = jnp.eye(N_RNN_BLOCKS, dtype=w.dtype)
    full = jnp.einsum('dnkj,nm->dnkmj', w, eye).reshape(2, D_RNN, D_RNN)
    nt = D_RNN // RG_TC
    return jnp.stack([full[:, c * RG_TC:(c + 1) * RG_TC, c * RG_TC:(c + 1) * RG_TC]
                      for c in range(nt)], axis=1)


def _pair_chunks(a):
    val = a[..., :D_FF].reshape(a.shape[:-1] + (FF_NCH, FF_CH))
    gt = a[..., D_FF:].reshape(a.shape[:-1] + (FF_NCH, FF_CH))
    both = jnp.concatenate([val, gt], axis=-1)
    return jnp.moveaxis(both, -2, 0)


def kernel(x, c, positions, w_ada, b_ada, norm1_g, w_in, conv_rnn_w, conv_rnn_b, w_rg_a, b_rg_a,
           w_rg_i, b_rg_i, rg_lambda, w_rnn_o, lam_q1, lam_k1, lam_q2, lam_k2, subln_g, w_attn_o,
           w_out, norm2_g, w_up, conv_ffn_w, conv_ffn_b, w_down, final_g):
    l = 0
    x3 = x.transpose(1, 0, 2)
    posf = jnp.broadcast_to(positions.astype(F32)[:, :, None], (BATCH, SEQ, LANES))
    inv_freq = ROPE_THETA ** (-jnp.arange(0, ROPE_DIM, 2, dtype=F32) / ROPE_DIM)
    invf = jnp.tile(inv_freq, LANES // ROPE_HALF).reshape(1, LANES)

    mod = _ada(c, w_ada[l], b_ada[l])
    sh1, sc1, g1, sh2, sc2, g2 = [mod[:, m * D_MODEL:(m + 1) * D_MODEL] for m in range(N_MOD)]

    w_in_bf = w_in[l].astype(BF16)
    g_n1 = norm1_g[l].reshape(1, D_MODEL)
    w_sb = jnp.concatenate([w_in_bf[:, :2 * D_MODEL], w_in_bf[:, 5 * D_MODEL:]], axis=1)
    w_bs = w_in_bf[:, 2 * D_MODEL:5 * D_MODEL]
    proj3 = _inproj_sb(x3, g_n1, sc1, sh1, w_sb).reshape(SEQ, BATCH, 4 * D_MODEL)
    qkv = _inproj_bs(x, g_n1, sc1.reshape(BATCH, 1, D_MODEL), sh1.reshape(BATCH, 1, D_MODEL),
                     posf, invf, w_bs)

    wg = jnp.concatenate([_block_diag_tiles(w_rg_a[l]), _block_diag_tiles(w_rg_i[l])],
                         axis=-1).astype(BF16)
    hfb = _rglru(proj3, conv_rnn_w[l], conv_rnn_b[l].reshape(1, D_RNN), wg,
                 b_rg_a[l].reshape(2, 1, D_RNN), b_rg_i[l].reshape(2, 1, D_RNN),
                 rg_lambda[l].reshape(2, 1, D_RNN))

    lamv = jnp.stack([lam_q1[l], lam_k1[l], lam_q2[l], lam_k2[l]]).astype(F32)
    attn3 = _attn(qkv, lamv, subln_g[l].reshape(V_DIM, 1)).transpose(1, 0, 2)

    x1 = _merge(hfb, proj3, attn3, x3, g1, w_rnn_o[l].astype(BF16), w_attn_o[l].astype(BF16),
                w_out[l].astype(BF16))

    out = _ffn(x1, norm2_g[l].reshape(1, D_MODEL), sc2, sh2, g2, final_g.reshape(1, D_MODEL),
               _pair_chunks(w_up[l]).astype(BF16), _pair_chunks(conv_ffn_w[l]),
               _pair_chunks(conv_ffn_b[l].reshape(1, 2 * D_FF)),
               w_down[l].reshape(FF_NCH, FF_CH, D_MODEL).astype(BF16))
    return out.transpose(1, 0, 2)
```

```python
import functools
import math

import jax
import jax.numpy as jnp
from jax import lax
from jax.experimental import pallas as pl
from jax.experimental.pallas import tpu as pltpu

F32 = jnp.float32
BF16 = jnp.bfloat16

D_MODEL = 1024
BATCH = 16
SEQ = 2048
TOKENS = BATCH * SEQ
D_RNN = D_MODEL
N_RNN_BLOCKS = 16
RNN_BLOCK = D_RNN // N_RNN_BLOCKS
RNN_CONV_W = 4
RNN_CONV_LEFT = 2
RG_C = 8.0
N_HEADS = 8
HEAD_DIM = 64
V_DIM = 2 * HEAD_DIM
ROPE_DIM = HEAD_DIM // 4
ROPE_HALF = ROPE_DIM // 2
ROPE_THETA = 500000.0
D_FF = 2816
N_MOD = 6
NORM_EPS = 1e-6
IN_COLS = 7 * D_MODEL
LAM_INIT = 0.8 - 0.6 * math.exp(-0.3 * 0)
Q_SCALE = HEAD_DIM ** -0.5 * math.log2(math.e)

LANES = 128
VMEM_LIMIT = 52 * 1024 * 1024

ADA_TN = 1024
IN_TS = 64
IN_TN = 1024
QKV_TM = 1024
RG_TT = 128
RG_TC = 256
AT_TQ = 512
AT_KB = 256
AT_VROWS = V_DIM + 16
AT_PAIRS_PER_HEAD = SEQ // (2 * AT_TQ)
MG_TS = 32
FF_TS = 64
FF_CH = 256
FF_NCH = D_FF // FF_CH


def _params(sem, flags=None):
    return pltpu.CompilerParams(dimension_semantics=sem, vmem_limit_bytes=VMEM_LIMIT, flags=flags)


def _ada_kernel(c_ref, w_ref, b_ref, o_ref):
    c = c_ref[...]
    ca = c * jax.nn.sigmoid(c)
    o_ref[...] = jnp.dot(ca, w_ref[...], preferred_element_type=F32,
                         precision=lax.Precision.HIGHEST) + b_ref[...]


def _ada(c, w, b):
    n = w.shape[1]
    return pl.pallas_call(
        _ada_kernel,
        grid=(n // ADA_TN,),
        in_specs=[pl.BlockSpec((BATCH, D_MODEL), lambda j: (0, 0)),
                  pl.BlockSpec((D_MODEL, ADA_TN), lambda j: (0, j)),
                  pl.BlockSpec((1, ADA_TN), lambda j: (0, j))],
        out_specs=pl.BlockSpec((BATCH, ADA_TN), lambda j: (0, j)),
        out_shape=jax.ShapeDtypeStruct((BATCH, n), F32),
        compiler_params=_params(("arbitrary",)),
        name="adaln",
    )(c, w, b.reshape(1, n))


def _rms_mod(x, g, sc, sh):
    ms = jnp.mean(x * x, axis=-1, keepdims=True)
    y = x * lax.rsqrt(ms + NORM_EPS) * g
    return y * (1.0 + sc) + sh


def _inproj_sb_kernel(x_ref, g_ref, sc_ref, sh_ref, w_ref, o_ref, h_scr):
    h = _rms_mod(x_ref[...], g_ref[...], sc_ref[...], sh_ref[...])
    h_scr[...] = h.reshape(IN_TS * BATCH, D_MODEL).astype(BF16)
    for j in range(w_ref.shape[1] // IN_TN):
        cols = slice(j * IN_TN, (j + 1) * IN_TN)
        o_ref[:, cols] = jnp.dot(h_scr[...], w_ref[:, cols],
                                 preferred_element_type=F32).astype(BF16)


def _resident(shape):
    return pl.BlockSpec(shape, lambda *_: (0,) * len(shape), pipeline_mode=pl.Buffered(1))


def _inproj_sb(x3, g, sc, sh, w_bf):
    tm = IN_TS * BATCH
    ncol = w_bf.shape[1]
    return pl.pallas_call(
        _inproj_sb_kernel,
        grid=(SEQ // IN_TS,),
        in_specs=[pl.BlockSpec((IN_TS, BATCH, D_MODEL), lambda i: (i, 0, 0)),
                  _resident((1, D_MODEL)),
                  _resident((BATCH, D_MODEL)),
                  _resident((BATCH, D_MODEL)),
                  _resident((D_MODEL, ncol))],
        out_specs=pl.BlockSpec((tm, ncol), lambda i: (i, 0)),
        out_shape=jax.ShapeDtypeStruct((TOKENS, ncol), BF16),
        scratch_shapes=[pltpu.VMEM((tm, D_MODEL), BF16)],
        compiler_params=_params(("parallel",)),
        name="inproj_sb",
    )(x3, g, sc, sh, w_bf)


def _inproj_bs_kernel(x_ref, g_ref, sc_ref, sh_ref, pos_ref, invf_ref, w_ref, o_ref,
                      h_scr, tab_scr):
    h = _rms_mod(x_ref[0], g_ref[...], sc_ref[0], sh_ref[0])
    h_scr[...] = h.astype(BF16)
    ang = pos_ref[0] * invf_ref[...]
    c = jnp.cos(ang)
    s = jnp.sin(ang)
    lane = lax.broadcasted_iota(jnp.int32, ang.shape, 1) % HEAD_DIM
    tab_scr[0] = jnp.where(lane < ROPE_DIM, c, 1.0)
    tab_scr[1] = jnp.where(lane < ROPE_HALF, -s, 0.0)
    tab_scr[2] = jnp.where((lane >= ROPE_HALF) & (lane < ROPE_DIM), s, 0.0)

    def rope_tile(j, scale):
        acc = jnp.dot(h_scr[...], w_ref[:, j * D_MODEL:(j + 1) * D_MODEL],
                      preferred_element_type=F32)
        ct, sa, sb = tab_scr[0] * scale, tab_scr[1] * scale, tab_scr[2] * scale
        for cidx in range(D_MODEL // LANES):
            lo = j * D_MODEL + cidx * LANES
            t = acc[:, cidx * LANES:(cidx + 1) * LANES]
            r = (t * ct + pltpu.roll(t, LANES - ROPE_HALF, 1) * sa
                 + pltpu.roll(t, ROPE_HALF, 1) * sb)
            o_ref[0, :, lo:lo + LANES] = r.astype(BF16)

    rope_tile(0, Q_SCALE)
    rope_tile(1, 1.0)
    o_ref[0, :, 2 * D_MODEL:] = jnp.dot(h_scr[...], w_ref[:, 2 * D_MODEL:],
                                        preferred_element_type=F32).astype(BF16)


def _inproj_bs(x, g, sc, sh, posf, invf, w_bf):
    ncol = w_bf.shape[1]
    return pl.pallas_call(
        _inproj_bs_kernel,
        grid=(BATCH, SEQ // QKV_TM),
        in_specs=[pl.BlockSpec((1, QKV_TM, D_MODEL), lambda b, i: (b, i, 0)),
                  _resident((1, D_MODEL)),
                  pl.BlockSpec((1, 1, D_MODEL), lambda b, i: (b, 0, 0)),
                  pl.BlockSpec((1, 1, D_MODEL), lambda b, i: (b, 0, 0)),
                  pl.BlockSpec((1, QKV_TM, LANES), lambda b, i: (b, i, 0)),
                  _resident((1, LANES)),
                  _resident((D_MODEL, ncol))],
        out_specs=pl.BlockSpec((1, QKV_TM, ncol), lambda b, i: (b, i, 0)),
        out_shape=jax.ShapeDtypeStruct((BATCH, SEQ, ncol), BF16),
        scratch_shapes=[pltpu.VMEM((QKV_TM, D_MODEL), BF16),
                        pltpu.VMEM((3, QKV_TM, LANES), F32)],
        compiler_params=_params(("parallel", "parallel")),
        name="inproj_bs",
    )(x, g, sc, sh, posf, invf, w_bf)


def _rglru_kernel(xp_ref, xm_ref, xn_ref, cw_ref, cb_ref, wg_ref, ba_ref, bi_ref, lam_ref,
                  o_ref, a_scr, u_scr, h_scr):
    d = pl.program_id(1)
    t = pl.program_id(2)
    nt = pl.num_programs(2)
    te = t + d * (nt - 1 - 2 * t)
    rows = RG_TT * BATCH

    pm = (te > 0).astype(F32)
    nm = (te < nt - 1).astype(F32)
    xin = jnp.concatenate([xp_ref[...].astype(F32) * pm,
                           xm_ref[...].astype(F32),
                           xn_ref[...].astype(F32) * nm], axis=0)
    cw = cw_ref[...]
    xc = cb_ref[...]
    for k in range(RNN_CONV_W):
        xc = xc + xin[k:k + RG_TT] * cw[k:k + 1]
    x2 = xc.reshape(rows, RG_TC)

    g = jnp.dot(x2.astype(BF16), wg_ref[0, 0], preferred_element_type=F32)
    r = jax.nn.sigmoid(g[:, :RG_TC] + ba_ref[0])
    gi = jax.nn.sigmoid(g[:, RG_TC:] + bi_ref[0])
    z = -lam_ref[0]
    sp = jnp.maximum(z, 0.0) + jnp.log1p(jnp.exp(-jnp.abs(z)))
    log_a = -RG_C * r * sp
    a = jnp.exp(log_a)
    th = jnp.tanh(log_a)
    one_minus_a2 = -2.0 * th / (1.0 - th)
    u = jnp.sqrt(one_minus_a2) * (gi * x2)
    a_scr[...] = a.reshape(RG_TT, BATCH, RG_TC)
    u_scr[...] = u.reshape(RG_TT, BATCH, RG_TC)

    @pl.when(t == 0)
    def _():
        h_scr[...] = jnp.zeros_like(h_scr)

    def body(s, h):
        idx = s + d * (RG_TT - 1 - 2 * s)
        h = a_scr[idx] * h + u_scr[idx]
        o_ref[0, idx] = h.astype(BF16)
        return h

    h_scr[...] = lax.fori_loop(0, RG_TT, body, h_scr[...], unroll=8)


def _rglru(proj3, cw, cb, wg, ba, bi, lam):
    nt = SEQ // RG_TT

    def te(d, t):
        return t + d * (nt - 1 - 2 * t)

    return pl.pallas_call(
        _rglru_kernel,
        grid=(D_RNN // RG_TC, 2, nt),
        in_specs=[
            pl.BlockSpec((2, BATCH, RG_TC),
                         lambda c, d, t: (jnp.maximum(te(d, t) * (RG_TT // 2) - 1, 0), 0, c)),
            pl.BlockSpec((RG_TT, BATCH, RG_TC), lambda c, d, t: (te(d, t), 0, c)),
            pl.BlockSpec((1, BATCH, RG_TC),
                         lambda c, d, t: (jnp.minimum((te(d, t) + 1) * RG_TT, SEQ - 1), 0, c)),
            pl.BlockSpec((RNN_CONV_W, RG_TC), lambda c, d, t: (0, c)),
            pl.BlockSpec((1, RG_TC), lambda c, d, t: (0, c)),
            pl.BlockSpec((1, 1, RG_TC, 2 * RG_TC), lambda c, d, t: (d, c, 0, 0)),
            pl.BlockSpec((1, 1, RG_TC), lambda c, d, t: (d, 0, c)),
            pl.BlockSpec((1, 1, RG_TC), lambda c, d, t: (d, 0, c)),
            pl.BlockSpec((1, 1, RG_TC), lambda c, d, t: (d, 0, c)),
        ],
        out_specs=pl.BlockSpec((1, RG_TT, BATCH, RG_TC), lambda c, d, t: (d, te(d, t), 0, c)),
        out_shape=jax.ShapeDtypeStruct((2, SEQ, BATCH, D_RNN), BF16),
        scratch_shapes=[pltpu.VMEM((RG_TT, BATCH, RG_TC), F32),
                        pltpu.VMEM((RG_TT, BATCH, RG_TC), F32),
                        pltpu.VMEM((BATCH, RG_TC), F32)],
        compiler_params=_params(("parallel", "arbitrary", "arbitrary")),
        name="rglru",
    )(proj3, proj3, proj3, cw, cb, wg, ba, bi, lam)


def _zero_after(x):
    bits = lax.bitcast_convert_type(x, jnp.uint32)
    half = jnp.uint32(16)
    return lax.bitcast_convert_type(
        lax.shift_right_logical(lax.shift_right_logical(bits, half), half), F32)


def _attn_kernel(lamv_ref, gt_ref, q_ref, k_ref, v_ref, o_ref,
                 vt_scr, sa_scr, sb_scr, pa_scr, pb_scr, ma_scr, mb_scr):
    g = pl.program_id(0)
    nkb = SEQ // AT_KB
    nq = 2 * AT_TQ
    dn = (((1,), (1,)), ((), ()))

    @pl.when(g == 0)
    def _():
        sa_scr[...] = jnp.zeros_like(sa_scr)
        sb_scr[...] = jnp.zeros_like(sb_scr)
        pa_scr[...] = jnp.ones_like(pa_scr)
        pb_scr[...] = jnp.ones_like(pb_scr)
        ma_scr[...] = jnp.zeros_like(ma_scr)
        mb_scr[...] = jnp.zeros_like(mb_scr)

    @pl.when((g == 0) | ((g - 1) % AT_PAIRS_PER_HEAD == 0))
    def _():
        vt_scr[0:V_DIM] = v_ref[0].astype(F32).T.astype(BF16)
        vt_scr[V_DIM:AT_VROWS] = jnp.ones((AT_VROWS - V_DIM, SEQ), BF16)

    lv = lamv_ref[...]
    lam = (jnp.exp(jnp.sum(lv[0:1] * lv[1:2], axis=-1, keepdims=True))
           - jnp.exp(jnp.sum(lv[2:3] * lv[3:4], axis=-1, keepdims=True)) + LAM_INIT)

    def half_step(half, s_new, m_new, s_old, m_old, p_new, p_old):
        q = q_ref[0, half * AT_TQ:(half + 1) * AT_TQ, :]
        lane = lax.broadcasted_iota(jnp.int32, q.shape, 1)
        zero = jnp.zeros_like(q)
        qcat = jnp.concatenate([jnp.where(lane < HEAD_DIM, q, zero),
                                jnp.where(lane >= HEAD_DIM, q, zero)], axis=0)
        m_prev = m_old[...]
        m8 = jnp.full((8, nq), -jnp.inf, F32)
        for kb in range(nkb):
            rows = slice(kb * AT_KB, (kb + 1) * AT_KB)
            s = lax.dot_general(k_ref[0, rows, :], qcat, dn, preferred_element_type=F32)
            s_new[rows, :] = s
            for r in range(AT_KB // 8):
                m8 = jnp.maximum(m8, s[r * 8:(r + 1) * 8, :])
            m_tied = m_prev + _zero_after(s[AT_KB - 8:AT_KB, :])
            e = jnp.exp2(s_old[rows, :].reshape(AT_KB // 8, 8, nq) - m_tied[None])
            p_new[rows, :] = e.reshape(AT_KB, nq).astype(BF16)
        m_new[...] = jnp.max(m8, axis=0, keepdims=True)
        acc = jnp.dot(vt_scr[...], p_old[...], preferred_element_type=F32)
        o1 = acc[0:V_DIM, :AT_TQ] / acc[V_DIM:V_DIM + 1, :AT_TQ]
        o2 = acc[0:V_DIM, AT_TQ:] / acc[V_DIM:V_DIM + 1, AT_TQ:]
        o = o1 - lam * o2
        ms = jnp.mean(o * o, axis=0, keepdims=True)
        y = o * lax.rsqrt(ms + NORM_EPS) * gt_ref[...]
        o_ref[0, half * AT_TQ:(half + 1) * AT_TQ, :] = (y * (1.0 - LAM_INIT)).T.astype(BF16)

    once = jnp.minimum(g + 1, 1)

    def first(i, c):
        half_step(0, sa_scr, ma_scr, sb_scr, mb_scr, pb_scr, pa_scr)
        return c

    def second(i, c):
        half_step(1, sb_scr, mb_scr, sa_scr, ma_scr, pa_scr, pb_scr)
        return c

    lax.fori_loop(0, once, first, 0)
    lax.fori_loop(0, once, second, 0)


def _attn(qkv, lamv, subln_gt):
    n_pairs = BATCH * N_HEADS * AT_PAIRS_PER_HEAD

    def pair_index(p, col0):
        head = p // AT_PAIRS_PER_HEAD
        return head // N_HEADS, p % AT_PAIRS_PER_HEAD, col0 + head % N_HEADS

    def head_index(p, col0):
        head = p // AT_PAIRS_PER_HEAD
        return head // N_HEADS, 0, col0 + head % N_HEADS

    front = lambda g: jnp.minimum(g, n_pairs - 1)
    back = lambda g: jnp.maximum(g - 1, 0)
    s_buf = pltpu.VMEM((SEQ, 2 * AT_TQ), F32)
    p_buf = pltpu.VMEM((SEQ, 2 * AT_TQ), BF16)
    m_buf = pltpu.VMEM((1, 2 * AT_TQ), F32)
    return pl.pallas_call(
        _attn_kernel,
        grid=(n_pairs + 1,),
        in_specs=[
            pl.BlockSpec((4, HEAD_DIM), lambda g: (0, 0)),
            pl.BlockSpec((V_DIM, 1), lambda g: (0, 0)),
            pl.BlockSpec((1, 2 * AT_TQ, LANES), lambda g: pair_index(front(g), 0)),
            pl.BlockSpec((1, SEQ, LANES), lambda g: head_index(front(g), N_HEADS)),
            pl.BlockSpec((1, SEQ, LANES), lambda g: head_index(back(g), 2 * N_HEADS)),
        ],
        out_specs=pl.BlockSpec((1, 2 * AT_TQ, LANES), lambda g: pair_index(back(g), 0)),
        out_shape=jax.ShapeDtypeStruct((BATCH, SEQ, N_HEADS * V_DIM), BF16),
        scratch_shapes=[pltpu.VMEM((AT_VROWS, SEQ), BF16),
                        s_buf, s_buf, p_buf, p_buf, m_buf, m_buf],
        compiler_params=_params(("arbitrary",)),
        name="diffattn",
    )(lamv, subln_gt, qkv, qkv, qkv)


def _merge_kernel(hf_ref, hb_ref, yr_ref, ga_ref, gb_ref, at_ref, x_ref, g1_ref,
                  wr_ref, wa_ref, wo_ref, o_ref):
    rows = MG_TS * BATCH
    hr = hf_ref[0].astype(F32) + hb_ref[0].astype(F32)
    ya = (hr * jax.nn.gelu(yr_ref[...].astype(F32))).reshape(rows, D_RNN).astype(BF16)
    br_a = jnp.dot(ya, wr_ref[...], preferred_element_type=F32)
    br_b = jnp.dot(at_ref[...].reshape(rows, D_MODEL), wa_ref[...], preferred_element_type=F32)
    ga = jax.nn.sigmoid(ga_ref[...].reshape(rows, D_MODEL).astype(F32))
    gb = jax.nn.sigmoid(gb_ref[...].reshape(rows, D_MODEL).astype(F32))
    merged = (ga * br_a + gb * br_b).astype(BF16)
    m = jnp.dot(merged, wo_ref[...], preferred_element_type=F32)
    o_ref[...] = x_ref[...] + g1_ref[...] * m.reshape(MG_TS, BATCH, D_MODEL)


def _merge(hfb, proj3, attn3, x3, g1, wr, wa, wo):
    tok = lambda cidx: pl.BlockSpec((MG_TS, BATCH, D_MODEL), lambda i: (i, 0, cidx))
    wspec = pl.BlockSpec((D_MODEL, D_MODEL), lambda i: (0, 0))
    return pl.pallas_call(
        _merge_kernel,
        grid=(SEQ // MG_TS,),
        in_specs=[pl.BlockSpec((1, MG_TS, BATCH, D_RNN), lambda i: (0, i, 0, 0)),
                  pl.BlockSpec((1, MG_TS, BATCH, D_RNN), lambda i: (1, i, 0, 0)),
                  tok(1), tok(2), tok(3), tok(0), tok(0),
                  pl.BlockSpec((BATCH, D_MODEL), lambda i: (0, 0)),
                  wspec, wspec, wspec],
        out_specs=tok(0),
        out_shape=jax.ShapeDtypeStruct((SEQ, BATCH, D_MODEL), F32),
        compiler_params=_params(("parallel",)),
        name="merge",
    )(hfb, hfb, proj3, proj3, proj3, attn3, x3, g1, wr, wa, wo)


def _ffn_kernel(xp_ref, xm_ref, xn_ref, g_ref, sc_ref, sh_ref, g2_ref, fg_ref,
                wu_ref, cw_ref, cb_ref, wd_ref, o_ref, h_scr, acta_scr, actb_scr, acc_scr):
    assert FF_NCH % 2 == 1
    i = pl.program_id(0)
    n = pl.num_programs(0)
    rows = FF_TS * BATCH
    pm = (i > 0).astype(F32)
    nm = (i < n - 1).astype(F32)
    g, sc, sh = g_ref[...], sc_ref[...], sh_ref[...]
    h_scr[0:1] = (_rms_mod(xp_ref[...], g, sc, sh) * pm).astype(BF16)
    h_scr[1:FF_TS + 1] = _rms_mod(xm_ref[...], g, sc, sh).astype(BF16)
    h_scr[FF_TS + 1:FF_TS + 2] = (_rms_mod(xn_ref[...], g, sc, sh) * nm).astype(BF16)

    def up_act(ci, slot):
        hx = h_scr[...].reshape((FF_TS + 2) * BATCH, D_MODEL)
        up = jnp.dot(hx, wu_ref[ci], preferred_element_type=F32)
        up = up.reshape(FF_TS + 2, BATCH, 2 * FF_CH)
        cw = cw_ref[ci]
        cv = cb_ref[ci]
        for k in range(3):
            cv = cv + up[k:k + FF_TS] * cw[k:k + 1]
        cv = cv.reshape(rows, 2 * FF_CH)
        val = cv[:, :FF_CH]
        gt = cv[:, FF_CH:]
        slot[...] = (gt * jax.nn.sigmoid(gt) * val).astype(BF16)

    def down(ci, slot):
        acc_scr[...] += jnp.dot(slot[...], wd_ref[ci], preferred_element_type=F32)

    acc_scr[...] = jnp.zeros_like(acc_scr)
    up_act(0, acta_scr)

    def chunk_pair(it, carry):
        ci = 1 + 2 * it
        up_act(ci, actb_scr)
        down(ci - 1, acta_scr)
        up_act(ci + 1, acta_scr)
        down(ci, actb_scr)
        return carry

    lax.fori_loop(0, (FF_NCH - 1) // 2, chunk_pair, 0)
    down(FF_NCH - 1, acta_scr)
    x2 = xm_ref[...] + g2_ref[...] * acc_scr[...].reshape(FF_TS, BATCH, D_MODEL)
    ms = jnp.mean(x2 * x2, axis=-1, keepdims=True)
    o_ref[...] = x2 * lax.rsqrt(ms + NORM_EPS) * fg_ref[...]


def _ffn(x1, g, sc, sh, g2, fg, wu, cw, cb, wd):
    const = lambda shape: pl.BlockSpec(shape, lambda i: (0,) * len(shape),
                                       pipeline_mode=pl.Buffered(1))
    return pl.pallas_call(
        _ffn_kernel,
        grid=(SEQ // FF_TS,),
        in_specs=[
            pl.BlockSpec((1, BATCH, D_MODEL), lambda i: (jnp.maximum(i * FF_TS - 1, 0), 0, 0)),
            pl.BlockSpec((FF_TS, BATCH, D_MODEL), lambda i: (i, 0, 0)),
            pl.BlockSpec((1, BATCH, D_MODEL),
                         lambda i: (jnp.minimum((i + 1) * FF_TS, SEQ - 1), 0, 0)),
            const((1, D_MODEL)), const((BATCH, D_MODEL)), const((BATCH, D_MODEL)),
            const((BATCH, D_MODEL)), const((1, D_MODEL)),
            const((FF_NCH, D_MODEL, 2 * FF_CH)),
            const((FF_NCH, 3, 2 * FF_CH)),
            const((FF_NCH, 1, 2 * FF_CH)),
            const((FF_NCH, FF_CH, D_MODEL)),
        ],
        out_specs=pl.BlockSpec((FF_TS, BATCH, D_MODEL), lambda i: (i, 0, 0)),
        out_shape=jax.ShapeDtypeStruct((SEQ, BATCH, D_MODEL), F32),
        scratch_shapes=[pltpu.VMEM((FF_TS + 2, BATCH, D_MODEL), BF16),
                        pltpu.VMEM((FF_TS * BATCH, FF_CH), BF16),
                        pltpu.VMEM((FF_TS * BATCH, FF_CH), BF16),
                        pltpu.VMEM((FF_TS * BATCH, D_MODEL), F32)],
        compiler_params=_params(("parallel",)),
        name="ffn",
    )(x1, x1, x1, g, sc, sh, g2, fg, wu, cw, cb, wd)


def _block_diag_tiles(w):
    eye = jnp.eye(N_RNN_BLOCKS, dtype=w.dtype)
    full = jnp.einsum('dnkj,nm->dnkmj', w, eye).reshape(2, D_RNN, D_RNN)
    nt = D_RNN // RG_TC
    return jnp.stack([full[:, c * RG_TC:(c + 1) * RG_TC, c * RG_TC:(c + 1) * RG_TC]
                      for c in range(nt)], axis=1)


def _pair_chunks(a):
    val = a[..., :D_FF].reshape(a.shape[:-1] + (FF_NCH, FF_CH))
    gt = a[..., D_FF:].reshape(a.shape[:-1] + (FF_NCH, FF_CH))
    both = jnp.concatenate([val, gt], axis=-1)
    return jnp.moveaxis(both, -2, 0)


def kernel(x, c, positions, w_ada, b_ada, norm1_g, w_in, conv_rnn_w, conv_rnn_b, w_rg_a, b_rg_a,
           w_rg_i, b_rg_i, rg_lambda, w_rnn_o, lam_q1, lam_k1, lam_q2, lam_k2, subln_g, w_attn_o,
           w_out, norm2_g, w_up, conv_ffn_w, conv_ffn_b, w_down, final_g):
    l = 0
    x3 = x.transpose(1, 0, 2)
    posf = jnp.broadcast_to(positions.astype(F32)[:, :, None], (BATCH, SEQ, LANES))
    inv_freq = ROPE_THETA ** (-jnp.arange(0, ROPE_DIM, 2, dtype=F32) / ROPE_DIM)
    invf = jnp.tile(inv_freq, LANES // ROPE_HALF).reshape(1, LANES)

    mod = _ada(c, w_ada[l], b_ada[l])
    sh1, sc1, g1, sh2, sc2, g2 = [mod[:, m * D_MODEL:(m + 1) * D_MODEL] for m in range(N_MOD)]

    w_in_bf = w_in[l].astype(BF16)
    g_n1 = norm1_g[l].reshape(1, D_MODEL)
    w_sb = jnp.concatenate([w_in_bf[:, :2 * D_MODEL], w_in_bf[:, 5 * D_MODEL:]], axis=1)
    w_bs = w_in_bf[:, 2 * D_MODEL:5 * D_MODEL]
    proj3 = _inproj_sb(x3, g_n1, sc1, sh1, w_sb).reshape(SEQ, BATCH, 4 * D_MODEL)
    qkv = _inproj_bs(x, g_n1, sc1.reshape(BATCH, 1, D_MODEL), sh1.reshape(BATCH, 1, D_MODEL),
                     posf, invf, w_bs)

    wg = jnp.concatenate([_block_diag_tiles(w_rg_a[l]), _block_diag_tiles(w_rg_i[l])],
                         axis=-1).astype(BF16)
    hfb = _rglru(proj3, conv_rnn_w[l], conv_rnn_b[l].reshape(1, D_RNN), wg,
                 b_rg_a[l].reshape(2, 1, D_RNN), b_rg_i[l].reshape(2, 1, D_RNN),
                 rg_lambda[l].reshape(2, 1, D_RNN))

    lamv = jnp.stack([lam_q1[l], lam_k1[l], lam_q2[l], lam_k2[l]]).astype(F32)
    attn3 = _attn(qkv, lamv, subln_g[l].reshape(V_DIM, 1)).transpose(1, 0, 2)

    x1 = _merge(hfb, proj3, attn3, x3, g1, w_rnn_o[l].astype(BF16), w_attn_o[l].astype(BF16),
                w_out[l].astype(BF16))

    out = _ffn(x1, norm2_g[l].reshape(1, D_MODEL), sc2, sh2, g2, final_g.reshape(1, D_MODEL),
               _pair_chunks(w_up[l]).astype(BF16), _pair_chunks(conv_ffn_w[l]),
               _pair_chunks(conv_ffn_b[l].reshape(1, 2 * D_FF)),
               w_down[l].reshape(FF_NCH, FF_CH, D_MODEL).astype(BF16))
    return out.transpose(1, 0, 2)
```

```python
import functools
import math

import jax
import jax.numpy as jnp
from jax import lax
from jax.experimental import pallas as pl
from jax.experimental.pallas import tpu as pltpu

F32 = jnp.float32
BF16 = jnp.bfloat16

D_MODEL = 1024
BATCH = 16
SEQ = 2048
TOKENS = BATCH * SEQ
D_RNN = D_MODEL
N_RNN_BLOCKS = 16
RNN_BLOCK = D_RNN // N_RNN_BLOCKS
RNN_CONV_W = 4
RNN_CONV_LEFT = 2
RG_C = 8.0
N_HEADS = 8
HEAD_DIM = 64
V_DIM = 2 * HEAD_DIM
ROPE_DIM = HEAD_DIM // 4
ROPE_HALF = ROPE_DIM // 2
ROPE_THETA = 500000.0
D_FF = 2816
N_MOD = 6
NORM_EPS = 1e-6
IN_COLS = 7 * D_MODEL
LAM_INIT = 0.8 - 0.6 * math.exp(-0.3 * 0)
LOG2_E = math.log2(math.e)
Q_SCALE = HEAD_DIM ** -0.5 * LOG2_E

LANES = 128
VMEM_LIMIT = 52 * 1024 * 1024

ADA_TN = 1024
IN_TS = 64
IN_TN = 1024
QKV_TM = 1024
RG_TT = 128
RG_TC = 256
AT_TQ = 512
AT_KB = 256
AT_VROWS = V_DIM + 16
AT_PAIRS_PER_HEAD = SEQ // (2 * AT_TQ)
MG_TS = 32
FF_TS = 64
FF_CH = 256
FF_NCH = D_FF // FF_CH


def _params(sem, flags=None):
    return pltpu.CompilerParams(dimension_semantics=sem, vmem_limit_bytes=VMEM_LIMIT, flags=flags)


def _ada_kernel(c_ref, w_ref, b_ref, o_ref):
    c = c_ref[...]
    ca = c * jax.nn.sigmoid(c)
    o_ref[...] = jnp.dot(ca, w_ref[...], preferred_element_type=F32,
                         precision=lax.Precision.HIGHEST) + b_ref[...]


def _ada(c, w, b):
    n = w.shape[1]
    return pl.pallas_call(
        _ada_kernel,
        grid=(n // ADA_TN,),
        in_specs=[pl.BlockSpec((BATCH, D_MODEL), lambda j: (0, 0)),
                  pl.BlockSpec((D_MODEL, ADA_TN), lambda j: (0, j)),
                  pl.BlockSpec((1, ADA_TN), lambda j: (0, j))],
        out_specs=pl.BlockSpec((BATCH, ADA_TN), lambda j: (0, j)),
        out_shape=jax.ShapeDtypeStruct((BATCH, n), F32),
        compiler_params=_params(("arbitrary",)),
        name="adaln",
    )(c, w, b.reshape(1, n))


def _rms_mod(x, g, sc, sh):
    ms = jnp.mean(x * x, axis=-1, keepdims=True)
    y = x * lax.rsqrt(ms + NORM_EPS) * g
    return y * (1.0 + sc) + sh


def _inproj_sb_kernel(x_ref, g_ref, sc_ref, sh_ref, w_ref, o_ref, h_scr):
    h = _rms_mod(x_ref[...], g_ref[...], sc_ref[...], sh_ref[...])
    h_scr[...] = h.reshape(IN_TS * BATCH, D_MODEL).astype(BF16)
    for j in range(w_ref.shape[1] // IN_TN):
        cols = slice(j * IN_TN, (j + 1) * IN_TN)
        o_ref[:, cols] = jnp.dot(h_scr[...], w_ref[:, cols],
                                 preferred_element_type=F32).astype(BF16)


def _resident(shape):
    return pl.BlockSpec(shape, lambda *_: (0,) * len(shape), pipeline_mode=pl.Buffered(1))


def _inproj_sb(x3, g, sc, sh, w_bf):
    tm = IN_TS * BATCH
    ncol = w_bf.shape[1]
    return pl.pallas_call(
        _inproj_sb_kernel,
        grid=(SEQ // IN_TS,),
        in_specs=[pl.BlockSpec((IN_TS, BATCH, D_MODEL), lambda i: (i, 0, 0)),
                  _resident((1, D_MODEL)),
                  _resident((BATCH, D_MODEL)),
                  _resident((BATCH, D_MODEL)),
                  _resident((D_MODEL, ncol))],
        out_specs=pl.BlockSpec((tm, ncol), lambda i: (i, 0)),
        out_shape=jax.ShapeDtypeStruct((TOKENS, ncol), BF16),
        scratch_shapes=[pltpu.VMEM((tm, D_MODEL), BF16)],
        compiler_params=_params(("parallel",)),
        name="inproj_sb",
    )(x3, g, sc, sh, w_bf)


def _inproj_bs_kernel(x_ref, g_ref, sc_ref, sh_ref, pos_ref, invf_ref, w_ref, o_ref,
                      h_scr, tab_scr):
    h = _rms_mod(x_ref[0], g_ref[...], sc_ref[0], sh_ref[0])
    h_scr[...] = h.astype(BF16)
    ang = pos_ref[0] * invf_ref[...]
    c = jnp.cos(ang)
    s = jnp.sin(ang)
    lane = lax.broadcasted_iota(jnp.int32, ang.shape, 1) % HEAD_DIM
    tab_scr[0] = jnp.where(lane < ROPE_DIM, c, 1.0)
    tab_scr[1] = jnp.where(lane < ROPE_HALF, -s, 0.0)
    tab_scr[2] = jnp.where((lane >= ROPE_HALF) & (lane < ROPE_DIM), s, 0.0)

    def rope_tile(j, scale):
        acc = jnp.dot(h_scr[...], w_ref[:, j * D_MODEL:(j + 1) * D_MODEL],
                      preferred_element_type=F32)
        ct, sa, sb = tab_scr[0] * scale, tab_scr[1] * scale, tab_scr[2] * scale
        for cidx in range(D_MODEL // LANES):
            lo = j * D_MODEL + cidx * LANES
            t = acc[:, cidx * LANES:(cidx + 1) * LANES]
            r = (t * ct + pltpu.roll(t, LANES - ROPE_HALF, 1) * sa
                 + pltpu.roll(t, ROPE_HALF, 1) * sb)
            o_ref[0, :, lo:lo + LANES] = r.astype(BF16)

    rope_tile(0, Q_SCALE)
    rope_tile(1, 1.0)
    o_ref[0, :, 2 * D_MODEL:] = jnp.dot(h_scr[...], w_ref[:, 2 * D_MODEL:],
                                        preferred_element_type=F32).astype(BF16)


def _inproj_bs(x, g, sc, sh, posf, invf, w_bf):
    ncol = w_bf.shape[1]
    return pl.pallas_call(
        _inproj_bs_kernel,
        grid=(BATCH, SEQ // QKV_TM),
        in_specs=[pl.BlockSpec((1, QKV_TM, D_MODEL), lambda b, i: (b, i, 0)),
                  _resident((1, D_MODEL)),
                  pl.BlockSpec((1, 1, D_MODEL), lambda b, i: (b, 0, 0)),
                  pl.BlockSpec((1, 1, D_MODEL), lambda b, i: (b, 0, 0)),
                  pl.BlockSpec((1, QKV_TM, LANES), lambda b, i: (b, i, 0)),
                  _resident((1, LANES)),
                  _resident((D_MODEL, ncol))],
        out_specs=pl.BlockSpec((1, QKV_TM, ncol), lambda b, i: (b, i, 0)),
        out_shape=jax.ShapeDtypeStruct((BATCH, SEQ, ncol), BF16),
        scratch_shapes=[pltpu.VMEM((QKV_TM, D_MODEL), BF16),
                        pltpu.VMEM((3, QKV_TM, LANES), F32)],
        compiler_params=_params(("parallel", "parallel")),
        name="inproj_bs",
    )(x, g, sc, sh, posf, invf, w_bf)


def _rglru_kernel(xp_ref, xm_ref, xn_ref, cw_ref, cb_ref, wg_ref, ba_ref, bi_ref, lam_ref,
                  o_ref, a_scr, u_scr, h_scr):
    d = pl.program_id(1)
    t = pl.program_id(2)
    nt = pl.num_programs(2)
    te = t + d * (nt - 1 - 2 * t)
    rows = RG_TT * BATCH

    pm = (te > 0).astype(F32)
    nm = (te < nt - 1).astype(F32)
    xin = jnp.concatenate([xp_ref[...].astype(F32) * pm,
                           xm_ref[...].astype(F32),
                           xn_ref[...].astype(F32) * nm], axis=0)
    cw = cw_ref[...]
    xc = cb_ref[...]
    for k in range(RNN_CONV_W):
        xc = xc + xin[k:k + RG_TT] * cw[k:k + 1]
    x2 = xc.reshape(rows, RG_TC)

    g = jnp.dot(x2.astype(BF16), wg_ref[0, 0], preferred_element_type=F32)
    two_r = 1.0 + jnp.tanh(g[:, :RG_TC] + ba_ref[0])
    two_i = 1.0 + jnp.tanh(g[:, RG_TC:] + bi_ref[0])
    z = -lam_ref[0]
    sp = jnp.maximum(z, 0.0) + jnp.log1p(jnp.exp(-jnp.abs(z)))
    c_ln = (-0.5 * RG_C) * sp
    a = jnp.exp2(two_r * (c_ln * LOG2_E))
    th = jnp.tanh(two_r * c_ln)
    y = -2.0 * th / (1.0 - th)
    u = jnp.where(y > 0.0, y * lax.rsqrt(y), 0.0) * ((0.5 * x2) * two_i)
    a_scr[...] = a.reshape(RG_TT, BATCH, RG_TC)
    u_scr[...] = u.reshape(RG_TT, BATCH, RG_TC)

    @pl.when(t == 0)
    def _():
        h_scr[...] = jnp.zeros_like(h_scr)

    def body(s, h):
        idx = s + d * (RG_TT - 1 - 2 * s)
        h = a_scr[idx] * h + u_scr[idx]
        o_ref[0, idx] = h.astype(BF16)
        return h

    h_scr[...] = lax.fori_loop(0, RG_TT, body, h_scr[...], unroll=8)


def _rglru(proj3, cw, cb, wg, ba, bi, lam):
    nt = SEQ // RG_TT

    def te(d, t):
        return t + d * (nt - 1 - 2 * t)

    return pl.pallas_call(
        _rglru_kernel,
        grid=(D_RNN // RG_TC, 2, nt),
        in_specs=[
            pl.BlockSpec((2, BATCH, RG_TC),
                         lambda c, d, t: (jnp.maximum(te(d, t) * (RG_TT // 2) - 1, 0), 0, c)),
            pl.BlockSpec((RG_TT, BATCH, RG_TC), lambda c, d, t: (te(d, t), 0, c)),
            pl.BlockSpec((1, BATCH, RG_TC),
                         lambda c, d, t: (jnp.minimum((te(d, t) + 1) * RG_TT, SEQ - 1), 0, c)),
            pl.BlockSpec((RNN_CONV_W, RG_TC), lambda c, d, t: (0, c)),
            pl.BlockSpec((1, RG_TC), lambda c, d, t: (0, c)),
            pl.BlockSpec((1, 1, RG_TC, 2 * RG_TC), lambda c, d, t: (d, c, 0, 0)),
            pl.BlockSpec((1, 1, RG_TC), lambda c, d, t: (d, 0, c)),
            pl.BlockSpec((1, 1, RG_TC), lambda c, d, t: (d, 0, c)),
            pl.BlockSpec((1, 1, RG_TC), lambda c, d, t: (d, 0, c)),
        ],
        out_specs=pl.BlockSpec((1, RG_TT, BATCH, RG_TC), lambda c, d, t: (d, te(d, t), 0, c)),
        out_shape=jax.ShapeDtypeStruct((2, SEQ, BATCH, D_RNN), BF16),
        scratch_shapes=[pltpu.VMEM((RG_TT, BATCH, RG_TC), F32),
                        pltpu.VMEM((RG_TT, BATCH, RG_TC), F32),
                        pltpu.VMEM((BATCH, RG_TC), F32)],
        compiler_params=_params(("parallel", "arbitrary", "arbitrary")),
        name="rglru",
    )(proj3, proj3, proj3, cw, cb, wg, ba, bi, lam)


def _zero_after(x):
    bits = lax.bitcast_convert_type(x, jnp.uint32)
    half = jnp.uint32(16)
    return lax.bitcast_convert_type(
        lax.shift_right_logical(lax.shift_right_logical(bits, half), half), F32)


def _attn_kernel(lamv_ref, gt_ref, q_ref, k_ref, v_ref, o_ref,
                 vt_scr, sa_scr, sb_scr, pa_scr, pb_scr, ma_scr, mb_scr):
    g = pl.program_id(0)
    nkb = SEQ // AT_KB
    nq = 2 * AT_TQ
    dn = (((1,), (1,)), ((), ()))

    @pl.when(g == 0)
    def _():
        sa_scr[...] = jnp.zeros_like(sa_scr)
        sb_scr[...] = jnp.zeros_like(sb_scr)
        pa_scr[...] = jnp.ones_like(pa_scr)
        pb_scr[...] = jnp.ones_like(pb_scr)
        ma_scr[...] = jnp.zeros_like(ma_scr)
        mb_scr[...] = jnp.zeros_like(mb_scr)

    @pl.when((g == 0) | ((g - 1) % AT_PAIRS_PER_HEAD == 0))
    def _():
        vt_scr[0:V_DIM] = v_ref[0].astype(F32).T.astype(BF16)
        vt_scr[V_DIM:AT_VROWS] = jnp.ones((AT_VROWS - V_DIM, SEQ), BF16)

    lv = lamv_ref[...]
    lam = (jnp.exp(jnp.sum(lv[0:1] * lv[1:2], axis=-1, keepdims=True))
           - jnp.exp(jnp.sum(lv[2:3] * lv[3:4], axis=-1, keepdims=True)) + LAM_INIT)

    def half_step(half, s_new, m_new, s_old, m_old, p_new, p_old):
        q = q_ref[0, half * AT_TQ:(half + 1) * AT_TQ, :]
        lane = lax.broadcasted_iota(jnp.int32, q.shape, 1)
        zero = jnp.zeros_like(q)
        qcat = jnp.concatenate([jnp.where(lane < HEAD_DIM, q, zero),
                                jnp.where(lane >= HEAD_DIM, q, zero)], axis=0)
        m_prev = m_old[...]
        m8 = jnp.full((8, nq), -jnp.inf, F32)
        for kb in range(nkb):
            rows = slice(kb * AT_KB, (kb + 1) * AT_KB)
            s = lax.dot_general(k_ref[0, rows, :], qcat, dn, preferred_element_type=F32)
            s_new[rows, :] = s
            for r in range(AT_KB // 8):
                m8 = jnp.maximum(m8, s[r * 8:(r + 1) * 8, :])
            m_tied = m_prev + _zero_after(s[AT_KB - 8:AT_KB, :])
            e = jnp.exp2(s_old[rows, :].reshape(AT_KB // 8, 8, nq) - m_tied[None])
            p_new[rows, :] = e.reshape(AT_KB, nq).astype(BF16)
        m_new[...] = jnp.max(m8, axis=0, keepdims=True)
        acc = jnp.dot(vt_scr[...], p_old[...], preferred_element_type=F32)
        o1 = acc[0:V_DIM, :AT_TQ] / acc[V_DIM:V_DIM + 1, :AT_TQ]
        o2 = acc[0:V_DIM, AT_TQ:] / acc[V_DIM:V_DIM + 1, AT_TQ:]
        o = o1 - lam * o2
        ms = jnp.mean(o * o, axis=0, keepdims=True)
        y = o * lax.rsqrt(ms + NORM_EPS) * gt_ref[...]
        o_ref[0, half * AT_TQ:(half + 1) * AT_TQ, :] = (y * (1.0 - LAM_INIT)).T.astype(BF16)

    half_step(0, sa_scr, ma_scr, sb_scr, mb_scr, pb_scr, pa_scr)
    half_step(1, sb_scr, mb_scr, sa_scr, ma_scr, pa_scr, pb_scr)


def _attn(qkv, lamv, subln_gt):
    n_pairs = BATCH * N_HEADS * AT_PAIRS_PER_HEAD

    def pair_index(p, col0):
        head = p // AT_PAIRS_PER_HEAD
        return head // N_HEADS, p % AT_PAIRS_PER_HEAD, col0 + head % N_HEADS

    def head_index(p, col0):
        head = p // AT_PAIRS_PER_HEAD
        return head // N_HEADS, 0, col0 + head % N_HEADS

    front = lambda g: jnp.minimum(g, n_pairs - 1)
    back = lambda g: jnp.maximum(g - 1, 0)
    s_buf = pltpu.VMEM((SEQ, 2 * AT_TQ), F32)
    p_buf = pltpu.VMEM((SEQ, 2 * AT_TQ), BF16)
    m_buf = pltpu.VMEM((1, 2 * AT_TQ), F32)
    return pl.pallas_call(
        _attn_kernel,
        grid=(n_pairs + 1,),
        in_specs=[
            pl.BlockSpec((4, HEAD_DIM), lambda g: (0, 0)),
            pl.BlockSpec((V_DIM, 1), lambda g: (0, 0)),
            pl.BlockSpec((1, 2 * AT_TQ, LANES), lambda g: pair_index(front(g), 0)),
            pl.BlockSpec((1, SEQ, LANES), lambda g: head_index(front(g), N_HEADS)),
            pl.BlockSpec((1, SEQ, LANES), lambda g: head_index(back(g), 2 * N_HEADS)),
        ],
        out_specs=pl.BlockSpec((1, 2 * AT_TQ, LANES), lambda g: pair_index(back(g), 0)),
        out_shape=jax.ShapeDtypeStruct((BATCH, SEQ, N_HEADS * V_DIM), BF16),
        scratch_shapes=[pltpu.VMEM((AT_VROWS, SEQ), BF16),
                        s_buf, s_buf, p_buf, p_buf, m_buf, m_buf],
        compiler_params=_params(("arbitrary",)),
        name="diffattn",
    )(lamv, subln_gt, qkv, qkv, qkv)


def _merge_kernel(hf_ref, hb_ref, yr_ref, ga_ref, gb_ref, at_ref, x_ref, g1_ref,
                  wr_ref, wa_ref, wo_ref, o_ref):
    rows = MG_TS * BATCH
    hr = hf_ref[0].astype(F32) + hb_ref[0].astype(F32)
    ya = (hr * jax.nn.gelu(yr_ref[...].astype(F32))).reshape(rows, D_RNN).astype(BF16)
    br_a = jnp.dot(ya, wr_ref[...], preferred_element_type=F32)
    br_b = jnp.dot(at_ref[...].reshape(rows, D_MODEL), wa_ref[...], preferred_element_type=F32)
    ga = jax.nn.sigmoid(ga_ref[...].reshape(rows, D_MODEL).astype(F32))
    gb = jax.nn.sigmoid(gb_ref[...].reshape(rows, D_MODEL).astype(F32))
    merged = (ga * br_a + gb * br_b).astype(BF16)
    m = jnp.dot(merged, wo_ref[...], preferred_element_type=F32)
    o_ref[...] = x_ref[...] + g1_ref[...] * m.reshape(MG_TS, BATCH, D_MODEL)


def _merge(hfb, proj3, attn3, x3, g1, wr, wa, wo):
    tok = lambda cidx: pl.BlockSpec((MG_TS, BATCH, D_MODEL), lambda i: (i, 0, cidx))
    wspec = pl.BlockSpec((D_MODEL, D_MODEL), lambda i: (0, 0))
    return pl.pallas_call(
        _merge_kernel,
        grid=(SEQ // MG_TS,),
        in_specs=[pl.BlockSpec((1, MG_TS, BATCH, D_RNN), lambda i: (0, i, 0, 0)),
                  pl.BlockSpec((1, MG_TS, BATCH, D_RNN), lambda i: (1, i, 0, 0)),
                  tok(1), tok(2), tok(3), tok(0), tok(0),
                  pl.BlockSpec((BATCH, D_MODEL), lambda i: (0, 0)),
                  wspec, wspec, wspec],
        out_specs=tok(0),
        out_shape=jax.ShapeDtypeStruct((SEQ, BATCH, D_MODEL), F32),
        compiler_params=_params(("parallel",)),
        name="merge",
    )(hfb, hfb, proj3, proj3, proj3, attn3, x3, g1, wr, wa, wo)


def _ffn_kernel(xp_ref, xm_ref, xn_ref, g_ref, sc_ref, sh_ref, g2_ref, fg_ref,
                wu_ref, cw_ref, cb_ref, wd_ref, o_ref, h_scr, acta_scr, actb_scr, acc_scr):
    assert FF_NCH % 2 == 1
    i = pl.program_id(0)
    n = pl.num_programs(0)
    rows = FF_TS * BATCH
    pm = (i > 0).astype(F32)
    nm = (i < n - 1).astype(F32)
    g, sc, sh = g_ref[...], sc_ref[...], sh_ref[...]
    h_scr[0:1] = (_rms_mod(xp_ref[...], g, sc, sh) * pm).astype(BF16)
    h_scr[1:FF_TS + 1] = _rms_mod(xm_ref[...], g, sc, sh).astype(BF16)
    h_scr[FF_TS + 1:FF_TS + 2] = (_rms_mod(xn_ref[...], g, sc, sh) * nm).astype(BF16)

    def up_act(ci, slot):
        hx = h_scr[...].reshape((FF_TS + 2) * BATCH, D_MODEL)
        up = jnp.dot(hx, wu_ref[ci], preferred_element_type=F32)
        up = up.reshape(FF_TS + 2, BATCH, 2 * FF_CH)
        cw = cw_ref[ci]
        cv = cb_ref[ci]
        for k in range(3):
            cv = cv + up[k:k + FF_TS] * cw[k:k + 1]
        cv = cv.reshape(rows, 2 * FF_CH)
        val = cv[:, :FF_CH]
        gt = cv[:, FF_CH:]
        slot[...] = (gt * jax.nn.sigmoid(gt) * val).astype(BF16)

    def down(ci, slot):
        acc_scr[...] += jnp.dot(slot[...], wd_ref[ci], preferred_element_type=F32)

    acc_scr[...] = jnp.zeros_like(acc_scr)
    up_act(0, acta_scr)

    def chunk_pair(it, carry):
        ci = 1 + 2 * it
        up_act(ci, actb_scr)
        down(ci - 1, acta_scr)
        up_act(ci + 1, acta_scr)
        down(ci, actb_scr)
        return carry

    lax.fori_loop(0, (FF_NCH - 1) // 2, chunk_pair, 0)
    down(FF_NCH - 1, acta_scr)
    x2 = xm_ref[...] + g2_ref[...] * acc_scr[...].reshape(FF_TS, BATCH, D_MODEL)
    ms = jnp.mean(x2 * x2, axis=-1, keepdims=True)
    o_ref[...] = x2 * lax.rsqrt(ms + NORM_EPS) * fg_ref[...]


def _ffn(x1, g, sc, sh, g2, fg, wu, cw, cb, wd):
    const = lambda shape: pl.BlockSpec(shape, lambda i: (0,) * len(shape),
                                       pipeline_mode=pl.Buffered(1))
    return pl.pallas_call(
        _ffn_kernel,
        grid=(SEQ // FF_TS,),
        in_specs=[
            pl.BlockSpec((1, BATCH, D_MODEL), lambda i: (jnp.maximum(i * FF_TS - 1, 0), 0, 0)),
            pl.BlockSpec((FF_TS, BATCH, D_MODEL), lambda i: (i, 0, 0)),
            pl.BlockSpec((1, BATCH, D_MODEL),
                         lambda i: (jnp.minimum((i + 1) * FF_TS, SEQ - 1), 0, 0)),
            const((1, D_MODEL)), const((BATCH, D_MODEL)), const((BATCH, D_MODEL)),
            const((BATCH, D_MODEL)), const((1, D_MODEL)),
            const((FF_NCH, D_MODEL, 2 * FF_CH)),
            const((FF_NCH, 3, 2 * FF_CH)),
            const((FF_NCH, 1, 2 * FF_CH)),
            const((FF_NCH, FF_CH, D_MODEL)),
        ],
        out_specs=pl.BlockSpec((FF_TS, BATCH, D_MODEL), lambda i: (i, 0, 0)),
        out_shape=jax.ShapeDtypeStruct((SEQ, BATCH, D_MODEL), F32),
        scratch_shapes=[pltpu.VMEM((FF_TS + 2, BATCH, D_MODEL), BF16),
                        pltpu.VMEM((FF_TS * BATCH, FF_CH), BF16),
                        pltpu.VMEM((FF_TS * BATCH, FF_CH), BF16),
                        pltpu.VMEM((FF_TS * BATCH, D_MODEL), F32)],
        compiler_params=_params(("parallel",)),
        name="ffn",
    )(x1, x1, x1, g, sc, sh, g2, fg, wu, cw, cb, wd)


def _block_diag_tiles(w):
    eye = jnp.eye(N_RNN_BLOCKS, dtype=w.dtype)
    full = jnp.einsum('dnkj,nm->dnkmj', w, eye).reshape(2, D_RNN, D_RNN)
    nt = D_RNN // RG_TC
    return jnp.stack([full[:, c * RG_TC:(c + 1) * RG_TC, c * RG_TC:(c + 1) * RG_TC]
                      for c in range(nt)], axis=1)


def _pair_chunks(a):
    val = a[..., :D_FF].reshape(a.shape[:-1] + (FF_NCH, FF_CH))
    gt = a[..., D_FF:].reshape(a.shape[:-1] + (FF_NCH, FF_CH))
    both = jnp.concatenate([val, gt], axis=-1)
    return jnp.moveaxis(both, -2, 0)


def kernel(x, c, positions, w_ada, b_ada, norm1_g, w_in, conv_rnn_w, conv_rnn_b, w_rg_a, b_rg_a,
           w_rg_i, b_rg_i, rg_lambda, w_rnn_o, lam_q1, lam_k1, lam_q2, lam_k2, subln_g, w_attn_o,
           w_out, norm2_g, w_up, conv_ffn_w, conv_ffn_b, w_down, final_g):
    l = 0
    x3 = x.transpose(1, 0, 2)
    posf = jnp.broadcast_to(positions.astype(F32)[:, :, None], (BATCH, SEQ, LANES))
    inv_freq = ROPE_THETA ** (-jnp.arange(0, ROPE_DIM, 2, dtype=F32) / ROPE_DIM)
    invf = jnp.tile(inv_freq, LANES // ROPE_HALF).reshape(1, LANES)

    mod = _ada(c, w_ada[l], b_ada[l])
    sh1, sc1, g1, sh2, sc2, g2 = [mod[:, m * D_MODEL:(m + 1) * D_MODEL] for m in range(N_MOD)]

    w_in_bf = w_in[l].astype(BF16)
    g_n1 = norm1_g[l].reshape(1, D_MODEL)
    w_sb = jnp.concatenate([w_in_bf[:, :2 * D_MODEL], w_in_bf[:, 5 * D_MODEL:]], axis=1)
    w_bs = w_in_bf[:, 2 * D_MODEL:5 * D_MODEL]
    proj3 = _inproj_sb(x3, g_n1, sc1, sh1, w_sb).reshape(SEQ, BATCH, 4 * D_MODEL)
    qkv = _inproj_bs(x, g_n1, sc1.reshape(BATCH, 1, D_MODEL), sh1.reshape(BATCH, 1, D_MODEL),
                     posf, invf, w_bs)

    wg = jnp.concatenate([_block_diag_tiles(w_rg_a[l]), _block_diag_tiles(w_rg_i[l])],
                         axis=-1)
    wg = (0.5 * wg).astype(BF16)
    hfb = _rglru(proj3, conv_rnn_w[l], conv_rnn_b[l].reshape(1, D_RNN), wg,
                 0.5 * b_rg_a[l].reshape(2, 1, D_RNN), 0.5 * b_rg_i[l].reshape(2, 1, D_RNN),
                 rg_lambda[l].reshape(2, 1, D_RNN))

    lamv = jnp.stack([lam_q1[l], lam_k1[l], lam_q2[l], lam_k2[l]]).astype(F32)
    attn3 = _attn(qkv, lamv, subln_g[l].reshape(V_DIM, 1)).transpose(1, 0, 2)

    x1 = _merge(hfb, proj3, attn3, x3, g1, w_rnn_o[l].astype(BF16), w_attn_o[l].astype(BF16),
                w_out[l].astype(BF16))

    out = _ffn(x1, norm2_g[l].reshape(1, D_MODEL), sc2, sh2, g2, final_g.reshape(1, D_MODEL),
               _pair_chunks(w_up[l]).astype(BF16), _pair_chunks(conv_ffn_w[l]),
               _pair_chunks(conv_ffn_b[l].reshape(1, 2 * D_FF)),
               w_down[l].reshape(FF_NCH, FF_CH, D_MODEL).astype(BF16))
    return out.transpose(1, 0, 2)
```

```python
import functools
import math

import jax
import jax.numpy as jnp
from jax import lax
from jax.experimental import pallas as pl
from jax.experimental.pallas import tpu as pltpu

F32 = jnp.float32
BF16 = jnp.bfloat16

D_MODEL = 1024
BATCH = 16
SEQ = 2048
TOKENS = BATCH * SEQ
D_RNN = D_MODEL
N_RNN_BLOCKS = 16
RNN_BLOCK = D_RNN // N_RNN_BLOCKS
RNN_CONV_W = 4
RNN_CONV_LEFT = 2
RG_C = 8.0
N_HEADS = 8
HEAD_DIM = 64
V_DIM = 2 * HEAD_DIM
ROPE_DIM = HEAD_DIM // 4
ROPE_HALF = ROPE_DIM // 2
ROPE_THETA = 500000.0
D_FF = 2816
N_MOD = 6
NORM_EPS = 1e-6
IN_COLS = 7 * D_MODEL
LAM_INIT = 0.8 - 0.6 * math.exp(-0.3 * 0)
LOG2_E = math.log2(math.e)
Q_SCALE = HEAD_DIM ** -0.5 * LOG2_E

LANES = 128
VMEM_LIMIT = 52 * 1024 * 1024

ADA_TN = 1024
IN_TS = 64
IN_TN = 1024
QKV_TM = 1024
RG_TT = 128
RG_TC = 256
AT_TQ = 512
AT_KB = 256
AT_VROWS = V_DIM + 16
AT_NT = SEQ // AT_TQ
MG_TS = 32
FF_TS = 64
FF_CH = 256
FF_NCH = D_FF // FF_CH


def _params(sem, flags=None):
    return pltpu.CompilerParams(dimension_semantics=sem, vmem_limit_bytes=VMEM_LIMIT, flags=flags)


def _ada_kernel(c_ref, w_ref, b_ref, o_ref):
    c = c_ref[...]
    ca = c * jax.nn.sigmoid(c)
    o_ref[...] = jnp.dot(ca, w_ref[...], preferred_element_type=F32,
                         precision=lax.Precision.HIGHEST) + b_ref[...]


def _ada(c, w, b):
    n = w.shape[1]
    return pl.pallas_call(
        _ada_kernel,
        grid=(n // ADA_TN,),
        in_specs=[pl.BlockSpec((BATCH, D_MODEL), lambda j: (0, 0)),
                  pl.BlockSpec((D_MODEL, ADA_TN), lambda j: (0, j)),
                  pl.BlockSpec((1, ADA_TN), lambda j: (0, j))],
        out_specs=pl.BlockSpec((BATCH, ADA_TN), lambda j: (0, j)),
        out_shape=jax.ShapeDtypeStruct((BATCH, n), F32),
        compiler_params=_params(("arbitrary",)),
        name="adaln",
    )(c, w, b.reshape(1, n))


def _rms_mod(x, g, sc, sh):
    ms = jnp.mean(x * x, axis=-1, keepdims=True)
    y = x * lax.rsqrt(ms + NORM_EPS) * g
    return y * (1.0 + sc) + sh


def _inproj_sb_kernel(x_ref, g_ref, sc_ref, sh_ref, w_ref, o_ref, h_scr):
    h = _rms_mod(x_ref[...], g_ref[...], sc_ref[...], sh_ref[...])
    h_scr[...] = h.reshape(IN_TS * BATCH, D_MODEL).astype(BF16)
    for j in range(w_ref.shape[1] // IN_TN):
        cols = slice(j * IN_TN, (j + 1) * IN_TN)
        o_ref[:, cols] = jnp.dot(h_scr[...], w_ref[:, cols],
                                 preferred_element_type=F32).astype(BF16)


def _resident(shape):
    return pl.BlockSpec(shape, lambda *_: (0,) * len(shape), pipeline_mode=pl.Buffered(1))


def _inproj_sb(x3, g, sc, sh, w_bf):
    tm = IN_TS * BATCH
    ncol = w_bf.shape[1]
    return pl.pallas_call(
        _inproj_sb_kernel,
        grid=(SEQ // IN_TS,),
        in_specs=[pl.BlockSpec((IN_TS, BATCH, D_MODEL), lambda i: (i, 0, 0)),
                  _resident((1, D_MODEL)),
                  _resident((BATCH, D_MODEL)),
                  _resident((BATCH, D_MODEL)),
                  _resident((D_MODEL, ncol))],
        out_specs=pl.BlockSpec((tm, ncol), lambda i: (i, 0)),
        out_shape=jax.ShapeDtypeStruct((TOKENS, ncol), BF16),
        scratch_shapes=[pltpu.VMEM((tm, D_MODEL), BF16)],
        compiler_params=_params(("parallel",)),
        name="inproj_sb",
    )(x3, g, sc, sh, w_bf)


def _inproj_bs_kernel(x_ref, g_ref, sc_ref, sh_ref, pos_ref, invf_ref, w_ref, o_ref,
                      h_scr, tab_scr):
    h = _rms_mod(x_ref[0], g_ref[...], sc_ref[0], sh_ref[0])
    h_scr[...] = h.astype(BF16)
    ang = pos_ref[0] * invf_ref[...]
    c = jnp.cos(ang)
    s = jnp.sin(ang)
    lane = lax.broadcasted_iota(jnp.int32, ang.shape, 1) % HEAD_DIM
    tab_scr[0] = jnp.where(lane < ROPE_DIM, c, 1.0)
    tab_scr[1] = jnp.where(lane < ROPE_HALF, -s, 0.0)
    tab_scr[2] = jnp.where((lane >= ROPE_HALF) & (lane < ROPE_DIM), s, 0.0)

    def rope_tile(j, scale):
        acc = jnp.dot(h_scr[...], w_ref[:, j * D_MODEL:(j + 1) * D_MODEL],
                      preferred_element_type=F32)
        ct, sa, sb = tab_scr[0] * scale, tab_scr[1] * scale, tab_scr[2] * scale
        for cidx in range(D_MODEL // LANES):
            lo = j * D_MODEL + cidx * LANES
            t = acc[:, cidx * LANES:(cidx + 1) * LANES]
            r = (t * ct + pltpu.roll(t, LANES - ROPE_HALF, 1) * sa
                 + pltpu.roll(t, ROPE_HALF, 1) * sb)
            o_ref[0, :, lo:lo + LANES] = r.astype(BF16)

    rope_tile(0, Q_SCALE)
    rope_tile(1, 1.0)
    o_ref[0, :, 2 * D_MODEL:] = jnp.dot(h_scr[...], w_ref[:, 2 * D_MODEL:],
                                        preferred_element_type=F32).astype(BF16)


def _inproj_bs(x, g, sc, sh, posf, invf, w_bf):
    ncol = w_bf.shape[1]
    return pl.pallas_call(
        _inproj_bs_kernel,
        grid=(BATCH, SEQ // QKV_TM),
        in_specs=[pl.BlockSpec((1, QKV_TM, D_MODEL), lambda b, i: (b, i, 0)),
                  _resident((1, D_MODEL)),
                  pl.BlockSpec((1, 1, D_MODEL), lambda b, i: (b, 0, 0)),
                  pl.BlockSpec((1, 1, D_MODEL), lambda b, i: (b, 0, 0)),
                  pl.BlockSpec((1, QKV_TM, LANES), lambda b, i: (b, i, 0)),
                  _resident((1, LANES)),
                  _resident((D_MODEL, ncol))],
        out_specs=pl.BlockSpec((1, QKV_TM, ncol), lambda b, i: (b, i, 0)),
        out_shape=jax.ShapeDtypeStruct((BATCH, SEQ, ncol), BF16),
        scratch_shapes=[pltpu.VMEM((QKV_TM, D_MODEL), BF16),
                        pltpu.VMEM((3, QKV_TM, LANES), F32)],
        compiler_params=_params(("parallel", "parallel")),
        name="inproj_bs",
    )(x, g, sc, sh, posf, invf, w_bf)


def _rglru_kernel(xp_ref, xm_ref, xn_ref, cw_ref, cb_ref, wg_ref, ba_ref, bi_ref, lam_ref,
                  o_ref, a_scr, u_scr, h_scr):
    d = pl.program_id(1)
    t = pl.program_id(2)
    nt = pl.num_programs(2)
    te = t + d * (nt - 1 - 2 * t)
    rows = RG_TT * BATCH

    pm = (te > 0).astype(F32)
    nm = (te < nt - 1).astype(F32)
    xin = jnp.concatenate([xp_ref[...].astype(F32) * pm,
                           xm_ref[...].astype(F32),
                           xn_ref[...].astype(F32) * nm], axis=0)
    cw = cw_ref[...]
    xc = cb_ref[...]
    for k in range(RNN_CONV_W):
        xc = xc + xin[k:k + RG_TT] * cw[k:k + 1]
    x2 = xc.reshape(rows, RG_TC)

    g = jnp.dot(x2.astype(BF16), wg_ref[0, 0], preferred_element_type=F32)
    two_r = 1.0 + jnp.tanh(g[:, :RG_TC] + ba_ref[0])
    two_i = 1.0 + jnp.tanh(g[:, RG_TC:] + bi_ref[0])
    z = -lam_ref[0]
    sp = jnp.maximum(z, 0.0) + jnp.log1p(jnp.exp(-jnp.abs(z)))
    c_ln = (-0.5 * RG_C) * sp
    a = jnp.exp2(two_r * (c_ln * LOG2_E))
    th = jnp.tanh(two_r * c_ln)
    y = -2.0 * th / (1.0 - th)
    u = jnp.where(y > 0.0, y * lax.rsqrt(y), 0.0) * ((0.5 * x2) * two_i)
    a_scr[...] = a.reshape(RG_TT, BATCH, RG_TC)
    u_scr[...] = u.reshape(RG_TT, BATCH, RG_TC)

    @pl.when(t == 0)
    def _():
        h_scr[...] = jnp.zeros_like(h_scr)

    def body(s, h):
        idx = s + d * (RG_TT - 1 - 2 * s)
        h = a_scr[idx] * h + u_scr[idx]
        o_ref[0, idx] = h.astype(BF16)
        return h

    h_scr[...] = lax.fori_loop(0, RG_TT, body, h_scr[...], unroll=8)


def _rglru(proj3, cw, cb, wg, ba, bi, lam):
    nt = SEQ // RG_TT

    def te(d, t):
        return t + d * (nt - 1 - 2 * t)

    return pl.pallas_call(
        _rglru_kernel,
        grid=(D_RNN // RG_TC, 2, nt),
        in_specs=[
            pl.BlockSpec((2, BATCH, RG_TC),
                         lambda c, d, t: (jnp.maximum(te(d, t) * (RG_TT // 2) - 1, 0), 0, c)),
            pl.BlockSpec((RG_TT, BATCH, RG_TC), lambda c, d, t: (te(d, t), 0, c)),
            pl.BlockSpec((1, BATCH, RG_TC),
                         lambda c, d, t: (jnp.minimum((te(d, t) + 1) * RG_TT, SEQ - 1), 0, c)),
            pl.BlockSpec((RNN_CONV_W, RG_TC), lambda c, d, t: (0, c)),
            pl.BlockSpec((1, RG_TC), lambda c, d, t: (0, c)),
            pl.BlockSpec((1, 1, RG_TC, 2 * RG_TC), lambda c, d, t: (d, c, 0, 0)),
            pl.BlockSpec((1, 1, RG_TC), lambda c, d, t: (d, 0, c)),
            pl.BlockSpec((1, 1, RG_TC), lambda c, d, t: (d, 0, c)),
            pl.BlockSpec((1, 1, RG_TC), lambda c, d, t: (d, 0, c)),
        ],
        out_specs=pl.BlockSpec((1, RG_TT, BATCH, RG_TC), lambda c, d, t: (d, te(d, t), 0, c)),
        out_shape=jax.ShapeDtypeStruct((2, SEQ, BATCH, D_RNN), BF16),
        scratch_shapes=[pltpu.VMEM((RG_TT, BATCH, RG_TC), F32),
                        pltpu.VMEM((RG_TT, BATCH, RG_TC), F32),
                        pltpu.VMEM((BATCH, RG_TC), F32)],
        compiler_params=_params(("parallel", "arbitrary", "arbitrary")),
        name="rglru",
    )(proj3, proj3, proj3, cw, cb, wg, ba, bi, lam)


def _zero_after(x):
    bits = lax.bitcast_convert_type(x, jnp.uint32)
    half = jnp.uint32(16)
    return lax.bitcast_convert_type(
        lax.shift_right_logical(lax.shift_right_logical(bits, half), half), F32)


def _attn_kernel(lamv_ref, gt_ref, q_ref, k_ref, v_ref, o_ref, vt_scr, s_scr, m_scr, p_scr):
    g = pl.program_id(0)
    nkb = SEQ // AT_KB
    nq = 2 * AT_TQ
    dn = (((1,), (1,)), ((), ()))

    @pl.when(g == 0)
    def _():
        s_scr[...] = jnp.zeros_like(s_scr)
        m_scr[...] = jnp.zeros_like(m_scr)
        p_scr[...] = jnp.ones_like(p_scr)

    vt_scr[0:V_DIM] = v_ref[0].astype(F32).T.astype(BF16)
    vt_scr[V_DIM:AT_VROWS] = jnp.ones((AT_VROWS - V_DIM, SEQ), BF16)

    lv = lamv_ref[...]
    lam = (jnp.exp(jnp.sum(lv[0:1] * lv[1:2], axis=-1, keepdims=True))
           - jnp.exp(jnp.sum(lv[2:3] * lv[3:4], axis=-1, keepdims=True)) + LAM_INIT)

    for h in range(AT_NT):
        s_new, m_new = s_scr.at[h % 2], m_scr.at[h % 2]
        s_old, m_old = s_scr.at[(h - 1) % 2], m_scr.at[(h - 1) % 2]
        p_new, p_old = p_scr.at[(h - 1) % AT_NT], p_scr.at[h]
        q = q_ref[0, h * AT_TQ:(h + 1) * AT_TQ, :]
        lane = lax.broadcasted_iota(jnp.int32, q.shape, 1)
        zero = jnp.zeros_like(q)
        qcat = jnp.concatenate([jnp.where(lane < HEAD_DIM, q, zero),
                                jnp.where(lane >= HEAD_DIM, q, zero)], axis=0)
        m_prev = m_old[...]
        m8 = jnp.full((8, nq), -jnp.inf, F32)
        for kb in range(nkb):
            rows = slice(kb * AT_KB, (kb + 1) * AT_KB)
            s = lax.dot_general(k_ref[0, rows, :], qcat, dn, preferred_element_type=F32)
            s_new[rows, :] = s
            for r in range(AT_KB // 8):
                m8 = jnp.maximum(m8, s[r * 8:(r + 1) * 8, :])
            m_tied = m_prev + _zero_after(s[AT_KB - 8:AT_KB, :])
            e = jnp.exp2(s_old[rows, :].reshape(AT_KB // 8, 8, nq) - m_tied[None])
            p_new[rows, :] = e.reshape(AT_KB, nq).astype(BF16)
        m_new[...] = jnp.max(m8, axis=0, keepdims=True)
        acc = jnp.dot(vt_scr[...], p_old[...], preferred_element_type=F32)
        o1 = acc[0:V_DIM, :AT_TQ] / acc[V_DIM:V_DIM + 1, :AT_TQ]
        o2 = acc[0:V_DIM, AT_TQ:] / acc[V_DIM:V_DIM + 1, AT_TQ:]
        o = o1 - lam * o2
        ms = jnp.mean(o * o, axis=0, keepdims=True)
        y = o * lax.rsqrt(ms + NORM_EPS) * gt_ref[...]
        o_ref[0, h * AT_TQ:(h + 1) * AT_TQ, :] = (y * (1.0 - LAM_INIT)).T.astype(BF16)


def _attn(qkv, lamv, subln_gt):
    n_heads = BATCH * N_HEADS

    def head_index(hd, col0):
        return hd // N_HEADS, 0, col0 + hd % N_HEADS

    front = lambda g: jnp.minimum(g, n_heads - 1)
    back = lambda g: jnp.maximum(g - 1, 0)
    return pl.pallas_call(
        _attn_kernel,
        grid=(n_heads + 1,),
        in_specs=[
            pl.BlockSpec((4, HEAD_DIM), lambda g: (0, 0)),
            pl.BlockSpec((V_DIM, 1), lambda g: (0, 0)),
            pl.BlockSpec((1, SEQ, LANES), lambda g: head_index(front(g), 0)),
            pl.BlockSpec((1, SEQ, LANES), lambda g: head_index(front(g), N_HEADS)),
            pl.BlockSpec((1, SEQ, LANES), lambda g: head_index(back(g), 2 * N_HEADS)),
        ],
        out_specs=pl.BlockSpec((1, SEQ, LANES), lambda g: head_index(back(g), 0)),
        out_shape=jax.ShapeDtypeStruct((BATCH, SEQ, N_HEADS * V_DIM), BF16),
        scratch_shapes=[pltpu.VMEM((AT_VROWS, SEQ), BF16),
                        pltpu.VMEM((2, SEQ, 2 * AT_TQ), F32),
                        pltpu.VMEM((2, 1, 2 * AT_TQ), F32),
                        pltpu.VMEM((AT_NT, SEQ, 2 * AT_TQ), BF16)],
        compiler_params=_params(("arbitrary",)),
        name="diffattn",
    )(lamv, subln_gt, qkv, qkv, qkv)


def _merge_kernel(hf_ref, hb_ref, yr_ref, ga_ref, gb_ref, at_ref, x_ref, g1_ref,
                  wr_ref, wa_ref, wo_ref, o_ref):
    rows = MG_TS * BATCH
    hr = hf_ref[0].astype(F32) + hb_ref[0].astype(F32)
    ya = (hr * jax.nn.gelu(yr_ref[...].astype(F32))).reshape(rows, D_RNN).astype(BF16)
    br_a = jnp.dot(ya, wr_ref[...], preferred_element_type=F32)
    br_b = jnp.dot(at_ref[...].reshape(rows, D_MODEL), wa_ref[...], preferred_element_type=F32)
    ga = jax.nn.sigmoid(ga_ref[...].reshape(rows, D_MODEL).astype(F32))
    gb = jax.nn.sigmoid(gb_ref[...].reshape(rows, D_MODEL).astype(F32))
    merged = (ga * br_a + gb * br_b).astype(BF16)
    m = jnp.dot(merged, wo_ref[...], preferred_element_type=F32)
    o_ref[...] = x_ref[...] + g1_ref[...] * m.reshape(MG_TS, BATCH, D_MODEL)


def _merge(hfb, proj3, attn3, x3, g1, wr, wa, wo):
    tok = lambda cidx: pl.BlockSpec((MG_TS, BATCH, D_MODEL), lambda i: (i, 0, cidx))
    wspec = pl.BlockSpec((D_MODEL, D_MODEL), lambda i: (0, 0))
    return pl.pallas_call(
        _merge_kernel,
        grid=(SEQ // MG_TS,),
        in_specs=[pl.BlockSpec((1, MG_TS, BATCH, D_RNN), lambda i: (0, i, 0, 0)),
                  pl.BlockSpec((1, MG_TS, BATCH, D_RNN), lambda i: (1, i, 0, 0)),
                  tok(1), tok(2), tok(3), tok(0), tok(0),
                  pl.BlockSpec((BATCH, D_MODEL), lambda i: (0, 0)),
                  wspec, wspec, wspec],
        out_specs=tok(0),
        out_shape=jax.ShapeDtypeStruct((SEQ, BATCH, D_MODEL), F32),
        compiler_params=_params(("parallel",)),
        name="merge",
    )(hfb, hfb, proj3, proj3, proj3, attn3, x3, g1, wr, wa, wo)


def _ffn_kernel(xp_ref, xm_ref, xn_ref, g_ref, sc_ref, sh_ref, g2_ref, fg_ref,
                wu_ref, cw_ref, cb_ref, wd_ref, o_ref, h_scr, acta_scr, actb_scr, acc_scr):
    assert FF_NCH % 2 == 1
    i = pl.program_id(0)
    n = pl.num_programs(0)
    rows = FF_TS * BATCH
    pm = (i > 0).astype(F32)
    nm = (i < n - 1).astype(F32)
    g, sc, sh = g_ref[...], sc_ref[...], sh_ref[...]
    h_scr[0:1] = (_rms_mod(xp_ref[...], g, sc, sh) * pm).astype(BF16)
    h_scr[1:FF_TS + 1] = _rms_mod(xm_ref[...], g, sc, sh).astype(BF16)
    h_scr[FF_TS + 1:FF_TS + 2] = (_rms_mod(xn_ref[...], g, sc, sh) * nm).astype(BF16)

    def up_act(ci, slot):
        hx = h_scr[...].reshape((FF_TS + 2) * BATCH, D_MODEL)
        up = jnp.dot(hx, wu_ref[ci], preferred_element_type=F32)
        up = up.reshape(FF_TS + 2, BATCH, 2 * FF_CH)
        cw = cw_ref[ci]
        cv = cb_ref[ci]
        for k in range(3):
            cv = cv + up[k:k + FF_TS] * cw[k:k + 1]
        cv = cv.reshape(rows, 2 * FF_CH)
        val = cv[:, :FF_CH]
        gt = cv[:, FF_CH:]
        slot[...] = (gt * jax.nn.sigmoid(gt) * val).astype(BF16)

    def down(ci, slot):
        acc_scr[...] += jnp.dot(slot[...], wd_ref[ci], preferred_element_type=F32)

    acc_scr[...] = jnp.zeros_like(acc_scr)
    up_act(0, acta_scr)

    def chunk_pair(it, carry):
        ci = 1 + 2 * it
        up_act(ci, actb_scr)
        down(ci - 1, acta_scr)
        up_act(ci + 1, acta_scr)
        down(ci, actb_scr)
        return carry

    lax.fori_loop(0, (FF_NCH - 1) // 2, chunk_pair, 0)
    down(FF_NCH - 1, acta_scr)
    x2 = xm_ref[...] + g2_ref[...] * acc_scr[...].reshape(FF_TS, BATCH, D_MODEL)
    ms = jnp.mean(x2 * x2, axis=-1, keepdims=True)
    o_ref[...] = x2 * lax.rsqrt(ms + NORM_EPS) * fg_ref[...]


def _ffn(x1, g, sc, sh, g2, fg, wu, cw, cb, wd):
    const = lambda shape: pl.BlockSpec(shape, lambda i: (0,) * len(shape),
                                       pipeline_mode=pl.Buffered(1))
    return pl.pallas_call(
        _ffn_kernel,
        grid=(SEQ // FF_TS,),
        in_specs=[
            pl.BlockSpec((1, BATCH, D_MODEL), lambda i: (jnp.maximum(i * FF_TS - 1, 0), 0, 0)),
            pl.BlockSpec((FF_TS, BATCH, D_MODEL), lambda i: (i, 0, 0)),
            pl.BlockSpec((1, BATCH, D_MODEL),
                         lambda i: (jnp.minimum((i + 1) * FF_TS, SEQ - 1), 0, 0)),
            const((1, D_MODEL)), const((BATCH, D_MODEL)), const((BATCH, D_MODEL)),
            const((BATCH, D_MODEL)), const((1, D_MODEL)),
            const((FF_NCH, D_MODEL, 2 * FF_CH)),
            const((FF_NCH, 3, 2 * FF_CH)),
            const((FF_NCH, 1, 2 * FF_CH)),
            const((FF_NCH, FF_CH, D_MODEL)),
        ],
        out_specs=pl.BlockSpec((FF_TS, BATCH, D_MODEL), lambda i: (i, 0, 0)),
        out_shape=jax.ShapeDtypeStruct((SEQ, BATCH, D_MODEL), F32),
        scratch_shapes=[pltpu.VMEM((FF_TS + 2, BATCH, D_MODEL), BF16),
                        pltpu.VMEM((FF_TS * BATCH, FF_CH), BF16),
                        pltpu.VMEM((FF_TS * BATCH, FF_CH), BF16),
                        pltpu.VMEM((FF_TS * BATCH, D_MODEL), F32)],
        compiler_params=_params(("parallel",)),
        name="ffn",
    )(x1, x1, x1, g, sc, sh, g2, fg, wu, cw, cb, wd)


def _block_diag_tiles(w):
    eye = jnp.eye(N_RNN_BLOCKS, dtype=w.dtype)
    full = jnp.einsum('dnkj,nm->dnkmj', w, eye).reshape(2, D_RNN, D_RNN)
    nt = D_RNN // RG_TC
    return jnp.stack([full[:, c * RG_TC:(c + 1) * RG_TC, c * RG_TC:(c + 1) * RG_TC]
                      for c in range(nt)], axis=1)


def _pair_chunks(a):
    val = a[..., :D_FF].reshape(a.shape[:-1] + (FF_NCH, FF_CH))
    gt = a[..., D_FF:].reshape(a.shape[:-1] + (FF_NCH, FF_CH))
    both = jnp.concatenate([val, gt], axis=-1)
    return jnp.moveaxis(both, -2, 0)


def kernel(x, c, positions, w_ada, b_ada, norm1_g, w_in, conv_rnn_w, conv_rnn_b, w_rg_a, b_rg_a,
           w_rg_i, b_rg_i, rg_lambda, w_rnn_o, lam_q1, lam_k1, lam_q2, lam_k2, subln_g, w_attn_o,
           w_out, norm2_g, w_up, conv_ffn_w, conv_ffn_b, w_down, final_g):
    l = 0
    x3 = x.transpose(1, 0, 2)
    posf = jnp.broadcast_to(positions.astype(F32)[:, :, None], (BATCH, SEQ, LANES))
    inv_freq = ROPE_THETA ** (-jnp.arange(0, ROPE_DIM, 2, dtype=F32) / ROPE_DIM)
    invf = jnp.tile(inv_freq, LANES // ROPE_HALF).reshape(1, LANES)

    mod = _ada(c, w_ada[l], b_ada[l])
    sh1, sc1, g1, sh2, sc2, g2 = [mod[:, m * D_MODEL:(m + 1) * D_MODEL] for m in range(N_MOD)]

    w_in_bf = w_in[l].astype(BF16)
    g_n1 = norm1_g[l].reshape(1, D_MODEL)
    w_sb = jnp.concatenate([w_in_bf[:, :2 * D_MODEL], w_in_bf[:, 5 * D_MODEL:]], axis=1)
    w_bs = w_in_bf[:, 2 * D_MODEL:5 * D_MODEL]
    proj3 = _inproj_sb(x3, g_n1, sc1, sh1, w_sb).reshape(SEQ, BATCH, 4 * D_MODEL)
    qkv = _inproj_bs(x, g_n1, sc1.reshape(BATCH, 1, D_MODEL), sh1.reshape(BATCH, 1, D_MODEL),
                     posf, invf, w_bs)

    wg = jnp.concatenate([_block_diag_tiles(w_rg_a[l]), _block_diag_tiles(w_rg_i[l])],
                         axis=-1)
    wg = (0.5 * wg).astype(BF16)
    hfb = _rglru(proj3, conv_rnn_w[l], conv_rnn_b[l].reshape(1, D_RNN), wg,
                 0.5 * b_rg_a[l].reshape(2, 1, D_RNN), 0.5 * b_rg_i[l].reshape(2, 1, D_RNN),
                 rg_lambda[l].reshape(2, 1, D_RNN))

    lamv = jnp.stack([lam_q1[l], lam_k1[l], lam_q2[l], lam_k2[l]]).astype(F32)
    attn3 = _attn(qkv, lamv, subln_g[l].reshape(V_DIM, 1)).transpose(1, 0, 2)

    x1 = _merge(hfb, proj3, attn3, x3, g1, w_rnn_o[l].astype(BF16), w_attn_o[l].astype(BF16),
                w_out[l].astype(BF16))

    out = _ffn(x1, norm2_g[l].reshape(1, D_MODEL), sc2, sh2, g2, final_g.reshape(1, D_MODEL),
               _pair_chunks(w_up[l]).astype(BF16), _pair_chunks(conv_ffn_w[l]),
               _pair_chunks(conv_ffn_b[l].reshape(1, 2 * D_FF)),
               w_down[l].reshape(FF_NCH, FF_CH, D_MODEL).astype(BF16))
    return out.transpose(1, 0, 2)
```

```python
import math

import jax
import jax.numpy as jnp
from jax import lax
from jax.experimental import pallas as pl
from jax.experimental.pallas import tpu as pltpu

F32 = jnp.float32
BF16 = jnp.bfloat16

D_MODEL = 1024
BATCH = 16
SEQ = 2048
TOKENS = BATCH * SEQ
D_RNN = D_MODEL
N_RNN_BLOCKS = 16
RNN_BLOCK = D_RNN // N_RNN_BLOCKS
RNN_CONV_W = 4
RNN_CONV_LEFT = 2
RG_C = 8.0
N_HEADS = 8
HEAD_DIM = 64
V_DIM = 2 * HEAD_DIM
ROPE_DIM = HEAD_DIM // 4
ROPE_HALF = ROPE_DIM // 2
ROPE_THETA = 500000.0
D_FF = 2816
N_MOD = 6
NORM_EPS = 1e-6
LAM_INIT = 0.8 - 0.6 * math.exp(-0.3 * 0)
LOG2_E = math.log2(math.e)
Q_SCALE = HEAD_DIM ** -0.5 * LOG2_E

LANES = 128
VMEM_LIMIT = 52 * 1024 * 1024

ADA_TN = 1024
IN_TS = 64
QKV_TM = 1024
RG_TT = 128
RG_TC = 256
AT_TQ = 512
AT_KB = 256
AT_VROWS = V_DIM + 16
AT_NT = SEQ // AT_TQ
MG_TS = 32
FF_TS = 64
FF_CH = 256
FF_NCH = D_FF // FF_CH


def _params(sem):
    return pltpu.CompilerParams(dimension_semantics=sem, vmem_limit_bytes=VMEM_LIMIT)


def _resident(shape):
    return pl.BlockSpec(shape, lambda *_: (0,) * len(shape), pipeline_mode=pl.Buffered(1))


def _ada_kernel(c_ref, w_ref, b_ref, o_ref):
    c = c_ref[...]
    ca = c * jax.nn.sigmoid(c)
    o_ref[...] = jnp.dot(ca, w_ref[...], preferred_element_type=F32,
                         precision=lax.Precision.HIGHEST) + b_ref[...]


def _ada(c, w, b):
    n = w.shape[1]
    return pl.pallas_call(
        _ada_kernel,
        grid=(n // ADA_TN,),
        in_specs=[pl.BlockSpec((BATCH, D_MODEL), lambda j: (0, 0)),
                  pl.BlockSpec((D_MODEL, ADA_TN), lambda j: (0, j)),
                  pl.BlockSpec((1, ADA_TN), lambda j: (0, j))],
        out_specs=pl.BlockSpec((BATCH, ADA_TN), lambda j: (0, j)),
        out_shape=jax.ShapeDtypeStruct((BATCH, n), F32),
        compiler_params=_params(("arbitrary",)),
        name="adaln",
    )(c, w, b.reshape(1, n))


def _rms_mod(x, g, sc, sh):
    ms = jnp.mean(x * x, axis=-1, keepdims=True)
    y = x * lax.rsqrt(ms + NORM_EPS) * g
    return y * (1.0 + sc) + sh


def _resident_cols(width, block):
    return pl.BlockSpec((D_MODEL, width), lambda *_: (0, block), pipeline_mode=pl.Buffered(1))


def _inproj_sb_kernel(x_ref, g_ref, sc_ref, sh_ref, wxy_ref, wga_ref, wgb_ref, o_ref, h_scr):
    h = _rms_mod(x_ref[...], g_ref[...], sc_ref[...], sh_ref[...])
    h_scr[...] = h.reshape(IN_TS * BATCH, D_MODEL).astype(BF16)
    for j, (w_ref, c0) in enumerate([(wxy_ref, 0), (wxy_ref, D_MODEL), (wga_ref, 0), (wgb_ref, 0)]):
        o_ref[:, j * D_MODEL:(j + 1) * D_MODEL] = jnp.dot(
            h_scr[...], w_ref[:, c0:c0 + D_MODEL], preferred_element_type=F32).astype(BF16)


def _inproj_sb(x3, g, sc, sh, w_bf):
    tm = IN_TS * BATCH
    ncol = 4 * D_MODEL
    return pl.pallas_call(
        _inproj_sb_kernel,
        grid=(SEQ // IN_TS,),
        in_specs=[pl.BlockSpec((IN_TS, BATCH, D_MODEL), lambda i: (i, 0, 0)),
                  _resident((1, D_MODEL)),
                  _resident((BATCH, D_MODEL)),
                  _resident((BATCH, D_MODEL)),
                  _resident_cols(2 * D_MODEL, 0), _resident_cols(D_MODEL, 5),
                  _resident_cols(D_MODEL, 6)],
        out_specs=pl.BlockSpec((tm, ncol), lambda i: (i, 0)),
        out_shape=jax.ShapeDtypeStruct((TOKENS, ncol), BF16),
        scratch_shapes=[pltpu.VMEM((tm, D_MODEL), BF16)],
        compiler_params=_params(("parallel",)),
        name="inproj_sb",
    )(x3, g, sc, sh, w_bf, w_bf, w_bf)


def _inproj_bs_kernel(x_ref, g_ref, sc_ref, sh_ref, pos_ref, invf_ref, wq_ref, wk_ref, wv_ref,
                      o_ref, h_scr, tab_scr):
    h = _rms_mod(x_ref[0], g_ref[...], sc_ref[0], sh_ref[0])
    h_scr[...] = h.astype(BF16)
    ang = pos_ref[0] * invf_ref[...]
    c = jnp.cos(ang)
    s = jnp.sin(ang)
    lane = lax.broadcasted_iota(jnp.int32, ang.shape, 1) % HEAD_DIM
    tab_scr[0] = jnp.where(lane < ROPE_DIM, c, 1.0)
    tab_scr[1] = jnp.where(lane < ROPE_HALF, -s, 0.0)
    tab_scr[2] = jnp.where((lane >= ROPE_HALF) & (lane < ROPE_DIM), s, 0.0)

    def rope_tile(j, w_ref, scale):
        acc = jnp.dot(h_scr[...], w_ref[...], preferred_element_type=F32)
        ct, sa, sb = tab_scr[0] * scale, tab_scr[1] * scale, tab_scr[2] * scale
        for cidx in range(D_MODEL // LANES):
            lo = j * D_MODEL + cidx * LANES
            t = acc[:, cidx * LANES:(cidx + 1) * LANES]
            r = (t * ct + pltpu.roll(t, LANES - ROPE_HALF, 1) * sa
                 + pltpu.roll(t, ROPE_HALF, 1) * sb)
            o_ref[0, :, lo:lo + LANES] = r.astype(BF16)

    rope_tile(0, wq_ref, Q_SCALE)
    rope_tile(1, wk_ref, 1.0)
    o_ref[0, :, 2 * D_MODEL:] = jnp.dot(h_scr[...], wv_ref[...],
                                        preferred_element_type=F32).astype(BF16)


def _inproj_bs(x, g, sc, sh, posf, invf, w_bf):
    ncol = 3 * D_MODEL
    return pl.pallas_call(
        _inproj_bs_kernel,
        grid=(BATCH, SEQ // QKV_TM),
        in_specs=[pl.BlockSpec((1, QKV_TM, D_MODEL), lambda b, i: (b, i, 0)),
                  _resident((1, D_MODEL)),
                  pl.BlockSpec((1, 1, D_MODEL), lambda b, i: (b, 0, 0)),
                  pl.BlockSpec((1, 1, D_MODEL), lambda b, i: (b, 0, 0)),
                  pl.BlockSpec((1, QKV_TM, LANES), lambda b, i: (b, i, 0)),
                  _resident((1, LANES)),
                  _resident_cols(D_MODEL, 2), _resident_cols(D_MODEL, 3),
                  _resident_cols(D_MODEL, 4)],
        out_specs=pl.BlockSpec((1, QKV_TM, ncol), lambda b, i: (b, i, 0)),
        out_shape=jax.ShapeDtypeStruct((BATCH, SEQ, ncol), BF16),
        scratch_shapes=[pltpu.VMEM((QKV_TM, D_MODEL), BF16),
                        pltpu.VMEM((3, QKV_TM, LANES), F32)],
        compiler_params=_params(("parallel", "parallel")),
        name="inproj_bs",
    )(x, g, sc, sh, posf, invf, w_bf, w_bf, w_bf)


def _rglru_kernel(xp_ref, xm_ref, xn_ref, cw_ref, cb_ref, wg_ref, ba_ref, bi_ref, lam_ref,
                  o_ref, a_scr, u_scr, h_scr):
    d = pl.program_id(1)
    t = pl.program_id(2)
    nt = pl.num_programs(2)
    te = t + d * (nt - 1 - 2 * t)
    rows = RG_TT * BATCH

    pm = (te > 0).astype(F32)
    nm = (te < nt - 1).astype(F32)
    xin = jnp.concatenate([xp_ref[...].astype(F32) * pm,
                           xm_ref[...].astype(F32),
                           xn_ref[...].astype(F32) * nm], axis=0)
    cw = cw_ref[...]
    xc = cb_ref[...]
    for k in range(RNN_CONV_W):
        xc = xc + xin[k:k + RG_TT] * cw[k:k + 1]
    x2 = xc.reshape(rows, RG_TC)

    g = jnp.dot(x2.astype(BF16), wg_ref[0, 0], preferred_element_type=F32)
    two_r = 1.0 + jnp.tanh(g[:, :RG_TC] + ba_ref[0])
    two_i = 1.0 + jnp.tanh(g[:, RG_TC:] + bi_ref[0])
    z = -lam_ref[0]
    sp = jnp.maximum(z, 0.0) + jnp.log1p(jnp.exp(-jnp.abs(z)))
    c_ln = (-0.5 * RG_C) * sp
    a = jnp.exp2(two_r * (c_ln * LOG2_E))
    th = jnp.tanh(two_r * c_ln)
    y = -2.0 * th / (1.0 - th)
    u = jnp.where(y > 0.0, y * lax.rsqrt(y), 0.0) * ((0.5 * x2) * two_i)
    a_scr[...] = a.reshape(RG_TT, BATCH, RG_TC)
    u_scr[...] = u.reshape(RG_TT, BATCH, RG_TC)

    @pl.when(t == 0)
    def _():
        h_scr[...] = jnp.zeros_like(h_scr)

    def body(s, h):
        idx = s + d * (RG_TT - 1 - 2 * s)
        h = a_scr[idx] * h + u_scr[idx]
        o_ref[0, idx] = h.astype(BF16)
        return h

    h_scr[...] = lax.fori_loop(0, RG_TT, body, h_scr[...], unroll=8)


def _rglru(proj3, cw, cb, wg, ba, bi, lam):
    nt = SEQ // RG_TT

    def te(d, t):
        return t + d * (nt - 1 - 2 * t)

    return pl.pallas_call(
        _rglru_kernel,
        grid=(D_RNN // RG_TC, 2, nt),
        in_specs=[
            pl.BlockSpec((2, BATCH, RG_TC),
                         lambda c, d, t: (jnp.maximum(te(d, t) * (RG_TT // 2) - 1, 0), 0, c)),
            pl.BlockSpec((RG_TT, BATCH, RG_TC), lambda c, d, t: (te(d, t), 0, c)),
            pl.BlockSpec((1, BATCH, RG_TC),
                         lambda c, d, t: (jnp.minimum((te(d, t) + 1) * RG_TT, SEQ - 1), 0, c)),
            pl.BlockSpec((RNN_CONV_W, RG_TC), lambda c, d, t: (0, c)),
            pl.BlockSpec((1, RG_TC), lambda c, d, t: (0, c)),
            pl.BlockSpec((1, 1, RG_TC, 2 * RG_TC), lambda c, d, t: (d, c, 0, 0)),
            pl.BlockSpec((1, 1, RG_TC), lambda c, d, t: (d, 0, c)),
            pl.BlockSpec((1, 1, RG_TC), lambda c, d, t: (d, 0, c)),
            pl.BlockSpec((1, 1, RG_TC), lambda c, d, t: (d, 0, c)),
        ],
        out_specs=pl.BlockSpec((1, RG_TT, BATCH, RG_TC), lambda c, d, t: (d, te(d, t), 0, c)),
        out_shape=jax.ShapeDtypeStruct((2, SEQ, BATCH, D_RNN), BF16),
        scratch_shapes=[pltpu.VMEM((RG_TT, BATCH, RG_TC), F32),
                        pltpu.VMEM((RG_TT, BATCH, RG_TC), F32),
                        pltpu.VMEM((BATCH, RG_TC), F32)],
        compiler_params=_params(("parallel", "arbitrary", "arbitrary")),
        name="rglru",
    )(proj3, proj3, proj3, cw, cb, wg, ba, bi, lam)


def _zero_after(x):
    bits = lax.bitcast_convert_type(x, jnp.uint32)
    half = jnp.uint32(16)
    return lax.bitcast_convert_type(
        lax.shift_right_logical(lax.shift_right_logical(bits, half), half), F32)


def _attn_kernel(lamv_ref, gt_ref, q_ref, k_ref, v_ref, o_ref, vt_scr, s_scr, m_scr, p_scr):
    g = pl.program_id(0)
    nkb = SEQ // AT_KB
    nq = 2 * AT_TQ
    dn = (((1,), (1,)), ((), ()))

    @pl.when(g == 0)
    def _():
        s_scr[...] = jnp.zeros_like(s_scr)
        m_scr[...] = jnp.zeros_like(m_scr)
        p_scr[...] = jnp.ones_like(p_scr)

    vt_scr[0:V_DIM] = v_ref[0].astype(F32).T.astype(BF16)
    vt_scr[V_DIM:AT_VROWS] = jnp.ones((AT_VROWS - V_DIM, SEQ), BF16)

    lv = lamv_ref[...]
    lam = (jnp.exp(jnp.sum(lv[0:1] * lv[1:2], axis=-1, keepdims=True))
           - jnp.exp(jnp.sum(lv[2:3] * lv[3:4], axis=-1, keepdims=True)) + LAM_INIT)

    for h in range(AT_NT):
        s_new, m_new = s_scr.at[h % 2], m_scr.at[h % 2]
        s_old, m_old = s_scr.at[(h - 1) % 2], m_scr.at[(h - 1) % 2]
        p_new, p_old = p_scr.at[(h - 1) % AT_NT], p_scr.at[h]
        q = q_ref[0, h * AT_TQ:(h + 1) * AT_TQ, :]
        lane = lax.broadcasted_iota(jnp.int32, q.shape, 1)
        zero = jnp.zeros_like(q)
        qcat = jnp.concatenate([jnp.where(lane < HEAD_DIM, q, zero),
                                jnp.where(lane >= HEAD_DIM, q, zero)], axis=0)
        m_prev = m_old[...]
        m8 = jnp.full((8, nq), -jnp.inf, F32)
        for kb in range(nkb):
            rows = slice(kb * AT_KB, (kb + 1) * AT_KB)
            s = lax.dot_general(k_ref[0, rows, :], qcat, dn, preferred_element_type=F32)
            s_new[rows, :] = s
            for r in range(AT_KB // 8):
                m8 = jnp.maximum(m8, s[r * 8:(r + 1) * 8, :])
            m_tied = m_prev + _zero_after(s[AT_KB - 8:AT_KB, :])
            e = jnp.exp2(s_old[rows, :].reshape(AT_KB // 8, 8, nq) - m_tied[None])
            p_new[rows, :] = e.reshape(AT_KB, nq).astype(BF16)
        m_new[...] = jnp.max(m8, axis=0, keepdims=True)
        acc = jnp.dot(vt_scr[...], p_old[...], preferred_element_type=F32)
        o1 = acc[0:V_DIM, :AT_TQ] / acc[V_DIM:V_DIM + 1, :AT_TQ]
        o2 = acc[0:V_DIM, AT_TQ:] / acc[V_DIM:V_DIM + 1, AT_TQ:]
        o = o1 - lam * o2
        ms = jnp.mean(o * o, axis=0, keepdims=True)
        y = o * lax.rsqrt(ms + NORM_EPS) * gt_ref[...]
        o_ref[0, h * AT_TQ:(h + 1) * AT_TQ, :] = (y * (1.0 - LAM_INIT)).T.astype(BF16)


def _attn(qkv, lamv, subln_gt):
    n_heads = BATCH * N_HEADS

    def head_index(hd, col0):
        return hd // N_HEADS, 0, col0 + hd % N_HEADS

    front = lambda g: jnp.minimum(g, n_heads - 1)
    back = lambda g: jnp.maximum(g - 1, 0)
    return pl.pallas_call(
        _attn_kernel,
        grid=(n_heads + 1,),
        in_specs=[
            pl.BlockSpec((4, HEAD_DIM), lambda g: (0, 0)),
            pl.BlockSpec((V_DIM, 1), lambda g: (0, 0)),
            pl.BlockSpec((1, SEQ, LANES), lambda g: head_index(front(g), 0)),
            pl.BlockSpec((1, SEQ, LANES), lambda g: head_index(front(g), N_HEADS)),
            pl.BlockSpec((1, SEQ, LANES), lambda g: head_index(back(g), 2 * N_HEADS)),
        ],
        out_specs=pl.BlockSpec((1, SEQ, LANES), lambda g: head_index(back(g), 0)),
        out_shape=jax.ShapeDtypeStruct((BATCH, SEQ, N_HEADS * V_DIM), BF16),
        scratch_shapes=[pltpu.VMEM((AT_VROWS, SEQ), BF16),
                        pltpu.VMEM((2, SEQ, 2 * AT_TQ), F32),
                        pltpu.VMEM((2, 1, 2 * AT_TQ), F32),
                        pltpu.VMEM((AT_NT, SEQ, 2 * AT_TQ), BF16)],
        compiler_params=_params(("arbitrary",)),
        name="diffattn",
    )(lamv, subln_gt, qkv, qkv, qkv)


def _merge_kernel(hf_ref, hb_ref, yr_ref, ga_ref, gb_ref, at_ref, x_ref, g1_ref,
                  wr_ref, wa_ref, wo_ref, o_ref):
    rows = MG_TS * BATCH
    hr = hf_ref[0].astype(F32) + hb_ref[0].astype(F32)
    ya = (hr * jax.nn.gelu(yr_ref[...].astype(F32))).reshape(rows, D_RNN).astype(BF16)
    br_a = jnp.dot(ya, wr_ref[...], preferred_element_type=F32)
    br_b = jnp.dot(at_ref[...].reshape(rows, D_MODEL), wa_ref[...], preferred_element_type=F32)
    ga = jax.nn.sigmoid(ga_ref[...].reshape(rows, D_MODEL).astype(F32))
    gb = jax.nn.sigmoid(gb_ref[...].reshape(rows, D_MODEL).astype(F32))
    merged = (ga * br_a + gb * br_b).astype(BF16)
    m = jnp.dot(merged, wo_ref[...], preferred_element_type=F32)
    o_ref[...] = x_ref[...] + g1_ref[...] * m.reshape(MG_TS, BATCH, D_MODEL)


def _merge(hfb, proj3, attn3, x3, g1, wr, wa, wo):
    tok = lambda cidx: pl.BlockSpec((MG_TS, BATCH, D_MODEL), lambda i: (i, 0, cidx))
    wspec = _resident((D_MODEL, D_MODEL))
    return pl.pallas_call(
        _merge_kernel,
        grid=(SEQ // MG_TS,),
        in_specs=[pl.BlockSpec((1, MG_TS, BATCH, D_RNN), lambda i: (0, i, 0, 0)),
                  pl.BlockSpec((1, MG_TS, BATCH, D_RNN), lambda i: (1, i, 0, 0)),
                  tok(1), tok(2), tok(3), tok(0), tok(0),
                  _resident((BATCH, D_MODEL)),
                  wspec, wspec, wspec],
        out_specs=tok(0),
        out_shape=jax.ShapeDtypeStruct((SEQ, BATCH, D_MODEL), F32),
        compiler_params=_params(("parallel",)),
        name="merge",
    )(hfb, hfb, proj3, proj3, proj3, attn3, x3, g1, wr, wa, wo)


def _ffn_kernel(xp_ref, xm_ref, xn_ref, g_ref, sc_ref, sh_ref, g2_ref, fg_ref,
                wu_ref, cw_ref, cb_ref, wd_ref, o_ref, h_scr, acta_scr, actb_scr, acc_scr):
    i = pl.program_id(0)
    n = pl.num_programs(0)
    rows = FF_TS * BATCH
    pm = (i > 0).astype(F32)
    nm = (i < n - 1).astype(F32)
    g, sc, sh = g_ref[...], sc_ref[...], sh_ref[...]
    h_scr[0:1] = (_rms_mod(xp_ref[...], g, sc, sh) * pm).astype(BF16)
    h_scr[1:FF_TS + 1] = _rms_mod(xm_ref[...], g, sc, sh).astype(BF16)
    h_scr[FF_TS + 1:FF_TS + 2] = (_rms_mod(xn_ref[...], g, sc, sh) * nm).astype(BF16)

    def conv(up, lo):
        cw = cw_ref[:, lo:lo + FF_CH]
        cv = cb_ref[:, lo:lo + FF_CH]
        for k in range(3):
            cv = cv + up[k:k + FF_TS] * cw[k:k + 1]
        return cv.reshape(rows, FF_CH)

    def up_act(ci, slot):
        hx = h_scr[...].reshape((FF_TS + 2) * BATCH, D_MODEL)
        lo_v, lo_g = ci * FF_CH, D_FF + ci * FF_CH
        val = conv(jnp.dot(hx, wu_ref[:, lo_v:lo_v + FF_CH], preferred_element_type=F32
                           ).reshape(FF_TS + 2, BATCH, FF_CH), lo_v)
        gt = conv(jnp.dot(hx, wu_ref[:, lo_g:lo_g + FF_CH], preferred_element_type=F32
                          ).reshape(FF_TS + 2, BATCH, FF_CH), lo_g)
        slot[...] = (gt * jax.nn.sigmoid(gt) * val).astype(BF16)

    def down(ci, slot):
        acc_scr[...] += jnp.dot(slot[...], wd_ref[ci * FF_CH:(ci + 1) * FF_CH, :],
                                preferred_element_type=F32)

    slots = (acta_scr, actb_scr)
    acc_scr[...] = jnp.zeros_like(acc_scr)
    up_act(0, slots[0])
    for ci in range(1, FF_NCH):
        up_act(ci, slots[ci % 2])
        down(ci - 1, slots[(ci - 1) % 2])
    down(FF_NCH - 1, slots[(FF_NCH - 1) % 2])
    x2 = xm_ref[...] + g2_ref[...] * acc_scr[...].reshape(FF_TS, BATCH, D_MODEL)
    ms = jnp.mean(x2 * x2, axis=-1, keepdims=True)
    o_ref[...] = x2 * lax.rsqrt(ms + NORM_EPS) * fg_ref[...]


def _ffn(x1, g, sc, sh, g2, fg, wu, cw, cb, wd):
    return pl.pallas_call(
        _ffn_kernel,
        grid=(SEQ // FF_TS,),
        in_specs=[
            pl.BlockSpec((1, BATCH, D_MODEL), lambda i: (jnp.maximum(i * FF_TS - 1, 0), 0, 0)),
            pl.BlockSpec((FF_TS, BATCH, D_MODEL), lambda i: (i, 0, 0)),
            pl.BlockSpec((1, BATCH, D_MODEL),
                         lambda i: (jnp.minimum((i + 1) * FF_TS, SEQ - 1), 0, 0)),
            _resident((1, D_MODEL)), _resident((BATCH, D_MODEL)), _resident((BATCH, D_MODEL)),
            _resident((BATCH, D_MODEL)), _resident((1, D_MODEL)),
            _resident((D_MODEL, 2 * D_FF)),
            _resident((3, 2 * D_FF)),
            _resident((1, 2 * D_FF)),
            _resident((D_FF, D_MODEL)),
        ],
        out_specs=pl.BlockSpec((FF_TS, BATCH, D_MODEL), lambda i: (i, 0, 0)),
        out_shape=jax.ShapeDtypeStruct((SEQ, BATCH, D_MODEL), F32),
        scratch_shapes=[pltpu.VMEM((FF_TS + 2, BATCH, D_MODEL), BF16),
                        pltpu.VMEM((FF_TS * BATCH, FF_CH), BF16),
                        pltpu.VMEM((FF_TS * BATCH, FF_CH), BF16),
                        pltpu.VMEM((FF_TS * BATCH, D_MODEL), F32)],
        compiler_params=_params(("parallel",)),
        name="ffn",
    )(x1, x1, x1, g, sc, sh, g2, fg, wu, cw, cb, wd)


def _gate_tiles(w_a, w_i):
    per = RG_TC // RNN_BLOCK
    nt = D_RNN // RG_TC
    both = jnp.stack([w_a.reshape(2, nt, per, RNN_BLOCK, RNN_BLOCK),
                      w_i.reshape(2, nt, per, RNN_BLOCK, RNN_BLOCK)], axis=4)
    eye = jnp.eye(per, dtype=w_a.dtype)
    full = jnp.einsum('dcjkgn,jm->dcjkgmn', both, eye)
    return full.reshape(2, nt, RG_TC, 2 * RG_TC)


def kernel(x, c, positions, w_ada, b_ada, norm1_g, w_in, conv_rnn_w, conv_rnn_b, w_rg_a, b_rg_a,
           w_rg_i, b_rg_i, rg_lambda, w_rnn_o, lam_q1, lam_k1, lam_q2, lam_k2, subln_g, w_attn_o,
           w_out, norm2_g, w_up, conv_ffn_w, conv_ffn_b, w_down, final_g):
    l = 0
    x3 = x.transpose(1, 0, 2)
    posf = jnp.broadcast_to(positions.astype(F32)[:, :, None], (BATCH, SEQ, LANES))
    inv_freq = ROPE_THETA ** (-jnp.arange(0, ROPE_DIM, 2, dtype=F32) / ROPE_DIM)
    invf = jnp.tile(inv_freq, LANES // ROPE_HALF).reshape(1, LANES)

    mod = _ada(c, w_ada[l], b_ada[l])
    sh1, sc1, g1, sh2, sc2, g2 = [mod[:, m * D_MODEL:(m + 1) * D_MODEL] for m in range(N_MOD)]

    w_in_bf = w_in[l].astype(BF16)
    g_n1 = norm1_g[l].reshape(1, D_MODEL)
    proj3 = _inproj_sb(x3, g_n1, sc1, sh1, w_in_bf).reshape(SEQ, BATCH, 4 * D_MODEL)
    qkv = _inproj_bs(x, g_n1, sc1.reshape(BATCH, 1, D_MODEL), sh1.reshape(BATCH, 1, D_MODEL),
                     posf, invf, w_in_bf)

    wg = (0.5 * _gate_tiles(w_rg_a[l], w_rg_i[l])).astype(BF16)
    hfb = _rglru(proj3, conv_rnn_w[l], conv_rnn_b[l].reshape(1, D_RNN), wg,
                 0.5 * b_rg_a[l].reshape(2, 1, D_RNN), 0.5 * b_rg_i[l].reshape(2, 1, D_RNN),
                 rg_lambda[l].reshape(2, 1, D_RNN))

    lamv = jnp.stack([lam_q1[l], lam_k1[l], lam_q2[l], lam_k2[l]]).astype(F32)
    attn3 = _attn(qkv, lamv, subln_g[l].reshape(V_DIM, 1)).transpose(1, 0, 2)

    x1 = _merge(hfb, proj3, attn3, x3, g1, w_rnn_o[l].astype(BF16), w_attn_o[l].astype(BF16),
                w_out[l].astype(BF16))

    out = _ffn(x1, norm2_g[l].reshape(1, D_MODEL), sc2, sh2, g2, final_g.reshape(1, D_MODEL),
               w_up[l].astype(BF16), conv_ffn_w[l], conv_ffn_b[l].reshape(1, 2 * D_FF),
               w_down[l].astype(BF16))
    return out.transpose(1, 0, 2)
```

```python
import math

import jax
import jax.numpy as jnp
from jax import lax
from jax.experimental import pallas as pl
from jax.experimental.pallas import tpu as pltpu

F32 = jnp.float32
BF16 = jnp.bfloat16

D_MODEL = 1024
BATCH = 16
SEQ = 2048
TOKENS = BATCH * SEQ
D_RNN = D_MODEL
N_RNN_BLOCKS = 16
RNN_BLOCK = D_RNN // N_RNN_BLOCKS
RNN_CONV_W = 4
RNN_CONV_LEFT = 2
RG_C = 8.0
N_HEADS = 8
HEAD_DIM = 64
V_DIM = 2 * HEAD_DIM
ROPE_DIM = HEAD_DIM // 4
ROPE_HALF = ROPE_DIM // 2
ROPE_THETA = 500000.0
D_FF = 2816
N_MOD = 6
NORM_EPS = 1e-6
LAM_INIT = 0.8 - 0.6 * math.exp(-0.3 * 0)
LOG2_E = math.log2(math.e)
Q_SCALE = HEAD_DIM ** -0.5 * LOG2_E

LANES = 128
VMEM_LIMIT = 52 * 1024 * 1024

ADA_TN = 1024
IN_TS = 64
QKV_TM = 1024
RG_TT = 128
RG_TC = 256
AT_TQ = 512
AT_KB = 256
AT_VROWS = V_DIM + 16
AT_NT = SEQ // AT_TQ
MG_TS = 32
FF_TS = 64
FF_CH = 256
FF_NCH = D_FF // FF_CH


def _params(sem):
    return pltpu.CompilerParams(dimension_semantics=sem, vmem_limit_bytes=VMEM_LIMIT)


def _resident(shape):
    return pl.BlockSpec(shape, lambda *_: (0,) * len(shape), pipeline_mode=pl.Buffered(1))


def _ada_kernel(c_ref, w_ref, b_ref, o_ref):
    c = c_ref[...]
    ca = c * jax.nn.sigmoid(c)
    o_ref[...] = jnp.dot(ca, w_ref[...], preferred_element_type=F32,
                         precision=lax.Precision.HIGHEST) + b_ref[...]


def _ada(c, w, b):
    n = w.shape[1]
    return pl.pallas_call(
        _ada_kernel,
        grid=(n // ADA_TN,),
        in_specs=[pl.BlockSpec((BATCH, D_MODEL), lambda j: (0, 0)),
                  pl.BlockSpec((D_MODEL, ADA_TN), lambda j: (0, j)),
                  pl.BlockSpec((1, ADA_TN), lambda j: (0, j))],
        out_specs=pl.BlockSpec((BATCH, ADA_TN), lambda j: (0, j)),
        out_shape=jax.ShapeDtypeStruct((BATCH, n), F32),
        compiler_params=_params(("arbitrary",)),
        name="adaln",
    )(c, w, b.reshape(1, n))


def _rms_mod(x, g, sc, sh):
    ms = jnp.mean(x * x, axis=-1, keepdims=True)
    y = x * lax.rsqrt(ms + NORM_EPS) * g
    return y * (1.0 + sc) + sh


def _resident_cols(width, block):
    return pl.BlockSpec((D_MODEL, width), lambda *_: (0, block), pipeline_mode=pl.Buffered(1))


def _seq_batch_copies(hbm, vmem, sems, step, slot, ts, to_vmem):
    out = []
    for b in range(BATCH):
        h = hbm.at[b, pl.ds(step * ts, ts), :]
        v = vmem.at[slot, :, b, :]
        out.append(pltpu.make_async_copy(h, v, sems.at[slot, b]) if to_vmem
                   else pltpu.make_async_copy(v, h, sems.at[slot, b]))
    return out


def _inproj_sb_kernel(x_hbm, g_ref, sc_ref, sh_ref, wxy_ref, wga_ref, wgb_ref, o_ref, x3_ref,
                      x_buf, x_sem, h_scr):
    i = pl.program_id(0)
    slot = i % 2

    @pl.when(i == 0)
    def _():
        for cp in _seq_batch_copies(x_hbm, x_buf, x_sem, 0, 0, IN_TS, True):
            cp.start()

    @pl.when(i + 1 < pl.num_programs(0))
    def _():
        for cp in _seq_batch_copies(x_hbm, x_buf, x_sem, i + 1, 1 - slot, IN_TS, True):
            cp.start()

    for cp in _seq_batch_copies(x_hbm, x_buf, x_sem, i, slot, IN_TS, True):
        cp.wait()
    x = x_buf[slot]
    x3_ref[...] = x
    h = _rms_mod(x, g_ref[...], sc_ref[...], sh_ref[...])
    h_scr[...] = h.reshape(IN_TS * BATCH, D_MODEL).astype(BF16)
    for j, (w_ref, c0) in enumerate([(wxy_ref, 0), (wxy_ref, D_MODEL), (wga_ref, 0), (wgb_ref, 0)]):
        o_ref[:, j * D_MODEL:(j + 1) * D_MODEL] = jnp.dot(
            h_scr[...], w_ref[:, c0:c0 + D_MODEL], preferred_element_type=F32).astype(BF16)


def _inproj_sb(x, g, sc, sh, w_bf):
    tm = IN_TS * BATCH
    ncol = 4 * D_MODEL
    return pl.pallas_call(
        _inproj_sb_kernel,
        grid=(SEQ // IN_TS,),
        in_specs=[pl.BlockSpec(memory_space=pl.ANY),
                  _resident((1, D_MODEL)),
                  _resident((BATCH, D_MODEL)),
                  _resident((BATCH, D_MODEL)),
                  _resident_cols(2 * D_MODEL, 0), _resident_cols(D_MODEL, 5),
                  _resident_cols(D_MODEL, 6)],
        out_specs=[pl.BlockSpec((tm, ncol), lambda i: (i, 0)),
                   pl.BlockSpec((IN_TS, BATCH, D_MODEL), lambda i: (i, 0, 0))],
        out_shape=[jax.ShapeDtypeStruct((TOKENS, ncol), BF16),
                   jax.ShapeDtypeStruct((SEQ, BATCH, D_MODEL), F32)],
        scratch_shapes=[pltpu.VMEM((2, IN_TS, BATCH, D_MODEL), F32),
                        pltpu.SemaphoreType.DMA((2, BATCH)),
                        pltpu.VMEM((tm, D_MODEL), BF16)],
        compiler_params=_params(("arbitrary",)),
        name="inproj_sb",
    )(x, g, sc, sh, w_bf, w_bf, w_bf)


def _inproj_bs_kernel(x_ref, g_ref, sc_ref, sh_ref, pos_ref, invf_ref, wq_ref, wk_ref, wv_ref,
                      o_ref, h_scr, tab_scr):
    h = _rms_mod(x_ref[0], g_ref[...], sc_ref[0], sh_ref[0])
    h_scr[...] = h.astype(BF16)
    ang = pos_ref[0] * invf_ref[...]
    c = jnp.cos(ang)
    s = jnp.sin(ang)
    lane = lax.broadcasted_iota(jnp.int32, ang.shape, 1) % HEAD_DIM
    tab_scr[0] = jnp.where(lane < ROPE_DIM, c, 1.0)
    tab_scr[1] = jnp.where(lane < ROPE_HALF, -s, 0.0)
    tab_scr[2] = jnp.where((lane >= ROPE_HALF) & (lane < ROPE_DIM), s, 0.0)

    def rope_tile(j, w_ref, scale):
        acc = jnp.dot(h_scr[...], w_ref[...], preferred_element_type=F32)
        ct, sa, sb = tab_scr[0] * scale, tab_scr[1] * scale, tab_scr[2] * scale
        for cidx in range(D_MODEL // LANES):
            lo = j * D_MODEL + cidx * LANES
            t = acc[:, cidx * LANES:(cidx + 1) * LANES]
            r = (t * ct + pltpu.roll(t, LANES - ROPE_HALF, 1) * sa
                 + pltpu.roll(t, ROPE_HALF, 1) * sb)
            o_ref[0, :, lo:lo + LANES] = r.astype(BF16)

    rope_tile(0, wq_ref, Q_SCALE)
    rope_tile(1, wk_ref, 1.0)
    o_ref[0, :, 2 * D_MODEL:] = jnp.dot(h_scr[...], wv_ref[...],
                                        preferred_element_type=F32).astype(BF16)


def _inproj_bs(x, g, sc, sh, posf, invf, w_bf):
    ncol = 3 * D_MODEL
    return pl.pallas_call(
        _inproj_bs_kernel,
        grid=(BATCH, SEQ // QKV_TM),
        in_specs=[pl.BlockSpec((1, QKV_TM, D_MODEL), lambda b, i: (b, i, 0)),
                  _resident((1, D_MODEL)),
                  pl.BlockSpec((1, 1, D_MODEL), lambda b, i: (b, 0, 0)),
                  pl.BlockSpec((1, 1, D_MODEL), lambda b, i: (b, 0, 0)),
                  pl.BlockSpec((1, QKV_TM, LANES), lambda b, i: (b, i, 0)),
                  _resident((1, LANES)),
                  _resident_cols(D_MODEL, 2), _resident_cols(D_MODEL, 3),
                  _resident_cols(D_MODEL, 4)],
        out_specs=pl.BlockSpec((1, QKV_TM, ncol), lambda b, i: (b, i, 0)),
        out_shape=jax.ShapeDtypeStruct((BATCH, SEQ, ncol), BF16),
        scratch_shapes=[pltpu.VMEM((QKV_TM, D_MODEL), BF16),
                        pltpu.VMEM((3, QKV_TM, LANES), F32)],
        compiler_params=_params(("parallel", "parallel")),
        name="inproj_bs",
    )(x, g, sc, sh, posf, invf, w_bf, w_bf, w_bf)


def _rglru_kernel(xp_ref, xm_ref, xn_ref, cw_ref, cb_ref, wg_ref, ba_ref, bi_ref, lam_ref,
                  o_ref, a_scr, u_scr, h_scr):
    d = pl.program_id(1)
    t = pl.program_id(2)
    nt = pl.num_programs(2)
    te = t + d * (nt - 1 - 2 * t)
    rows = RG_TT * BATCH

    pm = (te > 0).astype(F32)
    nm = (te < nt - 1).astype(F32)
    xin = jnp.concatenate([xp_ref[...].astype(F32) * pm,
                           xm_ref[...].astype(F32),
                           xn_ref[...].astype(F32) * nm], axis=0)
    cw = cw_ref[...]
    xc = cb_ref[...]
    for k in range(RNN_CONV_W):
        xc = xc + xin[k:k + RG_TT] * cw[k:k + 1]
    x2 = xc.reshape(rows, RG_TC)

    g = jnp.dot(x2.astype(BF16), wg_ref[0, 0], preferred_element_type=F32)
    two_r = 1.0 + jnp.tanh(g[:, :RG_TC] + ba_ref[0])
    two_i = 1.0 + jnp.tanh(g[:, RG_TC:] + bi_ref[0])
    z = -lam_ref[0]
    sp = jnp.maximum(z, 0.0) + jnp.log1p(jnp.exp(-jnp.abs(z)))
    c_ln = (-0.5 * RG_C) * sp
    a = jnp.exp2(two_r * (c_ln * LOG2_E))
    th = jnp.tanh(two_r * c_ln)
    y = -2.0 * th / (1.0 - th)
    u = jnp.where(y > 0.0, y * lax.rsqrt(y), 0.0) * ((0.5 * x2) * two_i)
    a_scr[...] = a.reshape(RG_TT, BATCH, RG_TC)
    u_scr[...] = u.reshape(RG_TT, BATCH, RG_TC)

    @pl.when(t == 0)
    def _():
        h_scr[...] = jnp.zeros_like(h_scr)

    def body(s, h):
        idx = s + d * (RG_TT - 1 - 2 * s)
        h = a_scr[idx] * h + u_scr[idx]
        o_ref[0, idx] = h.astype(BF16)
        return h

    h_scr[...] = lax.fori_loop(0, RG_TT, body, h_scr[...], unroll=8)


def _rglru(proj3, cw, cb, wg, ba, bi, lam):
    nt = SEQ // RG_TT

    def te(d, t):
        return t + d * (nt - 1 - 2 * t)

    return pl.pallas_call(
        _rglru_kernel,
        grid=(D_RNN // RG_TC, 2, nt),
        in_specs=[
            pl.BlockSpec((2, BATCH, RG_TC),
                         lambda c, d, t: (jnp.maximum(te(d, t) * (RG_TT // 2) - 1, 0), 0, c)),
            pl.BlockSpec((RG_TT, BATCH, RG_TC), lambda c, d, t: (te(d, t), 0, c)),
            pl.BlockSpec((1, BATCH, RG_TC),
                         lambda c, d, t: (jnp.minimum((te(d, t) + 1) * RG_TT, SEQ - 1), 0, c)),
            pl.BlockSpec((RNN_CONV_W, RG_TC), lambda c, d, t: (0, c)),
            pl.BlockSpec((1, RG_TC), lambda c, d, t: (0, c)),
            pl.BlockSpec((1, 1, RG_TC, 2 * RG_TC), lambda c, d, t: (d, c, 0, 0)),
            pl.BlockSpec((1, 1, RG_TC), lambda c, d, t: (d, 0, c)),
            pl.BlockSpec((1, 1, RG_TC), lambda c, d, t: (d, 0, c)),
            pl.BlockSpec((1, 1, RG_TC), lambda c, d, t: (d, 0, c)),
        ],
        out_specs=pl.BlockSpec((1, RG_TT, BATCH, RG_TC), lambda c, d, t: (d, te(d, t), 0, c)),
        out_shape=jax.ShapeDtypeStruct((2, SEQ, BATCH, D_RNN), BF16),
        scratch_shapes=[pltpu.VMEM((RG_TT, BATCH, RG_TC), F32),
                        pltpu.VMEM((RG_TT, BATCH, RG_TC), F32),
                        pltpu.VMEM((BATCH, RG_TC), F32)],
        compiler_params=_params(("parallel", "arbitrary", "arbitrary")),
        name="rglru",
    )(proj3, proj3, proj3, cw, cb, wg, ba, bi, lam)


def _zero_after(x):
    bits = lax.bitcast_convert_type(x, jnp.uint32)
    half = jnp.uint32(16)
    return lax.bitcast_convert_type(
        lax.shift_right_logical(lax.shift_right_logical(bits, half), half), F32)


def _attn_kernel(lamv_ref, gt_ref, q_ref, k_ref, v_ref, o_ref, vt_scr, s_scr, m_scr, p_scr):
    g = pl.program_id(0)
    nkb = SEQ // AT_KB
    nq = 2 * AT_TQ
    dn = (((1,), (1,)), ((), ()))

    @pl.when(g == 0)
    def _():
        s_scr[...] = jnp.zeros_like(s_scr)
        m_scr[...] = jnp.zeros_like(m_scr)
        p_scr[...] = jnp.ones_like(p_scr)

    vt_scr[0:V_DIM] = v_ref[0].astype(F32).T.astype(BF16)
    vt_scr[V_DIM:AT_VROWS] = jnp.ones((AT_VROWS - V_DIM, SEQ), BF16)

    lv = lamv_ref[...]
    lam = (jnp.exp(jnp.sum(lv[0:1] * lv[1:2], axis=-1, keepdims=True))
           - jnp.exp(jnp.sum(lv[2:3] * lv[3:4], axis=-1, keepdims=True)) + LAM_INIT)

    for h in range(AT_NT):
        s_new, m_new = s_scr.at[h % 2], m_scr.at[h % 2]
        s_old, m_old = s_scr.at[(h - 1) % 2], m_scr.at[(h - 1) % 2]
        p_new, p_old = p_scr.at[(h - 1) % AT_NT], p_scr.at[h]
        q = q_ref[0, h * AT_TQ:(h + 1) * AT_TQ, :]
        lane = lax.broadcasted_iota(jnp.int32, q.shape, 1)
        zero = jnp.zeros_like(q)
        qcat = jnp.concatenate([jnp.where(lane < HEAD_DIM, q, zero),
                                jnp.where(lane >= HEAD_DIM, q, zero)], axis=0)
        m_prev = m_old[...]
        m8 = jnp.full((8, nq), -jnp.inf, F32)
        for kb in range(nkb):
            rows = slice(kb * AT_KB, (kb + 1) * AT_KB)
            s = lax.dot_general(k_ref[0, rows, :], qcat, dn, preferred_element_type=F32)
            s_new[rows, :] = s
            for r in range(AT_KB // 8):
                m8 = jnp.maximum(m8, s[r * 8:(r + 1) * 8, :])
            m_tied = m_prev + _zero_after(s[AT_KB - 8:AT_KB, :])
            e = jnp.exp2(s_old[rows, :].reshape(AT_KB // 8, 8, nq) - m_tied[None])
            p_new[rows, :] = e.reshape(AT_KB, nq).astype(BF16)
        m_new[...] = jnp.max(m8, axis=0, keepdims=True)
        acc = jnp.dot(vt_scr[...], p_old[...], preferred_element_type=F32)
        o1 = acc[0:V_DIM, :AT_TQ] / acc[V_DIM:V_DIM + 1, :AT_TQ]
        o2 = acc[0:V_DIM, AT_TQ:] / acc[V_DIM:V_DIM + 1, AT_TQ:]
        o = o1 - lam * o2
        ms = jnp.mean(o * o, axis=0, keepdims=True)
        y = o * lax.rsqrt(ms + NORM_EPS) * gt_ref[...]
        o_ref[0, h * AT_TQ:(h + 1) * AT_TQ, :] = (y * (1.0 - LAM_INIT)).T.astype(BF16)


def _attn(qkv, lamv, subln_gt):
    n_heads = BATCH * N_HEADS

    def head_index(hd, col0):
        return hd // N_HEADS, 0, col0 + hd % N_HEADS

    front = lambda g: jnp.minimum(g, n_heads - 1)
    back = lambda g: jnp.maximum(g - 1, 0)
    return pl.pallas_call(
        _attn_kernel,
        grid=(n_heads + 1,),
        in_specs=[
            pl.BlockSpec((4, HEAD_DIM), lambda g: (0, 0)),
            pl.BlockSpec((V_DIM, 1), lambda g: (0, 0)),
            pl.BlockSpec((1, SEQ, LANES), lambda g: head_index(front(g), 0)),
            pl.BlockSpec((1, SEQ, LANES), lambda g: head_index(front(g), N_HEADS)),
            pl.BlockSpec((1, SEQ, LANES), lambda g: head_index(back(g), 2 * N_HEADS)),
        ],
        out_specs=pl.BlockSpec((1, SEQ, LANES), lambda g: head_index(back(g), 0)),
        out_shape=jax.ShapeDtypeStruct((BATCH, SEQ, N_HEADS * V_DIM), BF16),
        scratch_shapes=[pltpu.VMEM((AT_VROWS, SEQ), BF16),
                        pltpu.VMEM((2, SEQ, 2 * AT_TQ), F32),
                        pltpu.VMEM((2, 1, 2 * AT_TQ), F32),
                        pltpu.VMEM((AT_NT, SEQ, 2 * AT_TQ), BF16)],
        compiler_params=_params(("arbitrary",)),
        name="diffattn",
    )(lamv, subln_gt, qkv, qkv, qkv)


def _merge_kernel(hf_ref, hb_ref, yr_ref, ga_ref, gb_ref, at_ref, x_ref, g1_ref,
                  wr_ref, wa_ref, wo_ref, o_ref):
    rows = MG_TS * BATCH
    hr = hf_ref[0].astype(F32) + hb_ref[0].astype(F32)
    ya = (hr * jax.nn.gelu(yr_ref[...].astype(F32))).reshape(rows, D_RNN).astype(BF16)
    br_a = jnp.dot(ya, wr_ref[...], preferred_element_type=F32)
    br_b = jnp.dot(at_ref[...].reshape(rows, D_MODEL), wa_ref[...], preferred_element_type=F32)
    ga = jax.nn.sigmoid(ga_ref[...].reshape(rows, D_MODEL).astype(F32))
    gb = jax.nn.sigmoid(gb_ref[...].reshape(rows, D_MODEL).astype(F32))
    merged = (ga * br_a + gb * br_b).astype(BF16)
    m = jnp.dot(merged, wo_ref[...], preferred_element_type=F32)
    o_ref[...] = x_ref[...] + g1_ref[...] * m.reshape(MG_TS, BATCH, D_MODEL)


def _merge(hfb, proj3, attn3, x3, g1, wr, wa, wo):
    tok = lambda cidx: pl.BlockSpec((MG_TS, BATCH, D_MODEL), lambda i: (i, 0, cidx))
    wspec = _resident((D_MODEL, D_MODEL))
    return pl.pallas_call(
        _merge_kernel,
        grid=(SEQ // MG_TS,),
        in_specs=[pl.BlockSpec((1, MG_TS, BATCH, D_RNN), lambda i: (0, i, 0, 0)),
                  pl.BlockSpec((1, MG_TS, BATCH, D_RNN), lambda i: (1, i, 0, 0)),
                  tok(1), tok(2), tok(3), tok(0), tok(0),
                  _resident((BATCH, D_MODEL)),
                  wspec, wspec, wspec],
        out_specs=tok(0),
        out_shape=jax.ShapeDtypeStruct((SEQ, BATCH, D_MODEL), F32),
        compiler_params=_params(("parallel",)),
        name="merge",
    )(hfb, hfb, proj3, proj3, proj3, attn3, x3, g1, wr, wa, wo)


def _ffn_kernel(xp_ref, xm_ref, xn_ref, g_ref, sc_ref, sh_ref, g2_ref, fg_ref,
                wu_ref, cw_ref, cb_ref, wd_ref, o_hbm, h_scr, acta_scr, actb_scr, acc_scr,
                o_buf, o_sem):
    i = pl.program_id(0)
    n = pl.num_programs(0)
    rows = FF_TS * BATCH
    slot = i % 2

    @pl.when(i >= 2)
    def _():
        for cp in _seq_batch_copies(o_hbm, o_buf, o_sem, i - 2, slot, FF_TS, False):
            cp.wait()
    pm = (i > 0).astype(F32)
    nm = (i < n - 1).astype(F32)
    g, sc, sh = g_ref[...], sc_ref[...], sh_ref[...]
    h_scr[0:1] = (_rms_mod(xp_ref[...], g, sc, sh) * pm).astype(BF16)
    h_scr[1:FF_TS + 1] = _rms_mod(xm_ref[...], g, sc, sh).astype(BF16)
    h_scr[FF_TS + 1:FF_TS + 2] = (_rms_mod(xn_ref[...], g, sc, sh) * nm).astype(BF16)

    def conv(up, lo):
        cw = cw_ref[:, lo:lo + FF_CH]
        cv = cb_ref[:, lo:lo + FF_CH]
        for k in range(3):
            cv = cv + up[k:k + FF_TS] * cw[k:k + 1]
        return cv.reshape(rows, FF_CH)

    def up_act(ci, slot):
        hx = h_scr[...].reshape((FF_TS + 2) * BATCH, D_MODEL)
        lo_v, lo_g = ci * FF_CH, D_FF + ci * FF_CH
        val = conv(jnp.dot(hx, wu_ref[:, lo_v:lo_v + FF_CH], preferred_element_type=F32
                           ).reshape(FF_TS + 2, BATCH, FF_CH), lo_v)
        gt = conv(jnp.dot(hx, wu_ref[:, lo_g:lo_g + FF_CH], preferred_element_type=F32
                          ).reshape(FF_TS + 2, BATCH, FF_CH), lo_g)
        slot[...] = (gt * jax.nn.sigmoid(gt) * val).astype(BF16)

    def down(ci, slot):
        acc_scr[...] += jnp.dot(slot[...], wd_ref[ci * FF_CH:(ci + 1) * FF_CH, :],
                                preferred_element_type=F32)

    slots = (acta_scr, actb_scr)
    acc_scr[...] = jnp.zeros_like(acc_scr)
    up_act(0, slots[0])
    for ci in range(1, FF_NCH):
        up_act(ci, slots[ci % 2])
        down(ci - 1, slots[(ci - 1) % 2])
    down(FF_NCH - 1, slots[(FF_NCH - 1) % 2])
    x2 = xm_ref[...] + g2_ref[...] * acc_scr[...].reshape(FF_TS, BATCH, D_MODEL)
    ms = jnp.mean(x2 * x2, axis=-1, keepdims=True)
    o_buf[slot] = x2 * lax.rsqrt(ms + NORM_EPS) * fg_ref[...]
    for cp in _seq_batch_copies(o_hbm, o_buf, o_sem, i, slot, FF_TS, False):
        cp.start()

    @pl.when(i == n - 1)
    def _():
        for cp in _seq_batch_copies(o_hbm, o_buf, o_sem, i - 1, 1 - slot, FF_TS, False):
            cp.wait()
        for cp in _seq_batch_copies(o_hbm, o_buf, o_sem, i, slot, FF_TS, False):
            cp.wait()


def _ffn(x1, g, sc, sh, g2, fg, wu, cw, cb, wd):
    assert SEQ // FF_TS >= 2
    return pl.pallas_call(
        _ffn_kernel,
        grid=(SEQ // FF_TS,),
        in_specs=[
            pl.BlockSpec((1, BATCH, D_MODEL), lambda i: (jnp.maximum(i * FF_TS - 1, 0), 0, 0)),
            pl.BlockSpec((FF_TS, BATCH, D_MODEL), lambda i: (i, 0, 0)),
            pl.BlockSpec((1, BATCH, D_MODEL),
                         lambda i: (jnp.minimum((i + 1) * FF_TS, SEQ - 1), 0, 0)),
            _resident((1, D_MODEL)), _resident((BATCH, D_MODEL)), _resident((BATCH, D_MODEL)),
            _resident((BATCH, D_MODEL)), _resident((1, D_MODEL)),
            _resident((D_MODEL, 2 * D_FF)),
            _resident((3, 2 * D_FF)),
            _resident((1, 2 * D_FF)),
            _resident((D_FF, D_MODEL)),
        ],
        out_specs=pl.BlockSpec(memory_space=pl.ANY),
        out_shape=jax.ShapeDtypeStruct((BATCH, SEQ, D_MODEL), F32),
        scratch_shapes=[pltpu.VMEM((FF_TS + 2, BATCH, D_MODEL), BF16),
                        pltpu.VMEM((FF_TS * BATCH, FF_CH), BF16),
                        pltpu.VMEM((FF_TS * BATCH, FF_CH), BF16),
                        pltpu.VMEM((FF_TS * BATCH, D_MODEL), F32),
                        pltpu.VMEM((2, FF_TS, BATCH, D_MODEL), F32),
                        pltpu.SemaphoreType.DMA((2, BATCH))],
        compiler_params=_params(("arbitrary",)),
        name="ffn",
    )(x1, x1, x1, g, sc, sh, g2, fg, wu, cw, cb, wd)


def _gate_tiles(w_a, w_i):
    per = RG_TC // RNN_BLOCK
    nt = D_RNN // RG_TC
    both = jnp.stack([w_a.reshape(2, nt, per, RNN_BLOCK, RNN_BLOCK),
                      w_i.reshape(2, nt, per, RNN_BLOCK, RNN_BLOCK)], axis=4)
    eye = jnp.eye(per, dtype=w_a.dtype)
    full = jnp.einsum('dcjkgn,jm->dcjkgmn', both, eye)
    return full.reshape(2, nt, RG_TC, 2 * RG_TC)


def kernel(x, c, positions, w_ada, b_ada, norm1_g, w_in, conv_rnn_w, conv_rnn_b, w_rg_a, b_rg_a,
           w_rg_i, b_rg_i, rg_lambda, w_rnn_o, lam_q1, lam_k1, lam_q2, lam_k2, subln_g, w_attn_o,
           w_out, norm2_g, w_up, conv_ffn_w, conv_ffn_b, w_down, final_g):
    l = 0
    posf = jnp.broadcast_to(positions.astype(F32)[:, :, None], (BATCH, SEQ, LANES))
    inv_freq = ROPE_THETA ** (-jnp.arange(0, ROPE_DIM, 2, dtype=F32) / ROPE_DIM)
    invf = jnp.tile(inv_freq, LANES // ROPE_HALF).reshape(1, LANES)

    mod = _ada(c, w_ada[l], b_ada[l])
    sh1, sc1, g1, sh2, sc2, g2 = [mod[:, m * D_MODEL:(m + 1) * D_MODEL] for m in range(N_MOD)]

    w_in_bf = w_in[l].astype(BF16)
    g_n1 = norm1_g[l].reshape(1, D_MODEL)
    proj, x3 = _inproj_sb(x, g_n1, sc1, sh1, w_in_bf)
    proj3 = proj.reshape(SEQ, BATCH, 4 * D_MODEL)
    qkv = _inproj_bs(x, g_n1, sc1.reshape(BATCH, 1, D_MODEL), sh1.reshape(BATCH, 1, D_MODEL),
                     posf, invf, w_in_bf)

    wg = (0.5 * _gate_tiles(w_rg_a[l], w_rg_i[l])).astype(BF16)
    hfb = _rglru(proj3, conv_rnn_w[l], conv_rnn_b[l].reshape(1, D_RNN), wg,
                 0.5 * b_rg_a[l].reshape(2, 1, D_RNN), 0.5 * b_rg_i[l].reshape(2, 1, D_RNN),
                 rg_lambda[l].reshape(2, 1, D_RNN))

    lamv = jnp.stack([lam_q1[l], lam_k1[l], lam_q2[l], lam_k2[l]]).astype(F32)
    attn3 = _attn(qkv, lamv, subln_g[l].reshape(V_DIM, 1)).transpose(1, 0, 2)

    x1 = _merge(hfb, proj3, attn3, x3, g1, w_rnn_o[l].astype(BF16), w_attn_o[l].astype(BF16),
                w_out[l].astype(BF16))

    return _ffn(x1, norm2_g[l].reshape(1, D_MODEL), sc2, sh2, g2, final_g.reshape(1, D_MODEL),
                w_up[l].astype(BF16), conv_ffn_w[l], conv_ffn_b[l].reshape(1, 2 * D_FF),
                w_down[l].astype(BF16))
```

```python
import math

import jax
import jax.numpy as jnp
import numpy as np
from jax import lax
from jax.experimental import pallas as pl
from jax.experimental.pallas import tpu as pltpu

F32 = jnp.float32
BF16 = jnp.bfloat16

D_MODEL = 1024
BATCH = 16
SEQ = 2048
TOKENS = BATCH * SEQ
D_RNN = D_MODEL
N_RNN_BLOCKS = 16
RNN_BLOCK = D_RNN // N_RNN_BLOCKS
RNN_CONV_W = 4
RNN_CONV_LEFT = 2
RG_C = 8.0
N_HEADS = 8
HEAD_DIM = 64
V_DIM = 2 * HEAD_DIM
ROPE_DIM = HEAD_DIM // 4
ROPE_HALF = ROPE_DIM // 2
ROPE_THETA = 500000.0
D_FF = 2816
N_MOD = 6
NORM_EPS = 1e-6
LAM_INIT = 0.8 - 0.6 * math.exp(-0.3 * 0)
LOG2_E = math.log2(math.e)
Q_SCALE = HEAD_DIM ** -0.5 * LOG2_E

LANES = 128
VMEM_LIMIT = 52 * 1024 * 1024

ADA_TN = 1024
IN_TS = 64
QKV_TM = 1024
RG_TT = 128
RG_TC = 256
AT_TQ = 512
AT_KB = 256
AT_VROWS = V_DIM + 16
AT_NT = SEQ // AT_TQ
MG_TS = 32
FF_TS = 64
FF_CH = 256
FF_NCH = D_FF // FF_CH


def _params(sem):
    return pltpu.CompilerParams(dimension_semantics=sem, vmem_limit_bytes=VMEM_LIMIT)


def _resident(shape):
    return pl.BlockSpec(shape, lambda *_: (0,) * len(shape), pipeline_mode=pl.Buffered(1))


def _ada_kernel(c_ref, w_ref, b_ref, o_ref):
    c = c_ref[...]
    ca = c * jax.nn.sigmoid(c)
    o_ref[...] = jnp.dot(ca, w_ref[...], preferred_element_type=F32,
                         precision=lax.Precision.HIGHEST) + b_ref[...]


def _ada(c, w, b):
    n = w.shape[1]
    return pl.pallas_call(
        _ada_kernel,
        grid=(n // ADA_TN,),
        in_specs=[pl.BlockSpec((BATCH, D_MODEL), lambda j: (0, 0)),
                  pl.BlockSpec((D_MODEL, ADA_TN), lambda j: (0, j)),
                  pl.BlockSpec((1, ADA_TN), lambda j: (0, j))],
        out_specs=pl.BlockSpec((BATCH, ADA_TN), lambda j: (0, j)),
        out_shape=jax.ShapeDtypeStruct((BATCH, n), F32),
        compiler_params=_params(("arbitrary",)),
        name="adaln",
    )(c, w, b.reshape(1, n))


def _rms_mod(x, g, sc, sh):
    ms = jnp.mean(x * x, axis=-1, keepdims=True)
    y = x * lax.rsqrt(ms + NORM_EPS) * g
    return y * (1.0 + sc) + sh


def _resident_cols(width, block):
    return pl.BlockSpec((D_MODEL, width), lambda *_: (0, block), pipeline_mode=pl.Buffered(1))


def _seq_batch_copies(hbm, vmem, sems, step, slot, ts, to_vmem):
    out = []
    for b in range(BATCH):
        h = hbm.at[b, pl.ds(step * ts, ts), :]
        v = vmem.at[slot, :, b, :]
        out.append(pltpu.make_async_copy(h, v, sems.at[slot, b]) if to_vmem
                   else pltpu.make_async_copy(v, h, sems.at[slot, b]))
    return out


def _inproj_sb_kernel(x_hbm, g_ref, sc_ref, sh_ref, wxy_ref, wga_ref, wgb_ref, o_ref, x3_ref,
                      x_buf, x_sem, h_scr):
    i = pl.program_id(0)
    slot = i % 2

    @pl.when(i == 0)
    def _():
        for cp in _seq_batch_copies(x_hbm, x_buf, x_sem, 0, 0, IN_TS, True):
            cp.start()

    @pl.when(i + 1 < pl.num_programs(0))
    def _():
        for cp in _seq_batch_copies(x_hbm, x_buf, x_sem, i + 1, 1 - slot, IN_TS, True):
            cp.start()

    for cp in _seq_batch_copies(x_hbm, x_buf, x_sem, i, slot, IN_TS, True):
        cp.wait()
    x = x_buf[slot]
    x3_ref[...] = x
    h = _rms_mod(x, g_ref[...], sc_ref[...], sh_ref[...])
    h_scr[...] = h.reshape(IN_TS * BATCH, D_MODEL).astype(BF16)
    for j, (w_ref, c0) in enumerate([(wxy_ref, 0), (wxy_ref, D_MODEL), (wga_ref, 0), (wgb_ref, 0)]):
        o_ref[:, j * D_MODEL:(j + 1) * D_MODEL] = jnp.dot(
            h_scr[...], w_ref[:, c0:c0 + D_MODEL], preferred_element_type=F32).astype(BF16)


def _inproj_sb(x, g, sc, sh, w_bf):
    tm = IN_TS * BATCH
    ncol = 4 * D_MODEL
    return pl.pallas_call(
        _inproj_sb_kernel,
        grid=(SEQ // IN_TS,),
        in_specs=[pl.BlockSpec(memory_space=pl.ANY),
                  _resident((1, D_MODEL)),
                  _resident((BATCH, D_MODEL)),
                  _resident((BATCH, D_MODEL)),
                  _resident_cols(2 * D_MODEL, 0), _resident_cols(D_MODEL, 5),
                  _resident_cols(D_MODEL, 6)],
        out_specs=[pl.BlockSpec((tm, ncol), lambda i: (i, 0)),
                   pl.BlockSpec((IN_TS, BATCH, D_MODEL), lambda i: (i, 0, 0))],
        out_shape=[jax.ShapeDtypeStruct((TOKENS, ncol), BF16),
                   jax.ShapeDtypeStruct((SEQ, BATCH, D_MODEL), F32)],
        scratch_shapes=[pltpu.VMEM((2, IN_TS, BATCH, D_MODEL), F32),
                        pltpu.SemaphoreType.DMA((2, BATCH)),
                        pltpu.VMEM((tm, D_MODEL), BF16)],
        compiler_params=_params(("arbitrary",)),
        name="inproj_sb",
    )(x, g, sc, sh, w_bf, w_bf, w_bf)


def _inproj_bs_kernel(x_ref, g_ref, sc_ref, sh_ref, pos_ref, invf_ref, wq_ref, wk_ref, wv_ref,
                      o_ref, h_scr, tab_scr):
    h = _rms_mod(x_ref[0], g_ref[...], sc_ref[0], sh_ref[0])
    h_scr[...] = h.astype(BF16)
    ang = pos_ref[0] * invf_ref[...]
    c = jnp.cos(ang)
    s = jnp.sin(ang)
    lane = lax.broadcasted_iota(jnp.int32, ang.shape, 1) % HEAD_DIM
    tab_scr[0] = jnp.where(lane < ROPE_DIM, c, 1.0)
    tab_scr[1] = jnp.where(lane < ROPE_HALF, -s, 0.0)
    tab_scr[2] = jnp.where((lane >= ROPE_HALF) & (lane < ROPE_DIM), s, 0.0)

    def rope_tile(j, w_ref, scale):
        acc = jnp.dot(h_scr[...], w_ref[...], preferred_element_type=F32)
        ct, sa, sb = tab_scr[0] * scale, tab_scr[1] * scale, tab_scr[2] * scale
        for cidx in range(D_MODEL // LANES):
            lo = j * D_MODEL + cidx * LANES
            t = acc[:, cidx * LANES:(cidx + 1) * LANES]
            r = (t * ct + pltpu.roll(t, LANES - ROPE_HALF, 1) * sa
                 + pltpu.roll(t, ROPE_HALF, 1) * sb)
            o_ref[0, :, lo:lo + LANES] = r.astype(BF16)

    rope_tile(0, wq_ref, Q_SCALE)
    rope_tile(1, wk_ref, 1.0)
    o_ref[0, :, 2 * D_MODEL:] = jnp.dot(h_scr[...], wv_ref[...],
                                        preferred_element_type=F32).astype(BF16)


def _inproj_bs(x, g, sc, sh, posf, invf, w_bf):
    ncol = 3 * D_MODEL
    return pl.pallas_call(
        _inproj_bs_kernel,
        grid=(BATCH, SEQ // QKV_TM),
        in_specs=[pl.BlockSpec((1, QKV_TM, D_MODEL), lambda b, i: (b, i, 0)),
                  _resident((1, D_MODEL)),
                  pl.BlockSpec((1, 1, D_MODEL), lambda b, i: (b, 0, 0)),
                  pl.BlockSpec((1, 1, D_MODEL), lambda b, i: (b, 0, 0)),
                  pl.BlockSpec((1, QKV_TM, LANES), lambda b, i: (b, i, 0)),
                  _resident((1, LANES)),
                  _resident_cols(D_MODEL, 2), _resident_cols(D_MODEL, 3),
                  _resident_cols(D_MODEL, 4)],
        out_specs=pl.BlockSpec((1, QKV_TM, ncol), lambda b, i: (b, i, 0)),
        out_shape=jax.ShapeDtypeStruct((BATCH, SEQ, ncol), BF16),
        scratch_shapes=[pltpu.VMEM((QKV_TM, D_MODEL), BF16),
                        pltpu.VMEM((3, QKV_TM, LANES), F32)],
        compiler_params=_params(("parallel", "parallel")),
        name="inproj_bs",
    )(x, g, sc, sh, posf, invf, w_bf, w_bf, w_bf)


def _rglru_kernel(xp_ref, xm_ref, xn_ref, cw_ref, cb_ref, wg_ref, ba_ref, bi_ref, lam_ref,
                  o_ref, a_scr, u_scr, h_scr):
    d = pl.program_id(1)
    t = pl.program_id(2)
    nt = pl.num_programs(2)
    te = t + d * (nt - 1 - 2 * t)
    rows = RG_TT * BATCH

    @pl.when(t == 0)
    def _():
        h_scr[...] = jnp.zeros_like(h_scr)

    def chunk(reverse):
        pm = (te > 0).astype(F32)
        nm = (te < nt - 1).astype(F32)
        xin = jnp.concatenate([xp_ref[...].astype(F32) * pm,
                               xm_ref[...].astype(F32),
                               xn_ref[...].astype(F32) * nm], axis=0)
        cw = cw_ref[...]
        xc = cb_ref[...]
        for k in range(RNN_CONV_W):
            xc = xc + xin[k:k + RG_TT] * cw[k:k + 1]
        x2 = xc.reshape(rows, RG_TC)

        g = jnp.dot(x2.astype(BF16), wg_ref[0, 0], preferred_element_type=F32)
        two_r = 1.0 + jnp.tanh(g[:, :RG_TC] + ba_ref[0])
        two_i = 1.0 + jnp.tanh(g[:, RG_TC:] + bi_ref[0])
        z = -lam_ref[0]
        sp = jnp.maximum(z, 0.0) + jnp.log1p(jnp.exp(-jnp.abs(z)))
        c_ln = (-0.5 * RG_C) * sp
        a = jnp.exp2(two_r * (c_ln * LOG2_E))
        th = jnp.tanh(two_r * c_ln)
        y = -2.0 * th / (1.0 - th)
        u = jnp.where(y > 0.0, y * lax.rsqrt(y), 0.0) * ((0.5 * x2) * two_i)
        a_scr[...] = a.reshape(RG_TT, BATCH, RG_TC)
        u_scr[...] = u.reshape(RG_TT, BATCH, RG_TC)
        h = h_scr[...]
        for s in range(RG_TT):
            idx = RG_TT - 1 - s if reverse else s
            h = a_scr[idx] * h + u_scr[idx]
            o_ref[0, idx] = h.astype(BF16)
        h_scr[...] = h

    @pl.when(d == 0)
    def _():
        chunk(False)

    @pl.when(d == 1)
    def _():
        chunk(True)


def _rglru(proj3, cw, cb, wg, ba, bi, lam):
    nt = SEQ // RG_TT

    def te(d, t):
        return t + d * (nt - 1 - 2 * t)

    return pl.pallas_call(
        _rglru_kernel,
        grid=(D_RNN // RG_TC, 2, nt),
        in_specs=[
            pl.BlockSpec((2, BATCH, RG_TC),
                         lambda c, d, t: (jnp.maximum(te(d, t) * (RG_TT // 2) - 1, 0), 0, c)),
            pl.BlockSpec((RG_TT, BATCH, RG_TC), lambda c, d, t: (te(d, t), 0, c)),
            pl.BlockSpec((1, BATCH, RG_TC),
                         lambda c, d, t: (jnp.minimum((te(d, t) + 1) * RG_TT, SEQ - 1), 0, c)),
            pl.BlockSpec((RNN_CONV_W, RG_TC), lambda c, d, t: (0, c)),
            pl.BlockSpec((1, RG_TC), lambda c, d, t: (0, c)),
            pl.BlockSpec((1, 1, RG_TC, 2 * RG_TC), lambda c, d, t: (d, c, 0, 0)),
            pl.BlockSpec((1, 1, RG_TC), lambda c, d, t: (d, 0, c)),
            pl.BlockSpec((1, 1, RG_TC), lambda c, d, t: (d, 0, c)),
            pl.BlockSpec((1, 1, RG_TC), lambda c, d, t: (d, 0, c)),
        ],
        out_specs=pl.BlockSpec((1, RG_TT, BATCH, RG_TC), lambda c, d, t: (d, te(d, t), 0, c)),
        out_shape=jax.ShapeDtypeStruct((2, SEQ, BATCH, D_RNN), BF16),
        scratch_shapes=[pltpu.VMEM((RG_TT, BATCH, RG_TC), F32),
                        pltpu.VMEM((RG_TT, BATCH, RG_TC), F32),
                        pltpu.VMEM((BATCH, RG_TC), F32)],
        compiler_params=_params(("parallel", "arbitrary", "arbitrary")),
        name="rglru",
    )(proj3, proj3, proj3, cw, cb, wg, ba, bi, lam)


def _zero_after(x):
    bits = lax.bitcast_convert_type(x, jnp.uint32)
    half = jnp.uint32(16)
    return lax.bitcast_convert_type(
        lax.shift_right_logical(lax.shift_right_logical(bits, half), half), F32)


def _attn_kernel(lamv_ref, gt_ref, q_ref, k_ref, v_ref, o_ref, vt_scr, s_scr, m_scr, p_scr):
    g = pl.program_id(0)
    nkb = SEQ // AT_KB
    nq = 2 * AT_TQ
    dn = (((1,), (1,)), ((), ()))

    @pl.when(g == 0)
    def _():
        s_scr[...] = jnp.zeros_like(s_scr)
        m_scr[...] = jnp.zeros_like(m_scr)
        p_scr[...] = jnp.ones_like(p_scr)

    vt_scr[0:V_DIM] = v_ref[0].astype(F32).T.astype(BF16)
    vt_scr[V_DIM:AT_VROWS] = jnp.ones((AT_VROWS - V_DIM, SEQ), BF16)

    lv = lamv_ref[...]
    lam = (jnp.exp(jnp.sum(lv[0:1] * lv[1:2], axis=-1, keepdims=True))
           - jnp.exp(jnp.sum(lv[2:3] * lv[3:4], axis=-1, keepdims=True)) + LAM_INIT)

    for h in range(AT_NT):
        s_new, m_new = s_scr.at[h % 2], m_scr.at[h % 2]
        s_old, m_old = s_scr.at[(h - 1) % 2], m_scr.at[(h - 1) % 2]
        p_new, p_old = p_scr.at[(h - 1) % AT_NT], p_scr.at[h]
        q = q_ref[0, h * AT_TQ:(h + 1) * AT_TQ, :]
        lane = lax.broadcasted_iota(jnp.int32, q.shape, 1)
        zero = jnp.zeros_like(q)
        qcat = jnp.concatenate([jnp.where(lane < HEAD_DIM, q, zero),
                                jnp.where(lane >= HEAD_DIM, q, zero)], axis=0)
        m_prev = m_old[...]
        m8 = jnp.full((8, nq), -jnp.inf, F32)
        for kb in range(nkb):
            rows = slice(kb * AT_KB, (kb + 1) * AT_KB)
            s = lax.dot_general(k_ref[0, rows, :], qcat, dn, preferred_element_type=F32)
            s_new[rows, :] = s
            for r in range(AT_KB // 8):
                m8 = jnp.maximum(m8, s[r * 8:(r + 1) * 8, :])
            m_tied = m_prev + _zero_after(s[AT_KB - 8:AT_KB, :])
            e = jnp.exp2(s_old[rows, :].reshape(AT_KB // 8, 8, nq) - m_tied[None])
            p_new[rows, :] = e.reshape(AT_KB, nq).astype(BF16)
        m_new[...] = jnp.max(m8, axis=0, keepdims=True)
        acc = jnp.dot(vt_scr[...], p_old[...], preferred_element_type=F32)
        o1 = acc[0:V_DIM, :AT_TQ] / acc[V_DIM:V_DIM + 1, :AT_TQ]
        o2 = acc[0:V_DIM, AT_TQ:] / acc[V_DIM:V_DIM + 1, AT_TQ:]
        o = o1 - lam * o2
        ms = jnp.mean(o * o, axis=0, keepdims=True)
        y = o * lax.rsqrt(ms + NORM_EPS) * gt_ref[...]
        o_ref[0, h * AT_TQ:(h + 1) * AT_TQ, :] = (y * (1.0 - LAM_INIT)).T.astype(BF16)


def _attn(qkv, lamv, subln_gt):
    n_heads = BATCH * N_HEADS

    def head_index(hd, col0):
        return hd // N_HEADS, 0, col0 + hd % N_HEADS

    front = lambda g: jnp.minimum(g, n_heads - 1)
    back = lambda g: jnp.maximum(g - 1, 0)
    return pl.pallas_call(
        _attn_kernel,
        grid=(n_heads + 1,),
        in_specs=[
            pl.BlockSpec((4, HEAD_DIM), lambda g: (0, 0)),
            pl.BlockSpec((V_DIM, 1), lambda g: (0, 0)),
            pl.BlockSpec((1, SEQ, LANES), lambda g: head_index(front(g), 0)),
            pl.BlockSpec((1, SEQ, LANES), lambda g: head_index(front(g), N_HEADS)),
            pl.BlockSpec((1, SEQ, LANES), lambda g: head_index(back(g), 2 * N_HEADS)),
        ],
        out_specs=pl.BlockSpec((1, SEQ, LANES), lambda g: head_index(back(g), 0)),
        out_shape=jax.ShapeDtypeStruct((BATCH, SEQ, N_HEADS * V_DIM), BF16),
        scratch_shapes=[pltpu.VMEM((AT_VROWS, SEQ), BF16),
                        pltpu.VMEM((2, SEQ, 2 * AT_TQ), F32),
                        pltpu.VMEM((2, 1, 2 * AT_TQ), F32),
                        pltpu.VMEM((AT_NT, SEQ, 2 * AT_TQ), BF16)],
        compiler_params=_params(("arbitrary",)),
        name="diffattn",
    )(lamv, subln_gt, qkv, qkv, qkv)


def _merge_kernel(hf_ref, hb_ref, yr_ref, ga_ref, gb_ref, at_ref, perm_ref, x_ref, g1_ref,
                  wr_ref, wa_ref, wo_ref, o_ref):
    rows = MG_TS * BATCH
    hr = hf_ref[0].astype(F32) + hb_ref[0].astype(F32)
    ya = (hr * jax.nn.gelu(yr_ref[...].astype(F32))).reshape(rows, D_RNN).astype(BF16)
    br_a = jnp.dot(ya, wr_ref[...], preferred_element_type=F32)
    at_sb = jnp.dot(perm_ref[...], at_ref[...].reshape(rows, D_MODEL),
                    preferred_element_type=F32).astype(BF16)
    br_b = jnp.dot(at_sb, wa_ref[...], preferred_element_type=F32)
    ga = jax.nn.sigmoid(ga_ref[...].reshape(rows, D_MODEL).astype(F32))
    gb = jax.nn.sigmoid(gb_ref[...].reshape(rows, D_MODEL).astype(F32))
    merged = (ga * br_a + gb * br_b).astype(BF16)
    m = jnp.dot(merged, wo_ref[...], preferred_element_type=F32)
    o_ref[...] = x_ref[...] + g1_ref[...] * m.reshape(MG_TS, BATCH, D_MODEL)


def _merge(hfb, proj3, attn, x3, g1, wr, wa, wo):
    tok = lambda cidx: pl.BlockSpec((MG_TS, BATCH, D_MODEL), lambda i: (i, 0, cidx))
    wspec = _resident((D_MODEL, D_MODEL))
    rows = MG_TS * BATCH
    r_out = np.arange(rows)
    perm = np.zeros((rows, rows), np.float32)
    perm[r_out, (r_out % BATCH) * MG_TS + r_out // BATCH] = 1.0
    return pl.pallas_call(
        _merge_kernel,
        grid=(SEQ // MG_TS,),
        in_specs=[pl.BlockSpec((1, MG_TS, BATCH, D_RNN), lambda i: (0, i, 0, 0)),
                  pl.BlockSpec((1, MG_TS, BATCH, D_RNN), lambda i: (1, i, 0, 0)),
                  tok(1), tok(2), tok(3),
                  pl.BlockSpec((BATCH, MG_TS, D_MODEL), lambda i: (0, i, 0)),
                  _resident((rows, rows)),
                  tok(0),
                  _resident((BATCH, D_MODEL)),
                  wspec, wspec, wspec],
        out_specs=tok(0),
        out_shape=jax.ShapeDtypeStruct((SEQ, BATCH, D_MODEL), F32),
        compiler_params=_params(("parallel",)),
        name="merge",
    )(hfb, hfb, proj3, proj3, proj3, attn, jnp.asarray(perm, BF16), x3, g1, wr, wa, wo)


def _ffn_kernel(xp_ref, xm_ref, xn_ref, g_ref, sc_ref, sh_ref, g2_ref, fg_ref,
                wu_ref, cw_ref, cb_ref, wd_ref, o_hbm, h_scr, acta_scr, actb_scr, acc_scr,
                o_buf, o_sem):
    i = pl.program_id(0)
    n = pl.num_programs(0)
    rows = FF_TS * BATCH
    slot = i % 2

    @pl.when(i >= 2)
    def _():
        for cp in _seq_batch_copies(o_hbm, o_buf, o_sem, i - 2, slot, FF_TS, False):
            cp.wait()
    pm = (i > 0).astype(F32)
    nm = (i < n - 1).astype(F32)
    g, sc, sh = g_ref[...], sc_ref[...], sh_ref[...]
    h_scr[0:1] = (_rms_mod(xp_ref[...], g, sc, sh) * pm).astype(BF16)
    h_scr[1:FF_TS + 1] = _rms_mod(xm_ref[...], g, sc, sh).astype(BF16)
    h_scr[FF_TS + 1:FF_TS + 2] = (_rms_mod(xn_ref[...], g, sc, sh) * nm).astype(BF16)

    def conv(up, lo):
        cw = cw_ref[:, lo:lo + FF_CH]
        cv = cb_ref[:, lo:lo + FF_CH]
        for k in range(3):
            cv = cv + up[k:k + FF_TS] * cw[k:k + 1]
        return cv.reshape(rows, FF_CH)

    def up_act(ci, slot):
        hx = h_scr[...].reshape((FF_TS + 2) * BATCH, D_MODEL)
        lo_v, lo_g = ci * FF_CH, D_FF + ci * FF_CH
        val = conv(jnp.dot(hx, wu_ref[:, lo_v:lo_v + FF_CH], preferred_element_type=F32
                           ).reshape(FF_TS + 2, BATCH, FF_CH), lo_v)
        gt = conv(jnp.dot(hx, wu_ref[:, lo_g:lo_g + FF_CH], preferred_element_type=F32
                          ).reshape(FF_TS + 2, BATCH, FF_CH), lo_g)
        slot[...] = (gt * jax.nn.sigmoid(gt) * val).astype(BF16)

    def down(ci, slot):
        acc_scr[...] += jnp.dot(slot[...], wd_ref[ci * FF_CH:(ci + 1) * FF_CH, :],
                                preferred_element_type=F32)

    slots = (acta_scr, actb_scr)
    acc_scr[...] = jnp.zeros_like(acc_scr)
    up_act(0, slots[0])
    for ci in range(1, FF_NCH):
        up_act(ci, slots[ci % 2])
        down(ci - 1, slots[(ci - 1) % 2])
    down(FF_NCH - 1, slots[(FF_NCH - 1) % 2])
    x2 = xm_ref[...] + g2_ref[...] * acc_scr[...].reshape(FF_TS, BATCH, D_MODEL)
    ms = jnp.mean(x2 * x2, axis=-1, keepdims=True)
    o_buf[slot] = x2 * lax.rsqrt(ms + NORM_EPS) * fg_ref[...]
    for cp in _seq_batch_copies(o_hbm, o_buf, o_sem, i, slot, FF_TS, False):
        cp.start()

    @pl.when(i == n - 1)
    def _():
        for cp in _seq_batch_copies(o_hbm, o_buf, o_sem, i - 1, 1 - slot, FF_TS, False):
            cp.wait()
        for cp in _seq_batch_copies(o_hbm, o_buf, o_sem, i, slot, FF_TS, False):
            cp.wait()


def _ffn(x1, g, sc, sh, g2, fg, wu, cw, cb, wd):
    assert SEQ // FF_TS >= 2
    return pl.pallas_call(
        _ffn_kernel,
        grid=(SEQ // FF_TS,),
        in_specs=[
            pl.BlockSpec((1, BATCH, D_MODEL), lambda i: (jnp.maximum(i * FF_TS - 1, 0), 0, 0)),
            pl.BlockSpec((FF_TS, BATCH, D_MODEL), lambda i: (i, 0, 0)),
            pl.BlockSpec((1, BATCH, D_MODEL),
                         lambda i: (jnp.minimum((i + 1) * FF_TS, SEQ - 1), 0, 0)),
            _resident((1, D_MODEL)), _resident((BATCH, D_MODEL)), _resident((BATCH, D_MODEL)),
            _resident((BATCH, D_MODEL)), _resident((1, D_MODEL)),
            _resident((D_MODEL, 2 * D_FF)),
            _resident((3, 2 * D_FF)),
            _resident((1, 2 * D_FF)),
            _resident((D_FF, D_MODEL)),
        ],
        out_specs=pl.BlockSpec(memory_space=pl.ANY),
        out_shape=jax.ShapeDtypeStruct((BATCH, SEQ, D_MODEL), F32),
        scratch_shapes=[pltpu.VMEM((FF_TS + 2, BATCH, D_MODEL), BF16),
                        pltpu.VMEM((FF_TS * BATCH, FF_CH), BF16),
                        pltpu.VMEM((FF_TS * BATCH, FF_CH), BF16),
                        pltpu.VMEM((FF_TS * BATCH, D_MODEL), F32),
                        pltpu.VMEM((2, FF_TS, BATCH, D_MODEL), F32),
                        pltpu.SemaphoreType.DMA((2, BATCH))],
        compiler_params=_params(("arbitrary",)),
        name="ffn",
    )(x1, x1, x1, g, sc, sh, g2, fg, wu, cw, cb, wd)


def _gate_tiles(w_a, w_i):
    per = RG_TC // RNN_BLOCK
    nt = D_RNN // RG_TC
    both = jnp.stack([w_a.reshape(2, nt, per, RNN_BLOCK, RNN_BLOCK),
                      w_i.reshape(2, nt, per, RNN_BLOCK, RNN_BLOCK)], axis=4)
    eye = jnp.eye(per, dtype=w_a.dtype)
    full = jnp.einsum('dcjkgn,jm->dcjkgmn', both, eye)
    return full.reshape(2, nt, RG_TC, 2 * RG_TC)


def kernel(x, c, positions, w_ada, b_ada, norm1_g, w_in, conv_rnn_w, conv_rnn_b, w_rg_a, b_rg_a,
           w_rg_i, b_rg_i, rg_lambda, w_rnn_o, lam_q1, lam_k1, lam_q2, lam_k2, subln_g, w_attn_o,
           w_out, norm2_g, w_up, conv_ffn_w, conv_ffn_b, w_down, final_g):
    l = 0
    posf = jnp.broadcast_to(positions.astype(F32)[:, :, None], (BATCH, SEQ, LANES))
    inv_freq = ROPE_THETA ** (-jnp.arange(0, ROPE_DIM, 2, dtype=F32) / ROPE_DIM)
    invf = jnp.tile(inv_freq, LANES // ROPE_HALF).reshape(1, LANES)

    mod = _ada(c, w_ada[l], b_ada[l])
    sh1, sc1, g1, sh2, sc2, g2 = [mod[:, m * D_MODEL:(m + 1) * D_MODEL] for m in range(N_MOD)]

    w_in_bf = w_in[l].astype(BF16)
    g_n1 = norm1_g[l].reshape(1, D_MODEL)
    proj, x3 = _inproj_sb(x, g_n1, sc1, sh1, w_in_bf)
    proj3 = proj.reshape(SEQ, BATCH, 4 * D_MODEL)
    qkv = _inproj_bs(x, g_n1, sc1.reshape(BATCH, 1, D_MODEL), sh1.reshape(BATCH, 1, D_MODEL),
                     posf, invf, w_in_bf)

    wg = (0.5 * _gate_tiles(w_rg_a[l], w_rg_i[l])).astype(BF16)
    hfb = _rglru(proj3, conv_rnn_w[l], conv_rnn_b[l].reshape(1, D_RNN), wg,
                 0.5 * b_rg_a[l].reshape(2, 1, D_RNN), 0.5 * b_rg_i[l].reshape(2, 1, D_RNN),
                 rg_lambda[l].reshape(2, 1, D_RNN))

    lamv = jnp.stack([lam_q1[l], lam_k1[l], lam_q2[l], lam_k2[l]]).astype(F32)
    attn = _attn(qkv, lamv, subln_g[l].reshape(V_DIM, 1))

    x1 = _merge(hfb, proj3, attn, x3, g1, w_rnn_o[l].astype(BF16), w_attn_o[l].astype(BF16),
                w_out[l].astype(BF16))

    return _ffn(x1, norm2_g[l].reshape(1, D_MODEL), sc2, sh2, g2, final_g.reshape(1, D_MODEL),
                w_up[l].astype(BF16), conv_ffn_w[l], conv_ffn_b[l].reshape(1, 2 * D_FF),
                w_down[l].astype(BF16))
```

```python
import math

import jax
import jax.numpy as jnp
import numpy as np
from jax import lax
from jax.experimental import pallas as pl
from jax.experimental.pallas import tpu as pltpu

F32 = jnp.float32
BF16 = jnp.bfloat16

D_MODEL = 1024
BATCH = 16
SEQ = 2048
TOKENS = BATCH * SEQ
D_RNN = D_MODEL
N_RNN_BLOCKS = 16
RNN_BLOCK = D_RNN // N_RNN_BLOCKS
RNN_CONV_W = 4
RNN_CONV_LEFT = 2
RG_C = 8.0
N_HEADS = 8
HEAD_DIM = 64
V_DIM = 2 * HEAD_DIM
ROPE_DIM = HEAD_DIM // 4
ROPE_HALF = ROPE_DIM // 2
ROPE_THETA = 500000.0
D_FF = 2816
N_MOD = 6
NORM_EPS = 1e-6
LAM_INIT = 0.8 - 0.6 * math.exp(-0.3 * 0)
LOG2_E = math.log2(math.e)
Q_SCALE = HEAD_DIM ** -0.5 * LOG2_E

LANES = 128
VMEM_LIMIT = 52 * 1024 * 1024

ADA_TN = 1024
IN_TS = 64
QKV_TM = 1024
RG_TT = 128
RG_TC = 256
AT_TQ = 512
AT_KB = 256
AT_VROWS = V_DIM + 16
AT_NT = SEQ // AT_TQ
MG_TS = 32
FF_TS = 64
FF_CH = 256
FF_NCH = D_FF // FF_CH


def _params(sem):
    return pltpu.CompilerParams(dimension_semantics=sem, vmem_limit_bytes=VMEM_LIMIT)


def _resident(shape):
    return pl.BlockSpec(shape, lambda *_: (0,) * len(shape), pipeline_mode=pl.Buffered(1))


def _ada_kernel(c_ref, w_ref, b_ref, o_ref):
    c = c_ref[...]
    ca = c * jax.nn.sigmoid(c)
    o_ref[...] = jnp.dot(ca, w_ref[...], preferred_element_type=F32,
                         precision=lax.Precision.HIGHEST) + b_ref[...]


def _ada(c, w, b):
    n = w.shape[1]
    return pl.pallas_call(
        _ada_kernel,
        grid=(n // ADA_TN,),
        in_specs=[pl.BlockSpec((BATCH, D_MODEL), lambda j: (0, 0)),
                  pl.BlockSpec((D_MODEL, ADA_TN), lambda j: (0, j)),
                  pl.BlockSpec((1, ADA_TN), lambda j: (0, j))],
        out_specs=pl.BlockSpec((BATCH, ADA_TN), lambda j: (0, j)),
        out_shape=jax.ShapeDtypeStruct((BATCH, n), F32),
        compiler_params=_params(("arbitrary",)),
        name="adaln",
    )(c, w, b.reshape(1, n))


def _rms_mod(x, g, sc, sh):
    ms = jnp.mean(x * x, axis=-1, keepdims=True)
    y = x * lax.rsqrt(ms + NORM_EPS) * g
    return y * (1.0 + sc) + sh


def _resident_cols(width, block):
    return pl.BlockSpec((D_MODEL, width), lambda *_: (0, block), pipeline_mode=pl.Buffered(1))


def _seq_batch_copies(hbm, vmem, sems, step, slot, ts, to_vmem):
    out = []
    for b in range(BATCH):
        h = hbm.at[b, pl.ds(step * ts, ts), :]
        v = vmem.at[slot, :, b, :]
        out.append(pltpu.make_async_copy(h, v, sems.at[slot, b]) if to_vmem
                   else pltpu.make_async_copy(v, h, sems.at[slot, b]))
    return out


def _inproj_sb_kernel(x_hbm, g_ref, sc_ref, sh_ref, wxy_ref, wga_ref, wgb_ref, o_ref, x3_ref,
                      x_buf, x_sem, h_scr):
    i = pl.program_id(0)
    slot = i % 2

    @pl.when(i == 0)
    def _():
        for cp in _seq_batch_copies(x_hbm, x_buf, x_sem, 0, 0, IN_TS, True):
            cp.start()

    @pl.when(i + 1 < pl.num_programs(0))
    def _():
        for cp in _seq_batch_copies(x_hbm, x_buf, x_sem, i + 1, 1 - slot, IN_TS, True):
            cp.start()

    for cp in _seq_batch_copies(x_hbm, x_buf, x_sem, i, slot, IN_TS, True):
        cp.wait()
    x = x_buf[slot]
    x3_ref[...] = x
    h = _rms_mod(x, g_ref[...], sc_ref[...], sh_ref[...])
    h_scr[...] = h.reshape(IN_TS * BATCH, D_MODEL).astype(BF16)
    for j, (w_ref, c0) in enumerate([(wxy_ref, 0), (wxy_ref, D_MODEL), (wga_ref, 0), (wgb_ref, 0)]):
        o_ref[:, j * D_MODEL:(j + 1) * D_MODEL] = jnp.dot(
            h_scr[...], w_ref[:, c0:c0 + D_MODEL], preferred_element_type=F32).astype(BF16)


def _inproj_sb(x, g, sc, sh, w_bf):
    tm = IN_TS * BATCH
    ncol = 4 * D_MODEL
    return pl.pallas_call(
        _inproj_sb_kernel,
        grid=(SEQ // IN_TS,),
        in_specs=[pl.BlockSpec(memory_space=pl.ANY),
                  _resident((1, D_MODEL)),
                  _resident((BATCH, D_MODEL)),
                  _resident((BATCH, D_MODEL)),
                  _resident_cols(2 * D_MODEL, 0), _resident_cols(D_MODEL, 5),
                  _resident_cols(D_MODEL, 6)],
        out_specs=[pl.BlockSpec((tm, ncol), lambda i: (i, 0)),
                   pl.BlockSpec((IN_TS, BATCH, D_MODEL), lambda i: (i, 0, 0))],
        out_shape=[jax.ShapeDtypeStruct((TOKENS, ncol), BF16),
                   jax.ShapeDtypeStruct((SEQ, BATCH, D_MODEL), F32)],
        scratch_shapes=[pltpu.VMEM((2, IN_TS, BATCH, D_MODEL), F32),
                        pltpu.SemaphoreType.DMA((2, BATCH)),
                        pltpu.VMEM((tm, D_MODEL), BF16)],
        compiler_params=_params(("arbitrary",)),
        name="inproj_sb",
    )(x, g, sc, sh, w_bf, w_bf, w_bf)


def _inproj_bs_kernel(x_ref, g_ref, sc_ref, sh_ref, pos_ref, invf_ref, wq_ref, wk_ref, wv_ref,
                      o_ref, h_scr, tab_scr):
    h = _rms_mod(x_ref[0], g_ref[...], sc_ref[0], sh_ref[0])
    h_scr[...] = h.astype(BF16)
    ang = pos_ref[0] * invf_ref[...]
    c = jnp.cos(ang)
    s = jnp.sin(ang)
    lane = lax.broadcasted_iota(jnp.int32, ang.shape, 1) % HEAD_DIM
    tab_scr[0] = jnp.where(lane < ROPE_DIM, c, 1.0)
    tab_scr[1] = jnp.where(lane < ROPE_HALF, -s, 0.0)
    tab_scr[2] = jnp.where((lane >= ROPE_HALF) & (lane < ROPE_DIM), s, 0.0)

    def rope_tile(j, w_ref, scale):
        acc = jnp.dot(h_scr[...], w_ref[...], preferred_element_type=F32)
        ct, sa, sb = tab_scr[0] * scale, tab_scr[1] * scale, tab_scr[2] * scale
        for cidx in range(D_MODEL // LANES):
            lo = j * D_MODEL + cidx * LANES
            t = acc[:, cidx * LANES:(cidx + 1) * LANES]
            r = (t * ct + pltpu.roll(t, LANES - ROPE_HALF, 1) * sa
                 + pltpu.roll(t, ROPE_HALF, 1) * sb)
            o_ref[0, :, lo:lo + LANES] = r.astype(BF16)

    rope_tile(0, wq_ref, Q_SCALE)
    rope_tile(1, wk_ref, 1.0)
    o_ref[0, :, 2 * D_MODEL:] = jnp.dot(h_scr[...], wv_ref[...],
                                        preferred_element_type=F32).astype(BF16)


def _inproj_bs(x, g, sc, sh, posf, invf, w_bf):
    ncol = 3 * D_MODEL
    return pl.pallas_call(
        _inproj_bs_kernel,
        grid=(BATCH, SEQ // QKV_TM),
        in_specs=[pl.BlockSpec((1, QKV_TM, D_MODEL), lambda b, i: (b, i, 0)),
                  _resident((1, D_MODEL)),
                  pl.BlockSpec((1, 1, D_MODEL), lambda b, i: (b, 0, 0)),
                  pl.BlockSpec((1, 1, D_MODEL), lambda b, i: (b, 0, 0)),
                  pl.BlockSpec((1, QKV_TM, LANES), lambda b, i: (b, i, 0)),
                  _resident((1, LANES)),
                  _resident_cols(D_MODEL, 2), _resident_cols(D_MODEL, 3),
                  _resident_cols(D_MODEL, 4)],
        out_specs=pl.BlockSpec((1, QKV_TM, ncol), lambda b, i: (b, i, 0)),
        out_shape=jax.ShapeDtypeStruct((BATCH, SEQ, ncol), BF16),
        scratch_shapes=[pltpu.VMEM((QKV_TM, D_MODEL), BF16),
                        pltpu.VMEM((3, QKV_TM, LANES), F32)],
        compiler_params=_params(("parallel", "parallel")),
        name="inproj_bs",
    )(x, g, sc, sh, posf, invf, w_bf, w_bf, w_bf)


def _rglru_kernel(xp_ref, xm_ref, xn_ref, cw_ref, cb_ref, wg_ref, ba_ref, bi_ref, lam_ref,
                  o_ref, a_scr, u_scr, h_scr):
    d = pl.program_id(1)
    t = pl.program_id(2)
    nt = pl.num_programs(2)
    te = t + d * (nt - 1 - 2 * t)
    rows = RG_TT * BATCH

    @pl.when(t == 0)
    def _():
        h_scr[...] = jnp.zeros_like(h_scr)

    def chunk(reverse):
        pm = (te > 0).astype(F32)
        nm = (te < nt - 1).astype(F32)
        xin = jnp.concatenate([xp_ref[...].astype(F32) * pm,
                               xm_ref[...].astype(F32),
                               xn_ref[...].astype(F32) * nm], axis=0)
        cw = cw_ref[...]
        xc = cb_ref[...]
        for k in range(RNN_CONV_W):
            xc = xc + xin[k:k + RG_TT] * cw[k:k + 1]
        x2 = xc.reshape(rows, RG_TC)

        g = jnp.dot(x2.astype(BF16), wg_ref[0, 0], preferred_element_type=F32)
        two_r = 1.0 + jnp.tanh(g[:, :RG_TC] + ba_ref[0])
        two_i = 1.0 + jnp.tanh(g[:, RG_TC:] + bi_ref[0])
        z = -lam_ref[0]
        sp = jnp.maximum(z, 0.0) + jnp.log1p(jnp.exp(-jnp.abs(z)))
        c_ln = (-0.5 * RG_C) * sp
        a = jnp.exp2(two_r * (c_ln * LOG2_E))
        th = jnp.tanh(two_r * c_ln)
        y = -2.0 * th / (1.0 - th)
        u = jnp.where(y > 0.0, y * lax.rsqrt(y), 0.0) * ((0.5 * x2) * two_i)
        a_scr[...] = a.reshape(RG_TT, BATCH, RG_TC)
        u_scr[...] = u.reshape(RG_TT, BATCH, RG_TC)
        h = h_scr[...]
        for s in range(RG_TT):
            idx = RG_TT - 1 - s if reverse else s
            h = a_scr[idx] * h + u_scr[idx]
            o_ref[0, idx] = h.astype(BF16)
        h_scr[...] = h

    @pl.when(d == 0)
    def _():
        chunk(False)

    @pl.when(d == 1)
    def _():
        chunk(True)


def _rglru(proj3, cw, cb, wg, ba, bi, lam):
    nt = SEQ // RG_TT

    def te(d, t):
        return t + d * (nt - 1 - 2 * t)

    return pl.pallas_call(
        _rglru_kernel,
        grid=(D_RNN // RG_TC, 2, nt),
        in_specs=[
            pl.BlockSpec((2, BATCH, RG_TC),
                         lambda c, d, t: (jnp.maximum(te(d, t) * (RG_TT // 2) - 1, 0), 0, c)),
            pl.BlockSpec((RG_TT, BATCH, RG_TC), lambda c, d, t: (te(d, t), 0, c)),
            pl.BlockSpec((1, BATCH, RG_TC),
                         lambda c, d, t: (jnp.minimum((te(d, t) + 1) * RG_TT, SEQ - 1), 0, c)),
            pl.BlockSpec((RNN_CONV_W, RG_TC), lambda c, d, t: (0, c)),
            pl.BlockSpec((1, RG_TC), lambda c, d, t: (0, c)),
            pl.BlockSpec((1, 1, RG_TC, 2 * RG_TC), lambda c, d, t: (d, c, 0, 0)),
            pl.BlockSpec((1, 1, RG_TC), lambda c, d, t: (d, 0, c)),
            pl.BlockSpec((1, 1, RG_TC), lambda c, d, t: (d, 0, c)),
            pl.BlockSpec((1, 1, RG_TC), lambda c, d, t: (d, 0, c)),
        ],
        out_specs=pl.BlockSpec((1, RG_TT, BATCH, RG_TC), lambda c, d, t: (d, te(d, t), 0, c)),
        out_shape=jax.ShapeDtypeStruct((2, SEQ, BATCH, D_RNN), BF16),
        scratch_shapes=[pltpu.VMEM((RG_TT, BATCH, RG_TC), F32),
                        pltpu.VMEM((RG_TT, BATCH, RG_TC), F32),
                        pltpu.VMEM((BATCH, RG_TC), F32)],
        compiler_params=_params(("parallel", "arbitrary", "arbitrary")),
        name="rglru",
    )(proj3, proj3, proj3, cw, cb, wg, ba, bi, lam)


def _zero_after(x):
    bits = lax.bitcast_convert_type(x, jnp.uint32)
    half = jnp.uint32(16)
    return lax.bitcast_convert_type(
        lax.shift_right_logical(lax.shift_right_logical(bits, half), half), F32)


def _attn_kernel(lamv_ref, gt_ref, q_ref, k_ref, v_ref, o_ref, vt_scr, s_scr, m_scr, p_scr):
    g = pl.program_id(0)
    nkb = SEQ // AT_KB
    nq = 2 * AT_TQ
    dn = (((1,), (1,)), ((), ()))

    @pl.when(g == 0)
    def _():
        s_scr[...] = jnp.zeros_like(s_scr)
        m_scr[...] = jnp.zeros_like(m_scr)
        p_scr[...] = jnp.ones_like(p_scr)

    vt_scr[0:V_DIM] = v_ref[0].astype(F32).T.astype(BF16)
    vt_scr[V_DIM:AT_VROWS] = jnp.ones((AT_VROWS - V_DIM, SEQ), BF16)

    lv = lamv_ref[...]
    lam = (jnp.exp(jnp.sum(lv[0:1] * lv[1:2], axis=-1, keepdims=True))
           - jnp.exp(jnp.sum(lv[2:3] * lv[3:4], axis=-1, keepdims=True)) + LAM_INIT)

    for h in range(AT_NT):
        s_new, m_new = s_scr.at[h % 2], m_scr.at[h % 2]
        s_old, m_old = s_scr.at[(h - 1) % 2], m_scr.at[(h - 1) % 2]
        p_new, p_old = p_scr.at[(h - 1) % AT_NT], p_scr.at[h]
        q = q_ref[0, h * AT_TQ:(h + 1) * AT_TQ, :]
        lane = lax.broadcasted_iota(jnp.int32, q.shape, 1)
        zero = jnp.zeros_like(q)
        qcat = jnp.concatenate([jnp.where(lane < HEAD_DIM, q, zero),
                                jnp.where(lane >= HEAD_DIM, q, zero)], axis=0)
        m_prev = m_old[...]
        m8 = jnp.full((8, nq), -jnp.inf, F32)
        for kb in range(nkb):
            rows = slice(kb * AT_KB, (kb + 1) * AT_KB)
            s = lax.dot_general(k_ref[0, rows, :], qcat, dn, preferred_element_type=F32)
            s_new[rows, :] = s
            for r in range(AT_KB // 8):
                m8 = jnp.maximum(m8, s[r * 8:(r + 1) * 8, :])
            m_tied = m_prev + _zero_after(s[AT_KB - 8:AT_KB, :])
            e = jnp.exp2(s_old[rows, :].reshape(AT_KB // 8, 8, nq) - m_tied[None])
            p_new[rows, :] = e.reshape(AT_KB, nq).astype(BF16)
        m_new[...] = jnp.max(m8, axis=0, keepdims=True)
        acc = jnp.dot(vt_scr[...], p_old[...], preferred_element_type=F32)
        o1 = acc[0:V_DIM, :AT_TQ] / acc[V_DIM:V_DIM + 1, :AT_TQ]
        o2 = acc[0:V_DIM, AT_TQ:] / acc[V_DIM:V_DIM + 1, AT_TQ:]
        o = o1 - lam * o2
        ms = jnp.mean(o * o, axis=0, keepdims=True)
        y = o * lax.rsqrt(ms + NORM_EPS) * gt_ref[...]
        o_ref[0, h * AT_TQ:(h + 1) * AT_TQ, :] = (y * (1.0 - LAM_INIT)).T.astype(BF16)


def _attn(qkv, lamv, subln_gt):
    n_heads = BATCH * N_HEADS

    def head_index(hd, col0):
        return hd // N_HEADS, 0, col0 + hd % N_HEADS

    front = lambda g: jnp.minimum(g, n_heads - 1)
    back = lambda g: jnp.maximum(g - 1, 0)
    return pl.pallas_call(
        _attn_kernel,
        grid=(n_heads + 1,),
        in_specs=[
            pl.BlockSpec((4, HEAD_DIM), lambda g: (0, 0)),
            pl.BlockSpec((V_DIM, 1), lambda g: (0, 0)),
            pl.BlockSpec((1, SEQ, LANES), lambda g: head_index(front(g), 0)),
            pl.BlockSpec((1, SEQ, LANES), lambda g: head_index(front(g), N_HEADS)),
            pl.BlockSpec((1, SEQ, LANES), lambda g: head_index(back(g), 2 * N_HEADS)),
        ],
        out_specs=pl.BlockSpec((1, SEQ, LANES), lambda g: head_index(back(g), 0)),
        out_shape=jax.ShapeDtypeStruct((BATCH, SEQ, N_HEADS * V_DIM), BF16),
        scratch_shapes=[pltpu.VMEM((AT_VROWS, SEQ), BF16),
                        pltpu.VMEM((2, SEQ, 2 * AT_TQ), F32),
                        pltpu.VMEM((2, 1, 2 * AT_TQ), F32),
                        pltpu.VMEM((AT_NT, SEQ, 2 * AT_TQ), BF16)],
        compiler_params=_params(("arbitrary",)),
        name="diffattn",
    )(lamv, subln_gt, qkv, qkv, qkv)


def _merge_kernel(hf_ref, hb_ref, yr_ref, ga_ref, gb_ref, at_ref, perm_ref, x_ref, g1_ref,
                  wr_ref, wa_ref, wo_ref, o_ref):
    rows = MG_TS * BATCH
    hr = hf_ref[0].astype(F32) + hb_ref[0].astype(F32)
    ya = (hr * jax.nn.gelu(yr_ref[...].astype(F32))).reshape(rows, D_RNN).astype(BF16)
    br_a = jnp.dot(ya, wr_ref[...], preferred_element_type=F32)
    at_sb = jnp.dot(perm_ref[...], at_ref[...].reshape(rows, D_MODEL),
                    preferred_element_type=F32).astype(BF16)
    br_b = jnp.dot(at_sb, wa_ref[...], preferred_element_type=F32)
    ga = jax.nn.sigmoid(ga_ref[...].reshape(rows, D_MODEL).astype(F32))
    gb = jax.nn.sigmoid(gb_ref[...].reshape(rows, D_MODEL).astype(F32))
    merged = (ga * br_a + gb * br_b).astype(BF16)
    m = jnp.dot(merged, wo_ref[...], preferred_element_type=F32)
    o_ref[...] = x_ref[...] + g1_ref[...] * m.reshape(MG_TS, BATCH, D_MODEL)


def _merge(hfb, proj3, attn, x3, g1, wr, wa, wo):
    tok = lambda cidx: pl.BlockSpec((MG_TS, BATCH, D_MODEL), lambda i: (i, 0, cidx))
    wspec = _resident((D_MODEL, D_MODEL))
    rows = MG_TS * BATCH
    r_out = np.arange(rows)
    perm = np.zeros((rows, rows), np.float32)
    perm[r_out, (r_out % BATCH) * MG_TS + r_out // BATCH] = 1.0
    return pl.pallas_call(
        _merge_kernel,
        grid=(SEQ // MG_TS,),
        in_specs=[pl.BlockSpec((1, MG_TS, BATCH, D_RNN), lambda i: (0, i, 0, 0)),
                  pl.BlockSpec((1, MG_TS, BATCH, D_RNN), lambda i: (1, i, 0, 0)),
                  tok(1), tok(2), tok(3),
                  pl.BlockSpec((BATCH, MG_TS, D_MODEL), lambda i: (0, i, 0)),
                  _resident((rows, rows)),
                  tok(0),
                  _resident((BATCH, D_MODEL)),
                  wspec, wspec, wspec],
        out_specs=tok(0),
        out_shape=jax.ShapeDtypeStruct((SEQ, BATCH, D_MODEL), F32),
        compiler_params=_params(("parallel",)),
        name="merge",
    )(hfb, hfb, proj3, proj3, proj3, attn, jnp.asarray(perm, BF16), x3, g1, wr, wa, wo)


def _ffn_kernel(xp_ref, xm_ref, xn_ref, g_ref, sc_ref, sh_ref, g2_ref, fg_ref,
                wu_ref, cw_ref, cb_ref, wd_ref, o_hbm, h_scr, act_scr, o_buf, o_sem):
    i = pl.program_id(0)
    n = pl.num_programs(0)
    rows = FF_TS * BATCH
    slot = i % 2

    @pl.when(i >= 2)
    def _():
        for cp in _seq_batch_copies(o_hbm, o_buf, o_sem, i - 2, slot, FF_TS, False):
            cp.wait()
    pm = (i > 0).astype(F32)
    nm = (i < n - 1).astype(F32)
    g, sc, sh = g_ref[...], sc_ref[...], sh_ref[...]
    h_scr[0:1] = (_rms_mod(xp_ref[...], g, sc, sh) * pm).astype(BF16)
    h_scr[1:FF_TS + 1] = _rms_mod(xm_ref[...], g, sc, sh).astype(BF16)
    h_scr[FF_TS + 1:FF_TS + 2] = (_rms_mod(xn_ref[...], g, sc, sh) * nm).astype(BF16)

    def conv(up, lo):
        cw = cw_ref[:, lo:lo + FF_CH]
        cv = cb_ref[:, lo:lo + FF_CH]
        for k in range(3):
            cv = cv + up[k:k + FF_TS] * cw[k:k + 1]
        return cv.reshape(rows, FF_CH)

    def up_act(ci):
        hx = h_scr[...].reshape((FF_TS + 2) * BATCH, D_MODEL)
        lo_v, lo_g = ci * FF_CH, D_FF + ci * FF_CH
        val = conv(jnp.dot(hx, wu_ref[:, lo_v:lo_v + FF_CH], preferred_element_type=F32
                           ).reshape(FF_TS + 2, BATCH, FF_CH), lo_v)
        gt = conv(jnp.dot(hx, wu_ref[:, lo_g:lo_g + FF_CH], preferred_element_type=F32
                          ).reshape(FF_TS + 2, BATCH, FF_CH), lo_g)
        act_scr[:, lo_v:lo_v + FF_CH] = (gt * jax.nn.sigmoid(gt) * val).astype(BF16)

    for ci in range(FF_NCH):
        up_act(ci)
    down = jnp.dot(act_scr[...], wd_ref[...], preferred_element_type=F32)
    x2 = xm_ref[...] + g2_ref[...] * down.reshape(FF_TS, BATCH, D_MODEL)
    ms = jnp.mean(x2 * x2, axis=-1, keepdims=True)
    o_buf[slot] = x2 * lax.rsqrt(ms + NORM_EPS) * fg_ref[...]
    for cp in _seq_batch_copies(o_hbm, o_buf, o_sem, i, slot, FF_TS, False):
        cp.start()

    @pl.when(i == n - 1)
    def _():
        for cp in _seq_batch_copies(o_hbm, o_buf, o_sem, i - 1, 1 - slot, FF_TS, False):
            cp.wait()
        for cp in _seq_batch_copies(o_hbm, o_buf, o_sem, i, slot, FF_TS, False):
            cp.wait()


def _ffn(x1, g, sc, sh, g2, fg, wu, cw, cb, wd):
    assert SEQ // FF_TS >= 2
    return pl.pallas_call(
        _ffn_kernel,
        grid=(SEQ // FF_TS,),
        in_specs=[
            pl.BlockSpec((1, BATCH, D_MODEL), lambda i: (jnp.maximum(i * FF_TS - 1, 0), 0, 0)),
            pl.BlockSpec((FF_TS, BATCH, D_MODEL), lambda i: (i, 0, 0)),
            pl.BlockSpec((1, BATCH, D_MODEL),
                         lambda i: (jnp.minimum((i + 1) * FF_TS, SEQ - 1), 0, 0)),
            _resident((1, D_MODEL)), _resident((BATCH, D_MODEL)), _resident((BATCH, D_MODEL)),
            _resident((BATCH, D_MODEL)), _resident((1, D_MODEL)),
            _resident((D_MODEL, 2 * D_FF)),
            _resident((3, 2 * D_FF)),
            _resident((1, 2 * D_FF)),
            _resident((D_FF, D_MODEL)),
        ],
        out_specs=pl.BlockSpec(memory_space=pl.ANY),
        out_shape=jax.ShapeDtypeStruct((BATCH, SEQ, D_MODEL), F32),
        scratch_shapes=[pltpu.VMEM((FF_TS + 2, BATCH, D_MODEL), BF16),
                        pltpu.VMEM((FF_TS * BATCH, D_FF), BF16),
                        pltpu.VMEM((2, FF_TS, BATCH, D_MODEL), F32),
                        pltpu.SemaphoreType.DMA((2, BATCH))],
        compiler_params=_params(("arbitrary",)),
        name="ffn",
    )(x1, x1, x1, g, sc, sh, g2, fg, wu, cw, cb, wd)


def _gate_tiles(w_a, w_i):
    per = RG_TC // RNN_BLOCK
    nt = D_RNN // RG_TC
    both = jnp.stack([w_a.reshape(2, nt, per, RNN_BLOCK, RNN_BLOCK),
                      w_i.reshape(2, nt, per, RNN_BLOCK, RNN_BLOCK)], axis=4)
    eye = jnp.eye(per, dtype=w_a.dtype)
    full = jnp.einsum('dcjkgn,jm->dcjkgmn', both, eye)
    return full.reshape(2, nt, RG_TC, 2 * RG_TC)


def kernel(x, c, positions, w_ada, b_ada, norm1_g, w_in, conv_rnn_w, conv_rnn_b, w_rg_a, b_rg_a,
           w_rg_i, b_rg_i, rg_lambda, w_rnn_o, lam_q1, lam_k1, lam_q2, lam_k2, subln_g, w_attn_o,
           w_out, norm2_g, w_up, conv_ffn_w, conv_ffn_b, w_down, final_g):
    l = 0
    posf = jnp.broadcast_to(positions.astype(F32)[:, :, None], (BATCH, SEQ, LANES))
    inv_freq = ROPE_THETA ** (-jnp.arange(0, ROPE_DIM, 2, dtype=F32) / ROPE_DIM)
    invf = jnp.tile(inv_freq, LANES // ROPE_HALF).reshape(1, LANES)

    mod = _ada(c, w_ada[l], b_ada[l])
    sh1, sc1, g1, sh2, sc2, g2 = [mod[:, m * D_MODEL:(m + 1) * D_MODEL] for m in range(N_MOD)]

    w_in_bf = w_in[l].astype(BF16)
    g_n1 = norm1_g[l].reshape(1, D_MODEL)
    proj, x3 = _inproj_sb(x, g_n1, sc1, sh1, w_in_bf)
    proj3 = proj.reshape(SEQ, BATCH, 4 * D_MODEL)
    qkv = _inproj_bs(x, g_n1, sc1.reshape(BATCH, 1, D_MODEL), sh1.reshape(BATCH, 1, D_MODEL),
                     posf, invf, w_in_bf)

    wg = (0.5 * _gate_tiles(w_rg_a[l], w_rg_i[l])).astype(BF16)
    hfb = _rglru(proj3, conv_rnn_w[l], conv_rnn_b[l].reshape(1, D_RNN), wg,
                 0.5 * b_rg_a[l].reshape(2, 1, D_RNN), 0.5 * b_rg_i[l].reshape(2, 1, D_RNN),
                 rg_lambda[l].reshape(2, 1, D_RNN))

    lamv = jnp.stack([lam_q1[l], lam_k1[l], lam_q2[l], lam_k2[l]]).astype(F32)
    attn = _attn(qkv, lamv, subln_g[l].reshape(V_DIM, 1))

    x1 = _merge(hfb, proj3, attn, x3, g1, w_rnn_o[l].astype(BF16), w_attn_o[l].astype(BF16),
                w_out[l].astype(BF16))

    return _ffn(x1, norm2_g[l].reshape(1, D_MODEL), sc2, sh2, g2, final_g.reshape(1, D_MODEL),
                w_up[l].astype(BF16), conv_ffn_w[l], conv_ffn_b[l].reshape(1, 2 * D_FF),
                w_down[l].astype(BF16))
```

```python
import math

import jax
import jax.numpy as jnp
import numpy as np
from jax import lax
from jax.experimental import pallas as pl
from jax.experimental.pallas import tpu as pltpu

F32 = jnp.float32
BF16 = jnp.bfloat16

D_MODEL = 1024
BATCH = 16
SEQ = 2048
TOKENS = BATCH * SEQ
D_RNN = D_MODEL
N_RNN_BLOCKS = 16
RNN_BLOCK = D_RNN // N_RNN_BLOCKS
RNN_CONV_W = 4
RNN_CONV_LEFT = 2
RG_C = 8.0
N_HEADS = 8
HEAD_DIM = 64
V_DIM = 2 * HEAD_DIM
ROPE_DIM = HEAD_DIM // 4
ROPE_HALF = ROPE_DIM // 2
ROPE_THETA = 500000.0
D_FF = 2816
N_MOD = 6
NORM_EPS = 1e-6
LAM_INIT = 0.8 - 0.6 * math.exp(-0.3 * 0)
LOG2_E = math.log2(math.e)
Q_SCALE = HEAD_DIM ** -0.5 * LOG2_E

LANES = 128
VMEM_LIMIT = 52 * 1024 * 1024

ADA_TN = 1024
IN_TS = 64
QKV_TM = 1024
RG_TT = 128
RG_TC = 256
AT_TQ = 512
AT_KB = 256
AT_VROWS = V_DIM + 16
AT_NT = SEQ // AT_TQ
MG_TS = 32
FF_TS = 64
FF_CH = 256
FF_NCH = D_FF // FF_CH


def _params(sem):
    return pltpu.CompilerParams(dimension_semantics=sem, vmem_limit_bytes=VMEM_LIMIT)


def _resident(shape):
    return pl.BlockSpec(shape, lambda *_: (0,) * len(shape), pipeline_mode=pl.Buffered(1))


def _ada_kernel(c_ref, w_ref, b_ref, o_ref):
    c = c_ref[...]
    ca = c * jax.nn.sigmoid(c)
    o_ref[...] = jnp.dot(ca, w_ref[...], preferred_element_type=F32,
                         precision=lax.Precision.HIGHEST) + b_ref[...]


def _ada(c, w, b):
    n = w.shape[1]
    return pl.pallas_call(
        _ada_kernel,
        grid=(n // ADA_TN,),
        in_specs=[pl.BlockSpec((BATCH, D_MODEL), lambda j: (0, 0)),
                  pl.BlockSpec((D_MODEL, ADA_TN), lambda j: (0, j)),
                  pl.BlockSpec((1, ADA_TN), lambda j: (0, j))],
        out_specs=pl.BlockSpec((BATCH, ADA_TN), lambda j: (0, j)),
        out_shape=jax.ShapeDtypeStruct((BATCH, n), F32),
        compiler_params=_params(("arbitrary",)),
        name="adaln",
    )(c, w, b.reshape(1, n))


def _rms_mod(x, g, sc, sh):
    ms = jnp.mean(x * x, axis=-1, keepdims=True)
    y = x * lax.rsqrt(ms + NORM_EPS) * g
    return y * (1.0 + sc) + sh


def _resident_cols(width, block):
    return pl.BlockSpec((D_MODEL, width), lambda *_: (0, block), pipeline_mode=pl.Buffered(1))


def _seq_batch_copies(hbm, vmem, sems, step, slot, ts, to_vmem):
    out = []
    for b in range(BATCH):
        h = hbm.at[b, pl.ds(step * ts, ts), :]
        v = vmem.at[slot, :, b, :]
        out.append(pltpu.make_async_copy(h, v, sems.at[slot, b]) if to_vmem
                   else pltpu.make_async_copy(v, h, sems.at[slot, b]))
    return out


def _inproj_sb_kernel(x_hbm, g_ref, sc_ref, sh_ref, wxy_ref, wga_ref, wgb_ref, o_ref, x3_ref,
                      x_buf, x_sem, h_scr):
    i = pl.program_id(0)
    slot = i % 2

    @pl.when(i == 0)
    def _():
        for cp in _seq_batch_copies(x_hbm, x_buf, x_sem, 0, 0, IN_TS, True):
            cp.start()

    @pl.when(i + 1 < pl.num_programs(0))
    def _():
        for cp in _seq_batch_copies(x_hbm, x_buf, x_sem, i + 1, 1 - slot, IN_TS, True):
            cp.start()

    for cp in _seq_batch_copies(x_hbm, x_buf, x_sem, i, slot, IN_TS, True):
        cp.wait()
    x = x_buf[slot]
    x3_ref[...] = x
    h = _rms_mod(x, g_ref[...], sc_ref[...], sh_ref[...])
    h_scr[...] = h.reshape(IN_TS * BATCH, D_MODEL).astype(BF16)
    for j, (w_ref, c0) in enumerate([(wxy_ref, 0), (wxy_ref, D_MODEL), (wga_ref, 0), (wgb_ref, 0)]):
        o_ref[:, j * D_MODEL:(j + 1) * D_MODEL] = jnp.dot(
            h_scr[...], w_ref[:, c0:c0 + D_MODEL], preferred_element_type=F32).astype(BF16)


def _inproj_sb(x, g, sc, sh, w_bf):
    tm = IN_TS * BATCH
    ncol = 4 * D_MODEL
    return pl.pallas_call(
        _inproj_sb_kernel,
        grid=(SEQ // IN_TS,),
        in_specs=[pl.BlockSpec(memory_space=pl.ANY),
                  _resident((1, D_MODEL)),
                  _resident((BATCH, D_MODEL)),
                  _resident((BATCH, D_MODEL)),
                  _resident_cols(2 * D_MODEL, 0), _resident_cols(D_MODEL, 5),
                  _resident_cols(D_MODEL, 6)],
        out_specs=[pl.BlockSpec((tm, ncol), lambda i: (i, 0)),
                   pl.BlockSpec((IN_TS, BATCH, D_MODEL), lambda i: (i, 0, 0))],
        out_shape=[jax.ShapeDtypeStruct((TOKENS, ncol), BF16),
                   jax.ShapeDtypeStruct((SEQ, BATCH, D_MODEL), F32)],
        scratch_shapes=[pltpu.VMEM((2, IN_TS, BATCH, D_MODEL), F32),
                        pltpu.SemaphoreType.DMA((2, BATCH)),
                        pltpu.VMEM((tm, D_MODEL), BF16)],
        compiler_params=_params(("arbitrary",)),
        name="inproj_sb",
    )(x, g, sc, sh, w_bf, w_bf, w_bf)


def _rope_kernel(pos_ref, invf_ref, cos_ref, sin_ref):
    pos = pos_ref[...]
    for f in range(ROPE_HALF):
        ang = pos * invf_ref[f]
        cos_ref[f] = jnp.cos(ang)
        sin_ref[f] = jnp.sin(ang)


def _rope_tables(posf, inv_freq):
    full = lambda shape: pl.BlockSpec(shape, lambda i: (0,) * len(shape))
    out = jax.ShapeDtypeStruct((ROPE_HALF, BATCH, SEQ), F32)
    return pl.pallas_call(
        _rope_kernel,
        grid=(1,),
        in_specs=[full((BATCH, SEQ)), full((ROPE_HALF, 1, 1))],
        out_specs=[full((ROPE_HALF, BATCH, SEQ)), full((ROPE_HALF, BATCH, SEQ))],
        out_shape=[out, out],
        compiler_params=_params(("arbitrary",)),
        name="rope_tables",
    )(posf, inv_freq.reshape(ROPE_HALF, 1, 1))


def _inproj_bs_kernel(x_ref, g_ref, sc_ref, sh_ref, cos_ref, sin_ref, wq_ref, wk_ref, wv_ref,
                      o_ref, h_scr, tab_scr):
    h = _rms_mod(x_ref[0], g_ref[...], sc_ref[0], sh_ref[0])
    h_scr[...] = h.astype(BF16)
    c = cos_ref[0]
    s = sin_ref[0]
    lane = lax.broadcasted_iota(jnp.int32, c.shape, 1) % HEAD_DIM
    tab_scr[0] = jnp.where(lane < ROPE_DIM, c, 1.0)
    tab_scr[1] = jnp.where(lane < ROPE_HALF, -s, 0.0)
    tab_scr[2] = jnp.where((lane >= ROPE_HALF) & (lane < ROPE_DIM), s, 0.0)

    def rope_tile(j, w_ref, scale):
        acc = jnp.dot(h_scr[...], w_ref[...], preferred_element_type=F32)
        ct, sa, sb = tab_scr[0] * scale, tab_scr[1] * scale, tab_scr[2] * scale
        for cidx in range(D_MODEL // LANES):
            lo = j * D_MODEL + cidx * LANES
            t = acc[:, cidx * LANES:(cidx + 1) * LANES]
            r = (t * ct + pltpu.roll(t, LANES - ROPE_HALF, 1) * sa
                 + pltpu.roll(t, ROPE_HALF, 1) * sb)
            o_ref[0, :, lo:lo + LANES] = r.astype(BF16)

    rope_tile(0, wq_ref, Q_SCALE)
    rope_tile(1, wk_ref, 1.0)
    o_ref[0, :, 2 * D_MODEL:] = jnp.dot(h_scr[...], wv_ref[...],
                                        preferred_element_type=F32).astype(BF16)


def _inproj_bs(x, g, sc, sh, cos_t, sin_t, w_bf):
    ncol = 3 * D_MODEL
    return pl.pallas_call(
        _inproj_bs_kernel,
        grid=(BATCH, SEQ // QKV_TM),
        in_specs=[pl.BlockSpec((1, QKV_TM, D_MODEL), lambda b, i: (b, i, 0)),
                  _resident((1, D_MODEL)),
                  pl.BlockSpec((1, 1, D_MODEL), lambda b, i: (b, 0, 0)),
                  pl.BlockSpec((1, 1, D_MODEL), lambda b, i: (b, 0, 0)),
                  pl.BlockSpec((1, QKV_TM, LANES), lambda b, i: (b, i, 0)),
                  pl.BlockSpec((1, QKV_TM, LANES), lambda b, i: (b, i, 0)),
                  _resident_cols(D_MODEL, 2), _resident_cols(D_MODEL, 3),
                  _resident_cols(D_MODEL, 4)],
        out_specs=pl.BlockSpec((1, QKV_TM, ncol), lambda b, i: (b, i, 0)),
        out_shape=jax.ShapeDtypeStruct((BATCH, SEQ, ncol), BF16),
        scratch_shapes=[pltpu.VMEM((QKV_TM, D_MODEL), BF16),
                        pltpu.VMEM((3, QKV_TM, LANES), F32)],
        compiler_params=_params(("parallel", "parallel")),
        name="inproj_bs",
    )(x, g, sc, sh, cos_t, sin_t, w_bf, w_bf, w_bf)


def _rglru_kernel(xp_ref, xm_ref, xn_ref, cw_ref, cb_ref, wg_ref, ba_ref, bi_ref, lam_ref,
                  o_ref, a_scr, u_scr, h_scr):
    d = pl.program_id(1)
    t = pl.program_id(2)
    nt = pl.num_programs(2)
    te = t + d * (nt - 1 - 2 * t)
    rows = RG_TT * BATCH

    @pl.when(t == 0)
    def _():
        h_scr[...] = jnp.zeros_like(h_scr)

    def chunk(reverse):
        pm = (te > 0).astype(F32)
        nm = (te < nt - 1).astype(F32)
        xin = jnp.concatenate([xp_ref[...].astype(F32) * pm,
                               xm_ref[...].astype(F32),
                               xn_ref[...].astype(F32) * nm], axis=0)
        cw = cw_ref[...]
        xc = cb_ref[...]
        for k in range(RNN_CONV_W):
            xc = xc + xin[k:k + RG_TT] * cw[k:k + 1]
        x2 = xc.reshape(rows, RG_TC)

        g = jnp.dot(x2.astype(BF16), wg_ref[0, 0], preferred_element_type=F32)
        two_r = 1.0 + jnp.tanh(g[:, :RG_TC] + ba_ref[0])
        two_i = 1.0 + jnp.tanh(g[:, RG_TC:] + bi_ref[0])
        z = -lam_ref[0]
        sp = jnp.maximum(z, 0.0) + jnp.log1p(jnp.exp(-jnp.abs(z)))
        c_ln = (-0.5 * RG_C) * sp
        a = jnp.exp2(two_r * (c_ln * LOG2_E))
        th = jnp.tanh(two_r * c_ln)
        y = -2.0 * th / (1.0 - th)
        u = jnp.where(y > 0.0, y * lax.rsqrt(y), 0.0) * ((0.5 * x2) * two_i)
        a_scr[...] = a.reshape(RG_TT, BATCH, RG_TC)
        u_scr[...] = u.reshape(RG_TT, BATCH, RG_TC)
        h = h_scr[...]
        for s in range(RG_TT):
            idx = RG_TT - 1 - s if reverse else s
            h = a_scr[idx] * h + u_scr[idx]
            o_ref[0, idx] = h.astype(BF16)
        h_scr[...] = h

    @pl.when(d == 0)
    def _():
        chunk(False)

    @pl.when(d == 1)
    def _():
        chunk(True)


def _rglru(proj3, cw, cb, wg, ba, bi, lam):
    nt = SEQ // RG_TT

    def te(d, t):
        return t + d * (nt - 1 - 2 * t)

    return pl.pallas_call(
        _rglru_kernel,
        grid=(D_RNN // RG_TC, 2, nt),
        in_specs=[
            pl.BlockSpec((2, BATCH, RG_TC),
                         lambda c, d, t: (jnp.maximum(te(d, t) * (RG_TT // 2) - 1, 0), 0, c)),
            pl.BlockSpec((RG_TT, BATCH, RG_TC), lambda c, d, t: (te(d, t), 0, c)),
            pl.BlockSpec((1, BATCH, RG_TC),
                         lambda c, d, t: (jnp.minimum((te(d, t) + 1) * RG_TT, SEQ - 1), 0, c)),
            pl.BlockSpec((RNN_CONV_W, RG_TC), lambda c, d, t: (0, c)),
            pl.BlockSpec((1, RG_TC), lambda c, d, t: (0, c)),
            pl.BlockSpec((1, 1, RG_TC, 2 * RG_TC), lambda c, d, t: (d, c, 0, 0)),
            pl.BlockSpec((1, 1, RG_TC), lambda c, d, t: (d, 0, c)),
            pl.BlockSpec((1, 1, RG_TC), lambda c, d, t: (d, 0, c)),
            pl.BlockSpec((1, 1, RG_TC), lambda c, d, t: (d, 0, c)),
        ],
        out_specs=pl.BlockSpec((1, RG_TT, BATCH, RG_TC), lambda c, d, t: (d, te(d, t), 0, c)),
        out_shape=jax.ShapeDtypeStruct((2, SEQ, BATCH, D_RNN), BF16),
        scratch_shapes=[pltpu.VMEM((RG_TT, BATCH, RG_TC), F32),
                        pltpu.VMEM((RG_TT, BATCH, RG_TC), F32),
                        pltpu.VMEM((BATCH, RG_TC), F32)],
        compiler_params=_params(("parallel", "arbitrary", "arbitrary")),
        name="rglru",
    )(proj3, proj3, proj3, cw, cb, wg, ba, bi, lam)


def _zero_after(x):
    bits = lax.bitcast_convert_type(x, jnp.uint32)
    half = jnp.uint32(16)
    return lax.bitcast_convert_type(
        lax.shift_right_logical(lax.shift_right_logical(bits, half), half), F32)


def _attn_kernel(lamv_ref, gt_ref, q_ref, k_ref, v_ref, o_ref, vt_scr, s_scr, m_scr, p_scr):
    g = pl.program_id(0)
    nkb = SEQ // AT_KB
    nq = 2 * AT_TQ
    dn = (((1,), (1,)), ((), ()))

    @pl.when(g == 0)
    def _():
        s_scr[...] = jnp.zeros_like(s_scr)
        m_scr[...] = jnp.zeros_like(m_scr)
        p_scr[...] = jnp.ones_like(p_scr)

    vt_scr[0:V_DIM] = v_ref[0].astype(F32).T.astype(BF16)
    vt_scr[V_DIM:AT_VROWS] = jnp.ones((AT_VROWS - V_DIM, SEQ), BF16)

    lv = lamv_ref[...]
    lam = (jnp.exp(jnp.sum(lv[0:1] * lv[1:2], axis=-1, keepdims=True))
           - jnp.exp(jnp.sum(lv[2:3] * lv[3:4], axis=-1, keepdims=True)) + LAM_INIT)

    for h in range(AT_NT):
        s_new, m_new = s_scr.at[h % 2], m_scr.at[h % 2]
        s_old, m_old = s_scr.at[(h - 1) % 2], m_scr.at[(h - 1) % 2]
        p_new, p_old = p_scr.at[(h - 1) % AT_NT], p_scr.at[h]
        q = q_ref[0, h * AT_TQ:(h + 1) * AT_TQ, :]
        lane = lax.broadcasted_iota(jnp.int32, q.shape, 1)
        zero = jnp.zeros_like(q)
        qcat = jnp.concatenate([jnp.where(lane < HEAD_DIM, q, zero),
                                jnp.where(lane >= HEAD_DIM, q, zero)], axis=0)
        m_prev = m_old[...]
        m8 = jnp.full((8, nq), -jnp.inf, F32)
        for kb in range(nkb):
            rows = slice(kb * AT_KB, (kb + 1) * AT_KB)
            s = lax.dot_general(k_ref[0, rows, :], qcat, dn, preferred_element_type=F32)
            s_new[rows, :] = s
            for r in range(AT_KB // 8):
                m8 = jnp.maximum(m8, s[r * 8:(r + 1) * 8, :])
            m_tied = m_prev + _zero_after(s[AT_KB - 8:AT_KB, :])
            e = jnp.exp2(s_old[rows, :].reshape(AT_KB // 8, 8, nq) - m_tied[None])
            p_new[rows, :] = e.reshape(AT_KB, nq).astype(BF16)
        m_new[...] = jnp.max(m8, axis=0, keepdims=True)
        acc = jnp.dot(vt_scr[...], p_old[...], preferred_element_type=F32)
        o1 = acc[0:V_DIM, :AT_TQ] / acc[V_DIM:V_DIM + 1, :AT_TQ]
        o2 = acc[0:V_DIM, AT_TQ:] / acc[V_DIM:V_DIM + 1, AT_TQ:]
        o = o1 - lam * o2
        ms = jnp.mean(o * o, axis=0, keepdims=True)
        y = o * lax.rsqrt(ms + NORM_EPS) * gt_ref[...]
        o_ref[0, h * AT_TQ:(h + 1) * AT_TQ, :] = (y * (1.0 - LAM_INIT)).T.astype(BF16)


def _attn(qkv, lamv, subln_gt):
    n_heads = BATCH * N_HEADS

    def head_index(hd, col0):
        return hd // N_HEADS, 0, col0 + hd % N_HEADS

    front = lambda g: jnp.minimum(g, n_heads - 1)
    back = lambda g: jnp.maximum(g - 1, 0)
    return pl.pallas_call(
        _attn_kernel,
        grid=(n_heads + 1,),
        in_specs=[
            pl.BlockSpec((4, HEAD_DIM), lambda g: (0, 0)),
            pl.BlockSpec((V_DIM, 1), lambda g: (0, 0)),
            pl.BlockSpec((1, SEQ, LANES), lambda g: head_index(front(g), 0)),
            pl.BlockSpec((1, SEQ, LANES), lambda g: head_index(front(g), N_HEADS)),
            pl.BlockSpec((1, SEQ, LANES), lambda g: head_index(back(g), 2 * N_HEADS)),
        ],
        out_specs=pl.BlockSpec((1, SEQ, LANES), lambda g: head_index(back(g), 0)),
        out_shape=jax.ShapeDtypeStruct((BATCH, SEQ, N_HEADS * V_DIM), BF16),
        scratch_shapes=[pltpu.VMEM((AT_VROWS, SEQ), BF16),
                        pltpu.VMEM((2, SEQ, 2 * AT_TQ), F32),
                        pltpu.VMEM((2, 1, 2 * AT_TQ), F32),
                        pltpu.VMEM((AT_NT, SEQ, 2 * AT_TQ), BF16)],
        compiler_params=_params(("arbitrary",)),
        name="diffattn",
    )(lamv, subln_gt, qkv, qkv, qkv)


def _merge_kernel(hf_ref, hb_ref, yr_ref, ga_ref, gb_ref, at_ref, perm_ref, x_ref, g1_ref,
                  wr_ref, wa_ref, wo_ref, o_ref):
    rows = MG_TS * BATCH
    hr = hf_ref[0].astype(F32) + hb_ref[0].astype(F32)
    ya = (hr * jax.nn.gelu(yr_ref[...].astype(F32))).reshape(rows, D_RNN).astype(BF16)
    br_a = jnp.dot(ya, wr_ref[...], preferred_element_type=F32)
    at_sb = jnp.dot(perm_ref[...], at_ref[...].reshape(rows, D_MODEL),
                    preferred_element_type=F32).astype(BF16)
    br_b = jnp.dot(at_sb, wa_ref[...], preferred_element_type=F32)
    ga = jax.nn.sigmoid(ga_ref[...].reshape(rows, D_MODEL).astype(F32))
    gb = jax.nn.sigmoid(gb_ref[...].reshape(rows, D_MODEL).astype(F32))
    merged = (ga * br_a + gb * br_b).astype(BF16)
    m = jnp.dot(merged, wo_ref[...], preferred_element_type=F32)
    o_ref[...] = x_ref[...] + g1_ref[...] * m.reshape(MG_TS, BATCH, D_MODEL)


def _merge(hfb, proj3, attn, x3, g1, wr, wa, wo):
    tok = lambda cidx: pl.BlockSpec((MG_TS, BATCH, D_MODEL), lambda i: (i, 0, cidx))
    wspec = _resident((D_MODEL, D_MODEL))
    rows = MG_TS * BATCH
    r_out = np.arange(rows)
    perm = np.zeros((rows, rows), np.float32)
    perm[r_out, (r_out % BATCH) * MG_TS + r_out // BATCH] = 1.0
    return pl.pallas_call(
        _merge_kernel,
        grid=(SEQ // MG_TS,),
        in_specs=[pl.BlockSpec((1, MG_TS, BATCH, D_RNN), lambda i: (0, i, 0, 0)),
                  pl.BlockSpec((1, MG_TS, BATCH, D_RNN), lambda i: (1, i, 0, 0)),
                  tok(1), tok(2), tok(3),
                  pl.BlockSpec((BATCH, MG_TS, D_MODEL), lambda i: (0, i, 0)),
                  _resident((rows, rows)),
                  tok(0),
                  _resident((BATCH, D_MODEL)),
                  wspec, wspec, wspec],
        out_specs=tok(0),
        out_shape=jax.ShapeDtypeStruct((SEQ, BATCH, D_MODEL), F32),
        compiler_params=_params(("parallel",)),
        name="merge",
    )(hfb, hfb, proj3, proj3, proj3, attn, jnp.asarray(perm, BF16), x3, g1, wr, wa, wo)


def _ffn_kernel(xp_ref, xm_ref, xn_ref, g_ref, sc_ref, sh_ref, g2_ref, fg_ref,
                wu_ref, cw_ref, cb_ref, wd_ref, o_hbm, h_scr, act_scr, o_buf, o_sem):
    i = pl.program_id(0)
    n = pl.num_programs(0)
    rows = FF_TS * BATCH
    slot = i % 2

    @pl.when(i >= 2)
    def _():
        for cp in _seq_batch_copies(o_hbm, o_buf, o_sem, i - 2, slot, FF_TS, False):
            cp.wait()
    pm = (i > 0).astype(F32)
    nm = (i < n - 1).astype(F32)
    g, sc, sh = g_ref[...], sc_ref[...], sh_ref[...]
    h_scr[0:1] = (_rms_mod(xp_ref[...], g, sc, sh) * pm).astype(BF16)
    h_scr[1:FF_TS + 1] = _rms_mod(xm_ref[...], g, sc, sh).astype(BF16)
    h_scr[FF_TS + 1:FF_TS + 2] = (_rms_mod(xn_ref[...], g, sc, sh) * nm).astype(BF16)

    def conv(up, lo):
        cw = cw_ref[:, lo:lo + FF_CH]
        cv = cb_ref[:, lo:lo + FF_CH]
        for k in range(3):
            cv = cv + up[k:k + FF_TS] * cw[k:k + 1]
        return cv.reshape(rows, FF_CH)

    def up_act(ci):
        hx = h_scr[...].reshape((FF_TS + 2) * BATCH, D_MODEL)
        lo_v, lo_g = ci * FF_CH, D_FF + ci * FF_CH
        val = conv(jnp.dot(hx, wu_ref[:, lo_v:lo_v + FF_CH], preferred_element_type=F32
                           ).reshape(FF_TS + 2, BATCH, FF_CH), lo_v)
        gt = conv(jnp.dot(hx, wu_ref[:, lo_g:lo_g + FF_CH], preferred_element_type=F32
                          ).reshape(FF_TS + 2, BATCH, FF_CH), lo_g)
        act_scr[:, lo_v:lo_v + FF_CH] = (gt * jax.nn.sigmoid(gt) * val).astype(BF16)

    for ci in range(FF_NCH):
        up_act(ci)
    down = jnp.dot(act_scr[...], wd_ref[...], preferred_element_type=F32)
    x2 = xm_ref[...] + g2_ref[...] * down.reshape(FF_TS, BATCH, D_MODEL)
    ms = jnp.mean(x2 * x2, axis=-1, keepdims=True)
    o_buf[slot] = x2 * lax.rsqrt(ms + NORM_EPS) * fg_ref[...]
    for cp in _seq_batch_copies(o_hbm, o_buf, o_sem, i, slot, FF_TS, False):
        cp.start()

    @pl.when(i == n - 1)
    def _():
        for cp in _seq_batch_copies(o_hbm, o_buf, o_sem, i - 1, 1 - slot, FF_TS, False):
            cp.wait()
        for cp in _seq_batch_copies(o_hbm, o_buf, o_sem, i, slot, FF_TS, False):
            cp.wait()


def _ffn(x1, g, sc, sh, g2, fg, wu, cw, cb, wd):
    assert SEQ // FF_TS >= 2
    return pl.pallas_call(
        _ffn_kernel,
        grid=(SEQ // FF_TS,),
        in_specs=[
            pl.BlockSpec((1, BATCH, D_MODEL), lambda i: (jnp.maximum(i * FF_TS - 1, 0), 0, 0)),
            pl.BlockSpec((FF_TS, BATCH, D_MODEL), lambda i: (i, 0, 0)),
            pl.BlockSpec((1, BATCH, D_MODEL),
                         lambda i: (jnp.minimum((i + 1) * FF_TS, SEQ - 1), 0, 0)),
            _resident((1, D_MODEL)), _resident((BATCH, D_MODEL)), _resident((BATCH, D_MODEL)),
            _resident((BATCH, D_MODEL)), _resident((1, D_MODEL)),
            _resident((D_MODEL, 2 * D_FF)),
            _resident((3, 2 * D_FF)),
            _resident((1, 2 * D_FF)),
            _resident((D_FF, D_MODEL)),
        ],
        out_specs=pl.BlockSpec(memory_space=pl.ANY),
        out_shape=jax.ShapeDtypeStruct((BATCH, SEQ, D_MODEL), F32),
        scratch_shapes=[pltpu.VMEM((FF_TS + 2, BATCH, D_MODEL), BF16),
                        pltpu.VMEM((FF_TS * BATCH, D_FF), BF16),
                        pltpu.VMEM((2, FF_TS, BATCH, D_MODEL), F32),
                        pltpu.SemaphoreType.DMA((2, BATCH))],
        compiler_params=_params(("arbitrary",)),
        name="ffn",
    )(x1, x1, x1, g, sc, sh, g2, fg, wu, cw, cb, wd)


def _gate_tiles(w_a, w_i):
    per = RG_TC // RNN_BLOCK
    nt = D_RNN // RG_TC
    both = jnp.stack([w_a.reshape(2, nt, per, RNN_BLOCK, RNN_BLOCK),
                      w_i.reshape(2, nt, per, RNN_BLOCK, RNN_BLOCK)], axis=4)
    eye = jnp.eye(per, dtype=w_a.dtype)
    full = jnp.einsum('dcjkgn,jm->dcjkgmn', both, eye)
    return full.reshape(2, nt, RG_TC, 2 * RG_TC)


def kernel(x, c, positions, w_ada, b_ada, norm1_g, w_in, conv_rnn_w, conv_rnn_b, w_rg_a, b_rg_a,
           w_rg_i, b_rg_i, rg_lambda, w_rnn_o, lam_q1, lam_k1, lam_q2, lam_k2, subln_g, w_attn_o,
           w_out, norm2_g, w_up, conv_ffn_w, conv_ffn_b, w_down, final_g):
    l = 0
    inv_freq = ROPE_THETA ** (-jnp.arange(0, ROPE_DIM, 2, dtype=F32) / ROPE_DIM)
    cos8, sin8 = _rope_tables(positions.astype(F32), inv_freq)
    cos_t = jnp.tile(cos8.transpose(1, 2, 0), (1, 1, LANES // ROPE_HALF))
    sin_t = jnp.tile(sin8.transpose(1, 2, 0), (1, 1, LANES // ROPE_HALF))

    mod = _ada(c, w_ada[l], b_ada[l])
    sh1, sc1, g1, sh2, sc2, g2 = [mod[:, m * D_MODEL:(m + 1) * D_MODEL] for m in range(N_MOD)]

    w_in_bf = w_in[l].astype(BF16)
    g_n1 = norm1_g[l].reshape(1, D_MODEL)
    proj, x3 = _inproj_sb(x, g_n1, sc1, sh1, w_in_bf)
    proj3 = proj.reshape(SEQ, BATCH, 4 * D_MODEL)
    qkv = _inproj_bs(x, g_n1, sc1.reshape(BATCH, 1, D_MODEL), sh1.reshape(BATCH, 1, D_MODEL),
                     cos_t, sin_t, w_in_bf)

    wg = (0.5 * _gate_tiles(w_rg_a[l], w_rg_i[l])).astype(BF16)
    hfb = _rglru(proj3, conv_rnn_w[l], conv_rnn_b[l].reshape(1, D_RNN), wg,
                 0.5 * b_rg_a[l].reshape(2, 1, D_RNN), 0.5 * b_rg_i[l].reshape(2, 1, D_RNN),
                 rg_lambda[l].reshape(2, 1, D_RNN))

    lamv = jnp.stack([lam_q1[l], lam_k1[l], lam_q2[l], lam_k2[l]]).astype(F32)
    attn = _attn(qkv, lamv, subln_g[l].reshape(V_DIM, 1))

    x1 = _merge(hfb, proj3, attn, x3, g1, w_rnn_o[l].astype(BF16), w_attn_o[l].astype(BF16),
                w_out[l].astype(BF16))

    return _ffn(x1, norm2_g[l].reshape(1, D_MODEL), sc2, sh2, g2, final_g.reshape(1, D_MODEL),
                w_up[l].astype(BF16), conv_ffn_w[l], conv_ffn_b[l].reshape(1, 2 * D_FF),
                w_down[l].astype(BF16))
```

```python
import math

import jax
import jax.numpy as jnp
import numpy as np
from jax import lax
from jax.experimental import pallas as pl
from jax.experimental.pallas import tpu as pltpu

F32 = jnp.float32
BF16 = jnp.bfloat16

D_MODEL = 1024
BATCH = 16
SEQ = 2048
TOKENS = BATCH * SEQ
D_RNN = D_MODEL
N_RNN_BLOCKS = 16
RNN_BLOCK = D_RNN // N_RNN_BLOCKS
RNN_CONV_W = 4
RNN_CONV_LEFT = 2
RG_C = 8.0
N_HEADS = 8
HEAD_DIM = 64
V_DIM = 2 * HEAD_DIM
ROPE_DIM = HEAD_DIM // 4
ROPE_HALF = ROPE_DIM // 2
ROPE_THETA = 500000.0
D_FF = 2816
FFN_CONV_W = 3
N_MOD = 6
NORM_EPS = 1e-6
LAM_INIT = 0.8 - 0.6 * math.exp(-0.3 * 0)
LOG2_E = math.log2(math.e)
Q_SCALE = HEAD_DIM ** -0.5 * LOG2_E

LANES = 128
VMEM_LIMIT = 52 * 1024 * 1024

ADA_TN = 1024
IN_TS = 64
QKV_TM = 1024
RG_TT = 128
RG_TC = 256
AT_TQ = 512
AT_KB = 256
AT_VROWS = V_DIM + 16
AT_NT = SEQ // AT_TQ
MG_TS = 32
FF_TS = 64
FF_CH = 256
FF_NCH = D_FF // FF_CH


def _params(sem):
    return pltpu.CompilerParams(dimension_semantics=sem, vmem_limit_bytes=VMEM_LIMIT)


def _resident(shape):
    return pl.BlockSpec(shape, lambda *_: (0,) * len(shape), pipeline_mode=pl.Buffered(1))


def _ada_kernel(c_ref, w_ref, b_ref, o_ref):
    c = c_ref[...]
    ca = c * jax.nn.sigmoid(c)
    o_ref[...] = jnp.dot(ca, w_ref[...], preferred_element_type=F32,
                         precision=lax.Precision.HIGHEST) + b_ref[...]


def _ada(c, w, b):
    n = w.shape[1]
    return pl.pallas_call(
        _ada_kernel,
        grid=(n // ADA_TN,),
        in_specs=[pl.BlockSpec((BATCH, D_MODEL), lambda j: (0, 0)),
                  pl.BlockSpec((D_MODEL, ADA_TN), lambda j: (0, j)),
                  pl.BlockSpec((1, ADA_TN), lambda j: (0, j))],
        out_specs=pl.BlockSpec((BATCH, ADA_TN), lambda j: (0, j)),
        out_shape=jax.ShapeDtypeStruct((BATCH, n), F32),
        compiler_params=_params(("arbitrary",)),
        name="adaln",
    )(c, w, b.reshape(1, n))


def _rms_mod(x, g, sc, sh):
    ms = jnp.mean(x * x, axis=-1, keepdims=True)
    y = x * lax.rsqrt(ms + NORM_EPS) * g
    return y * (1.0 + sc) + sh


def _resident_cols(width, block):
    return pl.BlockSpec((D_MODEL, width), lambda *_: (0, block), pipeline_mode=pl.Buffered(1))


def _seq_batch_copies(hbm, vmem, sems, step, slot, ts, to_vmem):
    out = []
    for b in range(BATCH):
        h = hbm.at[b, pl.ds(step * ts, ts), :]
        v = vmem.at[slot, :, b, :]
        out.append(pltpu.make_async_copy(h, v, sems.at[slot, b]) if to_vmem
                   else pltpu.make_async_copy(v, h, sems.at[slot, b]))
    return out


def _inproj_sb_kernel(x_hbm, g_ref, sc_ref, sh_ref, wxy_ref, wga_ref, wgb_ref, o_ref, x3_ref,
                      x_buf, x_sem, h_scr):
    i = pl.program_id(0)
    slot = i % 2

    @pl.when(i == 0)
    def _():
        for cp in _seq_batch_copies(x_hbm, x_buf, x_sem, 0, 0, IN_TS, True):
            cp.start()

    @pl.when(i + 1 < pl.num_programs(0))
    def _():
        for cp in _seq_batch_copies(x_hbm, x_buf, x_sem, i + 1, 1 - slot, IN_TS, True):
            cp.start()

    for cp in _seq_batch_copies(x_hbm, x_buf, x_sem, i, slot, IN_TS, True):
        cp.wait()
    x = x_buf[slot]
    x3_ref[...] = x
    h = _rms_mod(x, g_ref[...], sc_ref[...], sh_ref[...])
    h_scr[...] = h.reshape(IN_TS * BATCH, D_MODEL).astype(BF16)
    for j, (w_ref, c0) in enumerate([(wxy_ref, 0), (wxy_ref, D_MODEL), (wga_ref, 0), (wgb_ref, 0)]):
        o_ref[:, j * D_MODEL:(j + 1) * D_MODEL] = jnp.dot(
            h_scr[...], w_ref[:, c0:c0 + D_MODEL], preferred_element_type=F32).astype(BF16)


def _inproj_sb(x, g, sc, sh, w_bf):
    tm = IN_TS * BATCH
    ncol = 4 * D_MODEL
    return pl.pallas_call(
        _inproj_sb_kernel,
        grid=(SEQ // IN_TS,),
        in_specs=[pl.BlockSpec(memory_space=pl.ANY),
                  _resident((1, D_MODEL)),
                  _resident((BATCH, D_MODEL)),
                  _resident((BATCH, D_MODEL)),
                  _resident_cols(2 * D_MODEL, 0), _resident_cols(D_MODEL, 5),
                  _resident_cols(D_MODEL, 6)],
        out_specs=[pl.BlockSpec((tm, ncol), lambda i: (i, 0)),
                   pl.BlockSpec((IN_TS, BATCH, D_MODEL), lambda i: (i, 0, 0))],
        out_shape=[jax.ShapeDtypeStruct((TOKENS, ncol), BF16),
                   jax.ShapeDtypeStruct((SEQ, BATCH, D_MODEL), F32)],
        scratch_shapes=[pltpu.VMEM((2, IN_TS, BATCH, D_MODEL), F32),
                        pltpu.SemaphoreType.DMA((2, BATCH)),
                        pltpu.VMEM((tm, D_MODEL), BF16)],
        compiler_params=_params(("arbitrary",)),
        name="inproj_sb",
    )(x, g, sc, sh, w_bf, w_bf, w_bf)


def _rope_kernel(pos_ref, invf_ref, cos_ref, sin_ref):
    pos = pos_ref[...]
    for f in range(ROPE_HALF):
        ang = pos * invf_ref[f]
        cos_ref[f] = jnp.cos(ang)
        sin_ref[f] = jnp.sin(ang)


def _rope_tables(posf, inv_freq):
    full = lambda shape: pl.BlockSpec(shape, lambda i: (0,) * len(shape))
    out = jax.ShapeDtypeStruct((ROPE_HALF, BATCH, SEQ), F32)
    return pl.pallas_call(
        _rope_kernel,
        grid=(1,),
        in_specs=[full((BATCH, SEQ)), full((ROPE_HALF, 1, 1))],
        out_specs=[full((ROPE_HALF, BATCH, SEQ)), full((ROPE_HALF, BATCH, SEQ))],
        out_shape=[out, out],
        compiler_params=_params(("arbitrary",)),
        name="rope_tables",
    )(posf, inv_freq.reshape(ROPE_HALF, 1, 1))


def _inproj_bs_kernel(x_ref, g_ref, sc_ref, sh_ref, cos_ref, sin_ref, wq_ref, wk_ref, wv_ref,
                      o_ref, h_scr, tab_scr):
    h = _rms_mod(x_ref[0], g_ref[...], sc_ref[0], sh_ref[0])
    h_scr[...] = h.astype(BF16)
    c = cos_ref[0]
    s = sin_ref[0]
    lane = lax.broadcasted_iota(jnp.int32, c.shape, 1) % HEAD_DIM
    tab_scr[0] = jnp.where(lane < ROPE_DIM, c, 1.0)
    tab_scr[1] = jnp.where(lane < ROPE_HALF, -s, 0.0)
    tab_scr[2] = jnp.where((lane >= ROPE_HALF) & (lane < ROPE_DIM), s, 0.0)

    def rope_tile(j, w_ref, scale):
        acc = jnp.dot(h_scr[...], w_ref[...], preferred_element_type=F32)
        ct, sa, sb = tab_scr[0] * scale, tab_scr[1] * scale, tab_scr[2] * scale
        for cidx in range(D_MODEL // LANES):
            lo = j * D_MODEL + cidx * LANES
            t = acc[:, cidx * LANES:(cidx + 1) * LANES]
            r = (t * ct + pltpu.roll(t, LANES - ROPE_HALF, 1) * sa
                 + pltpu.roll(t, ROPE_HALF, 1) * sb)
            o_ref[0, :, lo:lo + LANES] = r.astype(BF16)

    rope_tile(0, wq_ref, Q_SCALE)
    rope_tile(1, wk_ref, 1.0)
    o_ref[0, :, 2 * D_MODEL:] = jnp.dot(h_scr[...], wv_ref[...],
                                        preferred_element_type=F32).astype(BF16)


def _inproj_bs(x, g, sc, sh, cos_t, sin_t, w_bf):
    ncol = 3 * D_MODEL
    return pl.pallas_call(
        _inproj_bs_kernel,
        grid=(BATCH, SEQ // QKV_TM),
        in_specs=[pl.BlockSpec((1, QKV_TM, D_MODEL), lambda b, i: (b, i, 0)),
                  _resident((1, D_MODEL)),
                  pl.BlockSpec((1, 1, D_MODEL), lambda b, i: (b, 0, 0)),
                  pl.BlockSpec((1, 1, D_MODEL), lambda b, i: (b, 0, 0)),
                  pl.BlockSpec((1, QKV_TM, LANES), lambda b, i: (b, i, 0)),
                  pl.BlockSpec((1, QKV_TM, LANES), lambda b, i: (b, i, 0)),
                  _resident_cols(D_MODEL, 2), _resident_cols(D_MODEL, 3),
                  _resident_cols(D_MODEL, 4)],
        out_specs=pl.BlockSpec((1, QKV_TM, ncol), lambda b, i: (b, i, 0)),
        out_shape=jax.ShapeDtypeStruct((BATCH, SEQ, ncol), BF16),
        scratch_shapes=[pltpu.VMEM((QKV_TM, D_MODEL), BF16),
                        pltpu.VMEM((3, QKV_TM, LANES), F32)],
        compiler_params=_params(("parallel", "parallel")),
        name="inproj_bs",
    )(x, g, sc, sh, cos_t, sin_t, w_bf, w_bf, w_bf)


def _rglru_kernel(xp_ref, xm_ref, xn_ref, cw_ref, cb_ref, wg_ref, ba_ref, bi_ref, lam_ref,
                  o_ref, a_scr, u_scr, h_scr):
    d = pl.program_id(1)
    t = pl.program_id(2)
    nt = pl.num_programs(2)
    te = t + d * (nt - 1 - 2 * t)
    rows = RG_TT * BATCH

    @pl.when(t == 0)
    def _():
        h_scr[...] = jnp.zeros_like(h_scr)

    def chunk(reverse):
        pm = (te > 0).astype(F32)
        nm = (te < nt - 1).astype(F32)
        xin = jnp.concatenate([xp_ref[...].astype(F32) * pm,
                               xm_ref[...].astype(F32),
                               xn_ref[...].astype(F32) * nm], axis=0)
        cw = cw_ref[...]
        xc = cb_ref[...]
        for k in range(RNN_CONV_W):
            xc = xc + xin[k:k + RG_TT] * cw[k:k + 1]
        x2 = xc.reshape(rows, RG_TC)

        g = jnp.dot(x2.astype(BF16), wg_ref[0, 0], preferred_element_type=F32)
        two_r = 1.0 + jnp.tanh(g[:, :RG_TC] + ba_ref[0])
        two_i = 1.0 + jnp.tanh(g[:, RG_TC:] + bi_ref[0])
        z = -lam_ref[0]
        sp = jnp.maximum(z, 0.0) + jnp.log1p(jnp.exp(-jnp.abs(z)))
        c_ln = (-0.5 * RG_C) * sp
        a = jnp.exp2(two_r * (c_ln * LOG2_E))
        th = jnp.tanh(two_r * c_ln)
        y = -2.0 * th / (1.0 - th)
        u = jnp.where(y > 0.0, y * lax.rsqrt(y), 0.0) * ((0.5 * x2) * two_i)
        a_scr[...] = a.reshape(RG_TT, BATCH, RG_TC)
        u_scr[...] = u.reshape(RG_TT, BATCH, RG_TC)
        h = h_scr[...]
        for s in range(RG_TT):
            idx = RG_TT - 1 - s if reverse else s
            h = a_scr[idx] * h + u_scr[idx]
            o_ref[0, idx] = h.astype(BF16)
        h_scr[...] = h

    @pl.when(d == 0)
    def _():
        chunk(False)

    @pl.when(d == 1)
    def _():
        chunk(True)


def _rglru(proj3, cw, cb, wg, ba, bi, lam):
    nt = SEQ // RG_TT

    def te(d, t):
        return t + d * (nt - 1 - 2 * t)

    return pl.pallas_call(
        _rglru_kernel,
        grid=(D_RNN // RG_TC, 2, nt),
        in_specs=[
            pl.BlockSpec((RNN_CONV_LEFT, BATCH, RG_TC),
                         lambda c, d, t: (jnp.maximum(te(d, t) * (RG_TT // RNN_CONV_LEFT) - 1, 0),
                                          0, c)),
            pl.BlockSpec((RG_TT, BATCH, RG_TC), lambda c, d, t: (te(d, t), 0, c)),
            pl.BlockSpec((RNN_CONV_W - 1 - RNN_CONV_LEFT, BATCH, RG_TC),
                         lambda c, d, t: (jnp.minimum((te(d, t) + 1) * RG_TT, SEQ - 1), 0, c)),
            pl.BlockSpec((RNN_CONV_W, RG_TC), lambda c, d, t: (0, c)),
            pl.BlockSpec((1, RG_TC), lambda c, d, t: (0, c)),
            pl.BlockSpec((1, 1, RG_TC, 2 * RG_TC), lambda c, d, t: (d, c, 0, 0)),
            pl.BlockSpec((1, 1, RG_TC), lambda c, d, t: (d, 0, c)),
            pl.BlockSpec((1, 1, RG_TC), lambda c, d, t: (d, 0, c)),
            pl.BlockSpec((1, 1, RG_TC), lambda c, d, t: (d, 0, c)),
        ],
        out_specs=pl.BlockSpec((1, RG_TT, BATCH, RG_TC), lambda c, d, t: (d, te(d, t), 0, c)),
        out_shape=jax.ShapeDtypeStruct((2, SEQ, BATCH, D_RNN), BF16),
        scratch_shapes=[pltpu.VMEM((RG_TT, BATCH, RG_TC), F32),
                        pltpu.VMEM((RG_TT, BATCH, RG_TC), F32),
                        pltpu.VMEM((BATCH, RG_TC), F32)],
        compiler_params=_params(("parallel", "arbitrary", "arbitrary")),
        name="rglru",
    )(proj3, proj3, proj3, cw, cb, wg, ba, bi, lam)


def _zero_after(x):
    bits = lax.bitcast_convert_type(x, jnp.uint32)
    half = jnp.uint32(16)
    return lax.bitcast_convert_type(
        lax.shift_right_logical(lax.shift_right_logical(bits, half), half), F32)


def _attn_kernel(lamv_ref, gt_ref, q_ref, k_ref, v_ref, o_ref, vt_scr, s_scr, m_scr, p_scr):
    g = pl.program_id(0)
    nkb = SEQ // AT_KB
    nq = 2 * AT_TQ
    dn = (((1,), (1,)), ((), ()))

    @pl.when(g == 0)
    def _():
        s_scr[...] = jnp.zeros_like(s_scr)
        m_scr[...] = jnp.zeros_like(m_scr)
        p_scr[...] = jnp.ones_like(p_scr)

    vt_scr[0:V_DIM] = v_ref[0].astype(F32).T.astype(BF16)
    vt_scr[V_DIM:AT_VROWS] = jnp.ones((AT_VROWS - V_DIM, SEQ), BF16)

    lv = lamv_ref[...]
    lam = (jnp.exp(jnp.sum(lv[0:1] * lv[1:2], axis=-1, keepdims=True))
           - jnp.exp(jnp.sum(lv[2:3] * lv[3:4], axis=-1, keepdims=True)) + LAM_INIT)

    for h in range(AT_NT):
        s_new, m_new = s_scr.at[h % 2], m_scr.at[h % 2]
        s_old, m_old = s_scr.at[(h - 1) % 2], m_scr.at[(h - 1) % 2]
        p_new, p_old = p_scr.at[(h - 1) % AT_NT], p_scr.at[h]
        q = q_ref[0, h * AT_TQ:(h + 1) * AT_TQ, :]
        lane = lax.broadcasted_iota(jnp.int32, q.shape, 1)
        zero = jnp.zeros_like(q)
        qcat = jnp.concatenate([jnp.where(lane < HEAD_DIM, q, zero),
                                jnp.where(lane >= HEAD_DIM, q, zero)], axis=0)
        m_prev = m_old[...]
        m8 = jnp.full((8, nq), -jnp.inf, F32)
        for kb in range(nkb):
            rows = slice(kb * AT_KB, (kb + 1) * AT_KB)
            s = lax.dot_general(k_ref[0, rows, :], qcat, dn, preferred_element_type=F32)
            s_new[rows, :] = s
            for r in range(AT_KB // 8):
                m8 = jnp.maximum(m8, s[r * 8:(r + 1) * 8, :])
            m_tied = m_prev + _zero_after(s[AT_KB - 8:AT_KB, :])
            e = jnp.exp2(s_old[rows, :].reshape(AT_KB // 8, 8, nq) - m_tied[None])
            p_new[rows, :] = e.reshape(AT_KB, nq).astype(BF16)
        m_new[...] = jnp.max(m8, axis=0, keepdims=True)
        acc = jnp.dot(vt_scr[...], p_old[...], preferred_element_type=F32)
        o1 = acc[0:V_DIM, :AT_TQ] / acc[V_DIM:V_DIM + 1, :AT_TQ]
        o2 = acc[0:V_DIM, AT_TQ:] / acc[V_DIM:V_DIM + 1, AT_TQ:]
        o = o1 - lam * o2
        ms = jnp.mean(o * o, axis=0, keepdims=True)
        y = o * lax.rsqrt(ms + NORM_EPS) * gt_ref[...]
        o_ref[0, h * AT_TQ:(h + 1) * AT_TQ, :] = (y * (1.0 - LAM_INIT)).T.astype(BF16)


def _attn(qkv, lamv, subln_gt):
    n_heads = BATCH * N_HEADS

    def head_index(hd, col0):
        return hd // N_HEADS, 0, col0 + hd % N_HEADS

    front = lambda g: jnp.minimum(g, n_heads - 1)
    back = lambda g: jnp.maximum(g - 1, 0)
    return pl.pallas_call(
        _attn_kernel,
        grid=(n_heads + 1,),
        in_specs=[
            pl.BlockSpec((4, HEAD_DIM), lambda g: (0, 0)),
            pl.BlockSpec((V_DIM, 1), lambda g: (0, 0)),
            pl.BlockSpec((1, SEQ, LANES), lambda g: head_index(front(g), 0)),
            pl.BlockSpec((1, SEQ, LANES), lambda g: head_index(front(g), N_HEADS)),
            pl.BlockSpec((1, SEQ, LANES), lambda g: head_index(back(g), 2 * N_HEADS)),
        ],
        out_specs=pl.BlockSpec((1, SEQ, LANES), lambda g: head_index(back(g), 0)),
        out_shape=jax.ShapeDtypeStruct((BATCH, SEQ, N_HEADS * V_DIM), BF16),
        scratch_shapes=[pltpu.VMEM((AT_VROWS, SEQ), BF16),
                        pltpu.VMEM((2, SEQ, 2 * AT_TQ), F32),
                        pltpu.VMEM((2, 1, 2 * AT_TQ), F32),
                        pltpu.VMEM((AT_NT, SEQ, 2 * AT_TQ), BF16)],
        compiler_params=_params(("arbitrary",)),
        name="diffattn",
    )(lamv, subln_gt, qkv, qkv, qkv)


def _merge_kernel(hf_ref, hb_ref, yr_ref, ga_ref, gb_ref, at_ref, perm_ref, x_ref, g1_ref,
                  wr_ref, wa_ref, wo_ref, o_ref):
    rows = MG_TS * BATCH
    hr = hf_ref[0].astype(F32) + hb_ref[0].astype(F32)
    ya = (hr * jax.nn.gelu(yr_ref[...].astype(F32))).reshape(rows, D_RNN).astype(BF16)
    br_a = jnp.dot(ya, wr_ref[...], preferred_element_type=F32)
    at_sb = jnp.dot(perm_ref[...], at_ref[...].reshape(rows, D_MODEL),
                    preferred_element_type=F32).astype(BF16)
    br_b = jnp.dot(at_sb, wa_ref[...], preferred_element_type=F32)
    ga = jax.nn.sigmoid(ga_ref[...].reshape(rows, D_MODEL).astype(F32))
    gb = jax.nn.sigmoid(gb_ref[...].reshape(rows, D_MODEL).astype(F32))
    merged = (ga * br_a + gb * br_b).astype(BF16)
    m = jnp.dot(merged, wo_ref[...], preferred_element_type=F32)
    o_ref[...] = x_ref[...] + g1_ref[...] * m.reshape(MG_TS, BATCH, D_MODEL)


def _merge(hfb, proj3, attn, x3, g1, wr, wa, wo):
    tok = lambda cidx: pl.BlockSpec((MG_TS, BATCH, D_MODEL), lambda i: (i, 0, cidx))
    wspec = _resident((D_MODEL, D_MODEL))
    rows = MG_TS * BATCH
    r_out = np.arange(rows)
    perm = np.zeros((rows, rows), np.float32)
    perm[r_out, (r_out % BATCH) * MG_TS + r_out // BATCH] = 1.0
    return pl.pallas_call(
        _merge_kernel,
        grid=(SEQ // MG_TS,),
        in_specs=[pl.BlockSpec((1, MG_TS, BATCH, D_RNN), lambda i: (0, i, 0, 0)),
                  pl.BlockSpec((1, MG_TS, BATCH, D_RNN), lambda i: (1, i, 0, 0)),
                  tok(1), tok(2), tok(3),
                  pl.BlockSpec((BATCH, MG_TS, D_MODEL), lambda i: (0, i, 0)),
                  _resident((rows, rows)),
                  tok(0),
                  _resident((BATCH, D_MODEL)),
                  wspec, wspec, wspec],
        out_specs=tok(0),
        out_shape=jax.ShapeDtypeStruct((SEQ, BATCH, D_MODEL), F32),
        compiler_params=_params(("parallel",)),
        name="merge",
    )(hfb, hfb, proj3, proj3, proj3, attn, jnp.asarray(perm, BF16), x3, g1, wr, wa, wo)


def _ffn_kernel(xp_ref, xm_ref, xn_ref, g_ref, sc_ref, sh_ref, g2_ref, fg_ref,
                wu_ref, cw_ref, cb_ref, wd_ref, o_hbm, h_scr, act_scr, o_buf, o_sem):
    i = pl.program_id(0)
    n = pl.num_programs(0)
    rows = FF_TS * BATCH
    slot = i % 2

    @pl.when(i >= 2)
    def _():
        for cp in _seq_batch_copies(o_hbm, o_buf, o_sem, i - 2, slot, FF_TS, False):
            cp.wait()
    pm = (i > 0).astype(F32)
    nm = (i < n - 1).astype(F32)
    g, sc, sh = g_ref[...], sc_ref[...], sh_ref[...]
    h_scr[0:1] = (_rms_mod(xp_ref[...], g, sc, sh) * pm).astype(BF16)
    h_scr[1:FF_TS + 1] = _rms_mod(xm_ref[...], g, sc, sh).astype(BF16)
    h_scr[FF_TS + 1:FF_TS + 2] = (_rms_mod(xn_ref[...], g, sc, sh) * nm).astype(BF16)

    def conv(up, lo):
        cw = cw_ref[:, lo:lo + FF_CH]
        cv = cb_ref[:, lo:lo + FF_CH]
        for k in range(FFN_CONV_W):
            cv = cv + up[k:k + FF_TS] * cw[k:k + 1]
        return cv.reshape(rows, FF_CH)

    def up_act(ci):
        hx = h_scr[...].reshape((FF_TS + 2) * BATCH, D_MODEL)
        lo_v, lo_g = ci * FF_CH, D_FF + ci * FF_CH
        val = conv(jnp.dot(hx, wu_ref[:, lo_v:lo_v + FF_CH], preferred_element_type=F32
                           ).reshape(FF_TS + 2, BATCH, FF_CH), lo_v)
        gt = conv(jnp.dot(hx, wu_ref[:, lo_g:lo_g + FF_CH], preferred_element_type=F32
                          ).reshape(FF_TS + 2, BATCH, FF_CH), lo_g)
        act_scr[:, lo_v:lo_v + FF_CH] = (gt * jax.nn.sigmoid(gt) * val).astype(BF16)

    for ci in range(FF_NCH):
        up_act(ci)
    down = jnp.dot(act_scr[...], wd_ref[...], preferred_element_type=F32)
    x2 = xm_ref[...] + g2_ref[...] * down.reshape(FF_TS, BATCH, D_MODEL)
    ms = jnp.mean(x2 * x2, axis=-1, keepdims=True)
    o_buf[slot] = x2 * lax.rsqrt(ms + NORM_EPS) * fg_ref[...]
    for cp in _seq_batch_copies(o_hbm, o_buf, o_sem, i, slot, FF_TS, False):
        cp.start()

    @pl.when(i == n - 1)
    def _():
        for cp in _seq_batch_copies(o_hbm, o_buf, o_sem, i - 1, 1 - slot, FF_TS, False):
            cp.wait()
        for cp in _seq_batch_copies(o_hbm, o_buf, o_sem, i, slot, FF_TS, False):
            cp.wait()


def _ffn(x1, g, sc, sh, g2, fg, wu, cw, cb, wd):
    assert SEQ // FF_TS >= 2
    return pl.pallas_call(
        _ffn_kernel,
        grid=(SEQ // FF_TS,),
        in_specs=[
            pl.BlockSpec((1, BATCH, D_MODEL), lambda i: (jnp.maximum(i * FF_TS - 1, 0), 0, 0)),
            pl.BlockSpec((FF_TS, BATCH, D_MODEL), lambda i: (i, 0, 0)),
            pl.BlockSpec((1, BATCH, D_MODEL),
                         lambda i: (jnp.minimum((i + 1) * FF_TS, SEQ - 1), 0, 0)),
            _resident((1, D_MODEL)), _resident((BATCH, D_MODEL)), _resident((BATCH, D_MODEL)),
            _resident((BATCH, D_MODEL)), _resident((1, D_MODEL)),
            _resident((D_MODEL, 2 * D_FF)),
            _resident((3, 2 * D_FF)),
            _resident((1, 2 * D_FF)),
            _resident((D_FF, D_MODEL)),
        ],
        out_specs=pl.BlockSpec(memory_space=pl.ANY),
        out_shape=jax.ShapeDtypeStruct((BATCH, SEQ, D_MODEL), F32),
        scratch_shapes=[pltpu.VMEM((FF_TS + 2, BATCH, D_MODEL), BF16),
                        pltpu.VMEM((FF_TS * BATCH, D_FF), BF16),
                        pltpu.VMEM((2, FF_TS, BATCH, D_MODEL), F32),
                        pltpu.SemaphoreType.DMA((2, BATCH))],
        compiler_params=_params(("arbitrary",)),
        name="ffn",
    )(x1, x1, x1, g, sc, sh, g2, fg, wu, cw, cb, wd)


def _gate_tiles(w_a, w_i):
    per = RG_TC // RNN_BLOCK
    nt = D_RNN // RG_TC
    both = jnp.stack([w_a.reshape(2, nt, per, RNN_BLOCK, RNN_BLOCK),
                      w_i.reshape(2, nt, per, RNN_BLOCK, RNN_BLOCK)], axis=4)
    eye = jnp.eye(per, dtype=w_a.dtype)
    full = jnp.einsum('dcjkgn,jm->dcjkgmn', both, eye)
    return full.reshape(2, nt, RG_TC, 2 * RG_TC)


def kernel(x, c, positions, w_ada, b_ada, norm1_g, w_in, conv_rnn_w, conv_rnn_b, w_rg_a, b_rg_a,
           w_rg_i, b_rg_i, rg_lambda, w_rnn_o, lam_q1, lam_k1, lam_q2, lam_k2, subln_g, w_attn_o,
           w_out, norm2_g, w_up, conv_ffn_w, conv_ffn_b, w_down, final_g):
    l = 0
    inv_freq = ROPE_THETA ** (-jnp.arange(0, ROPE_DIM, 2, dtype=F32) / ROPE_DIM)
    cos8, sin8 = _rope_tables(positions.astype(F32), inv_freq)
    cos_t = jnp.tile(cos8.transpose(1, 2, 0), (1, 1, LANES // ROPE_HALF))
    sin_t = jnp.tile(sin8.transpose(1, 2, 0), (1, 1, LANES // ROPE_HALF))

    mod = _ada(c, w_ada[l], b_ada[l])
    sh1, sc1, g1, sh2, sc2, g2 = [mod[:, m * D_MODEL:(m + 1) * D_MODEL] for m in range(N_MOD)]

    w_in_bf = w_in[l].astype(BF16)
    g_n1 = norm1_g[l].reshape(1, D_MODEL)
    proj, x3 = _inproj_sb(x, g_n1, sc1, sh1, w_in_bf)
    proj3 = proj.reshape(SEQ, BATCH, 4 * D_MODEL)
    qkv = _inproj_bs(x, g_n1, sc1.reshape(BATCH, 1, D_MODEL), sh1.reshape(BATCH, 1, D_MODEL),
                     cos_t, sin_t, w_in_bf)

    wg = (0.5 * _gate_tiles(w_rg_a[l], w_rg_i[l])).astype(BF16)
    hfb = _rglru(proj3, conv_rnn_w[l], conv_rnn_b[l].reshape(1, D_RNN), wg,
                 0.5 * b_rg_a[l].reshape(2, 1, D_RNN), 0.5 * b_rg_i[l].reshape(2, 1, D_RNN),
                 rg_lambda[l].reshape(2, 1, D_RNN))

    lamv = jnp.stack([lam_q1[l], lam_k1[l], lam_q2[l], lam_k2[l]]).astype(F32)
    attn = _attn(qkv, lamv, subln_g[l].reshape(V_DIM, 1))

    x1 = _merge(hfb, proj3, attn, x3, g1, w_rnn_o[l].astype(BF16), w_attn_o[l].astype(BF16),
                w_out[l].astype(BF16))

    return _ffn(x1, norm2_g[l].reshape(1, D_MODEL), sc2, sh2, g2, final_g.reshape(1, D_MODEL),
                w_up[l].astype(BF16), conv_ffn_w[l], conv_ffn_b[l].reshape(1, 2 * D_FF),
                w_down[l].astype(BF16))
```

```python
import math

import jax
import jax.numpy as jnp
import numpy as np
from jax import lax
from jax.experimental import pallas as pl
from jax.experimental.pallas import tpu as pltpu

F32 = jnp.float32
BF16 = jnp.bfloat16

D_MODEL = 1024
BATCH = 16
SEQ = 2048
TOKENS = BATCH * SEQ
D_RNN = D_MODEL
N_RNN_BLOCKS = 16
RNN_BLOCK = D_RNN // N_RNN_BLOCKS
RNN_CONV_W = 4
RNN_CONV_LEFT = 2
RG_C = 8.0
N_HEADS = 8
HEAD_DIM = 64
V_DIM = 2 * HEAD_DIM
ROPE_DIM = HEAD_DIM // 4
ROPE_HALF = ROPE_DIM // 2
ROPE_THETA = 500000.0
D_FF = 2816
FFN_CONV_W = 3
N_MOD = 6
NORM_EPS = 1e-6
LAM_INIT = 0.8 - 0.6 * math.exp(-0.3 * 0)
LOG2_E = math.log2(math.e)
Q_SCALE = HEAD_DIM ** -0.5 * LOG2_E

LANES = 128
VMEM_LIMIT = 52 * 1024 * 1024

ADA_TN = 1024
IN_TS = 64
QKV_TM = 1024
RG_TT = 128
RG_TC = 256
AT_TQ = 512
AT_KB = 256
AT_VROWS = V_DIM + 16
AT_NT = SEQ // AT_TQ
MG_TS = 32
FF_TS = 64
FF_CH = 256
FF_NCH = D_FF // FF_CH


def _params(sem):
    return pltpu.CompilerParams(dimension_semantics=sem, vmem_limit_bytes=VMEM_LIMIT)


def _resident(shape):
    return pl.BlockSpec(shape, lambda *_: (0,) * len(shape), pipeline_mode=pl.Buffered(1))


def _ada_kernel(c_ref, w_ref, b_ref, o_ref):
    c = c_ref[...]
    ca = c * jax.nn.sigmoid(c)
    o_ref[...] = jnp.dot(ca, w_ref[...], preferred_element_type=F32,
                         precision=lax.Precision.HIGHEST) + b_ref[...]


def _ada(c, w, b):
    n = w.shape[1]
    return pl.pallas_call(
        _ada_kernel,
        grid=(n // ADA_TN,),
        in_specs=[pl.BlockSpec((BATCH, D_MODEL), lambda j: (0, 0)),
                  pl.BlockSpec((D_MODEL, ADA_TN), lambda j: (0, j)),
                  pl.BlockSpec((1, ADA_TN), lambda j: (0, j))],
        out_specs=pl.BlockSpec((BATCH, ADA_TN), lambda j: (0, j)),
        out_shape=jax.ShapeDtypeStruct((BATCH, n), F32),
        compiler_params=_params(("arbitrary",)),
        name="adaln",
    )(c, w, b.reshape(1, n))


def _rms_mod(x, g, sc, sh):
    ms = jnp.mean(x * x, axis=-1, keepdims=True)
    y = x * lax.rsqrt(ms + NORM_EPS) * g
    return y * (1.0 + sc) + sh


def _resident_cols(width, block):
    return pl.BlockSpec((D_MODEL, width), lambda *_: (0, block), pipeline_mode=pl.Buffered(1))


def _seq_batch_copies(hbm, vmem, sems, step, slot, ts, to_vmem):
    out = []
    for b in range(BATCH):
        h = hbm.at[b, pl.ds(step * ts, ts), :]
        v = vmem.at[slot, :, b, :]
        out.append(pltpu.make_async_copy(h, v, sems.at[slot, b]) if to_vmem
                   else pltpu.make_async_copy(v, h, sems.at[slot, b]))
    return out


def _inproj_sb_kernel(x_hbm, g_ref, sc_ref, sh_ref, wxy_ref, wga_ref, wgb_ref, o_ref, x3_ref,
                      x_buf, x_sem, h_scr):
    i = pl.program_id(0)
    slot = i % 2

    @pl.when(i == 0)
    def _():
        for cp in _seq_batch_copies(x_hbm, x_buf, x_sem, 0, 0, IN_TS, True):
            cp.start()

    @pl.when(i + 1 < pl.num_programs(0))
    def _():
        for cp in _seq_batch_copies(x_hbm, x_buf, x_sem, i + 1, 1 - slot, IN_TS, True):
            cp.start()

    for cp in _seq_batch_copies(x_hbm, x_buf, x_sem, i, slot, IN_TS, True):
        cp.wait()
    x = x_buf[slot]
    x3_ref[...] = x
    h = _rms_mod(x, g_ref[...], sc_ref[...], sh_ref[...])
    h_scr[...] = h.reshape(IN_TS * BATCH, D_MODEL).astype(BF16)
    for j, (w_ref, c0) in enumerate([(wxy_ref, 0), (wxy_ref, D_MODEL), (wga_ref, 0), (wgb_ref, 0)]):
        o_ref[:, j * D_MODEL:(j + 1) * D_MODEL] = jnp.dot(
            h_scr[...], w_ref[:, c0:c0 + D_MODEL], preferred_element_type=F32).astype(BF16)


def _inproj_sb(x, g, sc, sh, w_bf):
    tm = IN_TS * BATCH
    ncol = 4 * D_MODEL
    return pl.pallas_call(
        _inproj_sb_kernel,
        grid=(SEQ // IN_TS,),
        in_specs=[pl.BlockSpec(memory_space=pl.ANY),
                  _resident((1, D_MODEL)),
                  _resident((BATCH, D_MODEL)),
                  _resident((BATCH, D_MODEL)),
                  _resident_cols(2 * D_MODEL, 0), _resident_cols(D_MODEL, 5),
                  _resident_cols(D_MODEL, 6)],
        out_specs=[pl.BlockSpec((tm, ncol), lambda i: (i, 0)),
                   pl.BlockSpec((IN_TS, BATCH, D_MODEL), lambda i: (i, 0, 0))],
        out_shape=[jax.ShapeDtypeStruct((TOKENS, ncol), BF16),
                   jax.ShapeDtypeStruct((SEQ, BATCH, D_MODEL), F32)],
        scratch_shapes=[pltpu.VMEM((2, IN_TS, BATCH, D_MODEL), F32),
                        pltpu.SemaphoreType.DMA((2, BATCH)),
                        pltpu.VMEM((tm, D_MODEL), BF16)],
        compiler_params=_params(("arbitrary",)),
        name="inproj_sb",
    )(x, g, sc, sh, w_bf, w_bf, w_bf)


def _rope_kernel(pos_ref, invf_ref, cos_ref, sin_ref):
    pos = pos_ref[...]
    for f in range(ROPE_HALF):
        ang = pos * invf_ref[f]
        cos_ref[f] = jnp.cos(ang)
        sin_ref[f] = jnp.sin(ang)


def _rope_tables(posf, inv_freq):
    full = lambda shape: pl.BlockSpec(shape, lambda i: (0,) * len(shape))
    out = jax.ShapeDtypeStruct((ROPE_HALF, BATCH, SEQ), F32)
    return pl.pallas_call(
        _rope_kernel,
        grid=(1,),
        in_specs=[full((BATCH, SEQ)), full((ROPE_HALF, 1, 1))],
        out_specs=[full((ROPE_HALF, BATCH, SEQ)), full((ROPE_HALF, BATCH, SEQ))],
        out_shape=[out, out],
        compiler_params=_params(("arbitrary",)),
        name="rope_tables",
    )(posf, inv_freq.reshape(ROPE_HALF, 1, 1))


def _inproj_bs_kernel(x_ref, g_ref, sc_ref, sh_ref, cos_ref, sin_ref, wq_ref, wk_ref, wv_ref,
                      o_ref, h_scr, tab_scr):
    h = _rms_mod(x_ref[0], g_ref[...], sc_ref[0], sh_ref[0])
    h_scr[...] = h.astype(BF16)
    c = cos_ref[0]
    s = sin_ref[0]
    lane = lax.broadcasted_iota(jnp.int32, c.shape, 1) % HEAD_DIM
    tab_scr[0] = jnp.where(lane < ROPE_DIM, c, 1.0)
    tab_scr[1] = jnp.where(lane < ROPE_HALF, -s, 0.0)
    tab_scr[2] = jnp.where((lane >= ROPE_HALF) & (lane < ROPE_DIM), s, 0.0)

    def rope_tile(j, w_ref, scale):
        acc = jnp.dot(h_scr[...], w_ref[...], preferred_element_type=F32)
        ct, sa, sb = tab_scr[0] * scale, tab_scr[1] * scale, tab_scr[2] * scale
        for cidx in range(D_MODEL // LANES):
            lo = j * D_MODEL + cidx * LANES
            t = acc[:, cidx * LANES:(cidx + 1) * LANES]
            r = (t * ct + pltpu.roll(t, LANES - ROPE_HALF, 1) * sa
                 + pltpu.roll(t, ROPE_HALF, 1) * sb)
            o_ref[0, :, lo:lo + LANES] = r.astype(BF16)

    rope_tile(0, wq_ref, Q_SCALE)
    rope_tile(1, wk_ref, 1.0)
    o_ref[0, :, 2 * D_MODEL:] = jnp.dot(h_scr[...], wv_ref[...],
                                        preferred_element_type=F32).astype(BF16)


def _inproj_bs(x, g, sc, sh, cos_t, sin_t, w_bf):
    ncol = 3 * D_MODEL
    return pl.pallas_call(
        _inproj_bs_kernel,
        grid=(BATCH, SEQ // QKV_TM),
        in_specs=[pl.BlockSpec((1, QKV_TM, D_MODEL), lambda b, i: (b, i, 0)),
                  _resident((1, D_MODEL)),
                  pl.BlockSpec((1, 1, D_MODEL), lambda b, i: (b, 0, 0)),
                  pl.BlockSpec((1, 1, D_MODEL), lambda b, i: (b, 0, 0)),
                  pl.BlockSpec((1, QKV_TM, LANES), lambda b, i: (b, i, 0)),
                  pl.BlockSpec((1, QKV_TM, LANES), lambda b, i: (b, i, 0)),
                  _resident_cols(D_MODEL, 2), _resident_cols(D_MODEL, 3),
                  _resident_cols(D_MODEL, 4)],
        out_specs=pl.BlockSpec((1, QKV_TM, ncol), lambda b, i: (b, i, 0)),
        out_shape=jax.ShapeDtypeStruct((BATCH, SEQ, ncol), BF16),
        scratch_shapes=[pltpu.VMEM((QKV_TM, D_MODEL), BF16),
                        pltpu.VMEM((3, QKV_TM, LANES), F32)],
        compiler_params=_params(("parallel", "parallel")),
        name="inproj_bs",
    )(x, g, sc, sh, cos_t, sin_t, w_bf, w_bf, w_bf)


def _rglru_kernel(xp_ref, xm_ref, xn_ref, cw_ref, cb_ref, wg_ref, ba_ref, bi_ref, lam_ref,
                  o_ref, a_scr, u_scr, h_scr):
    d = pl.program_id(1)
    t = pl.program_id(2)
    nt = pl.num_programs(2)
    te = t + d * (nt - 1 - 2 * t)
    rows = RG_TT * BATCH

    @pl.when(t == 0)
    def _():
        h_scr[...] = jnp.zeros_like(h_scr)

    def chunk(reverse):
        pm = (te > 0).astype(F32)
        nm = (te < nt - 1).astype(F32)
        xin = jnp.concatenate([xp_ref[...] * pm.astype(BF16),
                               xm_ref[...],
                               xn_ref[...] * nm.astype(BF16)], axis=0)
        cw = cw_ref[...].astype(BF16)
        xc = cb_ref[...].astype(BF16)
        for k in range(RNN_CONV_W):
            xc = xc + xin[k:k + RG_TT] * cw[k:k + 1]
        xc_bf = xc.reshape(rows, RG_TC)
        x2 = xc_bf.astype(F32)

        g = jnp.dot(xc_bf, wg_ref[0, 0], preferred_element_type=F32)
        two_r = 1.0 + jnp.tanh(g[:, :RG_TC] + ba_ref[0])
        two_i = 1.0 + jnp.tanh(g[:, RG_TC:] + bi_ref[0])
        z = -lam_ref[0]
        sp = jnp.maximum(z, 0.0) + jnp.log1p(jnp.exp(-jnp.abs(z)))
        c_ln = (-0.5 * RG_C) * sp
        a = jnp.exp2(two_r * (c_ln * LOG2_E))
        th = jnp.tanh(two_r * c_ln)
        y = -2.0 * th / (1.0 - th)
        u = jnp.where(y > 0.0, y * lax.rsqrt(y), 0.0) * ((0.5 * x2) * two_i)
        a_scr[...] = a.reshape(RG_TT, BATCH, RG_TC)
        u_scr[...] = u.reshape(RG_TT, BATCH, RG_TC)
        h = h_scr[...]
        for s in range(RG_TT):
            idx = RG_TT - 1 - s if reverse else s
            h = a_scr[idx] * h + u_scr[idx]
            o_ref[0, idx] = h.astype(BF16)
        h_scr[...] = h

    @pl.when(d == 0)
    def _():
        chunk(False)

    @pl.when(d == 1)
    def _():
        chunk(True)


def _rglru(proj3, cw, cb, wg, ba, bi, lam):
    nt = SEQ // RG_TT

    def te(d, t):
        return t + d * (nt - 1 - 2 * t)

    return pl.pallas_call(
        _rglru_kernel,
        grid=(D_RNN // RG_TC, 2, nt),
        in_specs=[
            pl.BlockSpec((RNN_CONV_LEFT, BATCH, RG_TC),
                         lambda c, d, t: (jnp.maximum(te(d, t) * (RG_TT // RNN_CONV_LEFT) - 1, 0),
                                          0, c)),
            pl.BlockSpec((RG_TT, BATCH, RG_TC), lambda c, d, t: (te(d, t), 0, c)),
            pl.BlockSpec((RNN_CONV_W - 1 - RNN_CONV_LEFT, BATCH, RG_TC),
                         lambda c, d, t: (jnp.minimum((te(d, t) + 1) * RG_TT, SEQ - 1), 0, c)),
            pl.BlockSpec((RNN_CONV_W, RG_TC), lambda c, d, t: (0, c)),
            pl.BlockSpec((1, RG_TC), lambda c, d, t: (0, c)),
            pl.BlockSpec((1, 1, RG_TC, 2 * RG_TC), lambda c, d, t: (d, c, 0, 0)),
            pl.BlockSpec((1, 1, RG_TC), lambda c, d, t: (d, 0, c)),
            pl.BlockSpec((1, 1, RG_TC), lambda c, d, t: (d, 0, c)),
            pl.BlockSpec((1, 1, RG_TC), lambda c, d, t: (d, 0, c)),
        ],
        out_specs=pl.BlockSpec((1, RG_TT, BATCH, RG_TC), lambda c, d, t: (d, te(d, t), 0, c)),
        out_shape=jax.ShapeDtypeStruct((2, SEQ, BATCH, D_RNN), BF16),
        scratch_shapes=[pltpu.VMEM((RG_TT, BATCH, RG_TC), F32),
                        pltpu.VMEM((RG_TT, BATCH, RG_TC), F32),
                        pltpu.VMEM((BATCH, RG_TC), F32)],
        compiler_params=_params(("parallel", "arbitrary", "arbitrary")),
        name="rglru",
    )(proj3, proj3, proj3, cw, cb, wg, ba, bi, lam)


def _zero_after(x):
    bits = lax.bitcast_convert_type(x, jnp.uint32)
    half = jnp.uint32(16)
    return lax.bitcast_convert_type(
        lax.shift_right_logical(lax.shift_right_logical(bits, half), half), F32)


def _attn_kernel(lamv_ref, gt_ref, q_ref, k_ref, v_ref, o_ref, vt_scr, s_scr, m_scr, p_scr):
    g = pl.program_id(0)
    nkb = SEQ // AT_KB
    nq = 2 * AT_TQ
    dn = (((1,), (1,)), ((), ()))

    @pl.when(g == 0)
    def _():
        s_scr[...] = jnp.zeros_like(s_scr)
        m_scr[...] = jnp.zeros_like(m_scr)
        p_scr[...] = jnp.ones_like(p_scr)

    vt_scr[0:V_DIM] = v_ref[0].astype(F32).T.astype(BF16)
    vt_scr[V_DIM:AT_VROWS] = jnp.ones((AT_VROWS - V_DIM, SEQ), BF16)

    lv = lamv_ref[...]
    lam = (jnp.exp(jnp.sum(lv[0:1] * lv[1:2], axis=-1, keepdims=True))
           - jnp.exp(jnp.sum(lv[2:3] * lv[3:4], axis=-1, keepdims=True)) + LAM_INIT)

    for h in range(AT_NT):
        s_new, m_new = s_scr.at[h % 2], m_scr.at[h % 2]
        s_old, m_old = s_scr.at[(h - 1) % 2], m_scr.at[(h - 1) % 2]
        p_new, p_old = p_scr.at[(h - 1) % AT_NT], p_scr.at[h]
        q = q_ref[0, h * AT_TQ:(h + 1) * AT_TQ, :]
        lane = lax.broadcasted_iota(jnp.int32, q.shape, 1)
        zero = jnp.zeros_like(q)
        qcat = jnp.concatenate([jnp.where(lane < HEAD_DIM, q, zero),
                                jnp.where(lane >= HEAD_DIM, q, zero)], axis=0)
        m_prev = m_old[...]
        m8 = jnp.full((8, nq), -jnp.inf, F32)
        for kb in range(nkb):
            rows = slice(kb * AT_KB, (kb + 1) * AT_KB)
            s = lax.dot_general(k_ref[0, rows, :], qcat, dn, preferred_element_type=F32)
            s_new[rows, :] = s
            for r in range(AT_KB // 8):
                m8 = jnp.maximum(m8, s[r * 8:(r + 1) * 8, :])
            m_tied = m_prev + _zero_after(s[AT_KB - 8:AT_KB, :])
            e = jnp.exp2(s_old[rows, :].reshape(AT_KB // 8, 8, nq) - m_tied[None])
            p_new[rows, :] = e.reshape(AT_KB, nq).astype(BF16)
        m_new[...] = jnp.max(m8, axis=0, keepdims=True)
        acc = jnp.dot(vt_scr[...], p_old[...], preferred_element_type=F32)
        o1 = acc[0:V_DIM, :AT_TQ] / acc[V_DIM:V_DIM + 1, :AT_TQ]
        o2 = acc[0:V_DIM, AT_TQ:] / acc[V_DIM:V_DIM + 1, AT_TQ:]
        o = o1 - lam * o2
        ms = jnp.mean(o * o, axis=0, keepdims=True)
        y = o * lax.rsqrt(ms + NORM_EPS) * gt_ref[...]
        o_ref[0, h * AT_TQ:(h + 1) * AT_TQ, :] = (y * (1.0 - LAM_INIT)).T.astype(BF16)


def _attn(qkv, lamv, subln_gt):
    n_heads = BATCH * N_HEADS

    def head_index(hd, col0):
        return hd // N_HEADS, 0, col0 + hd % N_HEADS

    front = lambda g: jnp.minimum(g, n_heads - 1)
    back = lambda g: jnp.maximum(g - 1, 0)
    return pl.pallas_call(
        _attn_kernel,
        grid=(n_heads + 1,),
        in_specs=[
            pl.BlockSpec((4, HEAD_DIM), lambda g: (0, 0)),
            pl.BlockSpec((V_DIM, 1), lambda g: (0, 0)),
            pl.BlockSpec((1, SEQ, LANES), lambda g: head_index(front(g), 0)),
            pl.BlockSpec((1, SEQ, LANES), lambda g: head_index(front(g), N_HEADS)),
            pl.BlockSpec((1, SEQ, LANES), lambda g: head_index(back(g), 2 * N_HEADS)),
        ],
        out_specs=pl.BlockSpec((1, SEQ, LANES), lambda g: head_index(back(g), 0)),
        out_shape=jax.ShapeDtypeStruct((BATCH, SEQ, N_HEADS * V_DIM), BF16),
        scratch_shapes=[pltpu.VMEM((AT_VROWS, SEQ), BF16),
                        pltpu.VMEM((2, SEQ, 2 * AT_TQ), F32),
                        pltpu.VMEM((2, 1, 2 * AT_TQ), F32),
                        pltpu.VMEM((AT_NT, SEQ, 2 * AT_TQ), BF16)],
        compiler_params=_params(("arbitrary",)),
        name="diffattn",
    )(lamv, subln_gt, qkv, qkv, qkv)


def _merge_kernel(hf_ref, hb_ref, yr_ref, ga_ref, gb_ref, at_ref, perm_ref, x_ref, g1_ref,
                  wr_ref, wa_ref, wo_ref, o_ref):
    rows = MG_TS * BATCH
    hr = hf_ref[0].astype(F32) + hb_ref[0].astype(F32)
    ya = (hr * jax.nn.gelu(yr_ref[...].astype(F32))).reshape(rows, D_RNN).astype(BF16)
    br_a = jnp.dot(ya, wr_ref[...], preferred_element_type=F32)
    at_sb = jnp.dot(perm_ref[...], at_ref[...].reshape(rows, D_MODEL),
                    preferred_element_type=F32).astype(BF16)
    br_b = jnp.dot(at_sb, wa_ref[...], preferred_element_type=F32)
    ga = jax.nn.sigmoid(ga_ref[...].reshape(rows, D_MODEL).astype(F32))
    gb = jax.nn.sigmoid(gb_ref[...].reshape(rows, D_MODEL).astype(F32))
    merged = (ga * br_a + gb * br_b).astype(BF16)
    m = jnp.dot(merged, wo_ref[...], preferred_element_type=F32)
    o_ref[...] = x_ref[...] + g1_ref[...] * m.reshape(MG_TS, BATCH, D_MODEL)


def _merge(hfb, proj3, attn, x3, g1, wr, wa, wo):
    tok = lambda cidx: pl.BlockSpec((MG_TS, BATCH, D_MODEL), lambda i: (i, 0, cidx))
    wspec = _resident((D_MODEL, D_MODEL))
    rows = MG_TS * BATCH
    r_out = np.arange(rows)
    perm = np.zeros((rows, rows), np.float32)
    perm[r_out, (r_out % BATCH) * MG_TS + r_out // BATCH] = 1.0
    return pl.pallas_call(
        _merge_kernel,
        grid=(SEQ // MG_TS,),
        in_specs=[pl.BlockSpec((1, MG_TS, BATCH, D_RNN), lambda i: (0, i, 0, 0)),
                  pl.BlockSpec((1, MG_TS, BATCH, D_RNN), lambda i: (1, i, 0, 0)),
                  tok(1), tok(2), tok(3),
                  pl.BlockSpec((BATCH, MG_TS, D_MODEL), lambda i: (0, i, 0)),
                  _resident((rows, rows)),
                  tok(0),
                  _resident((BATCH, D_MODEL)),
                  wspec, wspec, wspec],
        out_specs=tok(0),
        out_shape=jax.ShapeDtypeStruct((SEQ, BATCH, D_MODEL), F32),
        compiler_params=_params(("parallel",)),
        name="merge",
    )(hfb, hfb, proj3, proj3, proj3, attn, jnp.asarray(perm, BF16), x3, g1, wr, wa, wo)


def _ffn_kernel(xp_ref, xm_ref, xn_ref, g_ref, sc_ref, sh_ref, g2_ref, fg_ref,
                wu_ref, cw_ref, cb_ref, wd_ref, o_hbm, h_scr, act_scr, o_buf, o_sem):
    i = pl.program_id(0)
    n = pl.num_programs(0)
    rows = FF_TS * BATCH
    slot = i % 2

    @pl.when(i >= 2)
    def _():
        for cp in _seq_batch_copies(o_hbm, o_buf, o_sem, i - 2, slot, FF_TS, False):
            cp.wait()
    pm = (i > 0).astype(F32)
    nm = (i < n - 1).astype(F32)
    g, sc, sh = g_ref[...], sc_ref[...], sh_ref[...]
    h_scr[0:1] = (_rms_mod(xp_ref[...], g, sc, sh) * pm).astype(BF16)
    h_scr[1:FF_TS + 1] = _rms_mod(xm_ref[...], g, sc, sh).astype(BF16)
    h_scr[FF_TS + 1:FF_TS + 2] = (_rms_mod(xn_ref[...], g, sc, sh) * nm).astype(BF16)

    def conv(up, lo):
        cw = cw_ref[:, lo:lo + FF_CH]
        cv = cb_ref[:, lo:lo + FF_CH]
        for k in range(FFN_CONV_W):
            cv = cv + up[k:k + FF_TS] * cw[k:k + 1]
        return cv.reshape(rows, FF_CH)

    def up_act(ci):
        hx = h_scr[...].reshape((FF_TS + 2) * BATCH, D_MODEL)
        lo_v, lo_g = ci * FF_CH, D_FF + ci * FF_CH
        val = conv(jnp.dot(hx, wu_ref[:, lo_v:lo_v + FF_CH], preferred_element_type=F32
                           ).reshape(FF_TS + 2, BATCH, FF_CH), lo_v)
        gt = conv(jnp.dot(hx, wu_ref[:, lo_g:lo_g + FF_CH], preferred_element_type=F32
                          ).reshape(FF_TS + 2, BATCH, FF_CH), lo_g)
        act_scr[:, lo_v:lo_v + FF_CH] = (gt * jax.nn.sigmoid(gt) * val).astype(BF16)

    for ci in range(FF_NCH):
        up_act(ci)
    down = jnp.dot(act_scr[...], wd_ref[...], preferred_element_type=F32)
    x2 = xm_ref[...] + g2_ref[...] * down.reshape(FF_TS, BATCH, D_MODEL)
    ms = jnp.mean(x2 * x2, axis=-1, keepdims=True)
    o_buf[slot] = x2 * lax.rsqrt(ms + NORM_EPS) * fg_ref[...]
    for cp in _seq_batch_copies(o_hbm, o_buf, o_sem, i, slot, FF_TS, False):
        cp.start()

    @pl.when(i == n - 1)
    def _():
        for cp in _seq_batch_copies(o_hbm, o_buf, o_sem, i - 1, 1 - slot, FF_TS, False):
            cp.wait()
        for cp in _seq_batch_copies(o_hbm, o_buf, o_sem, i, slot, FF_TS, False):
            cp.wait()


def _ffn(x1, g, sc, sh, g2, fg, wu, cw, cb, wd):
    assert SEQ // FF_TS >= 2
    return pl.pallas_call(
        _ffn_kernel,
        grid=(SEQ // FF_TS,),
        in_specs=[
            pl.BlockSpec((1, BATCH, D_MODEL), lambda i: (jnp.maximum(i * FF_TS - 1, 0), 0, 0)),
            pl.BlockSpec((FF_TS, BATCH, D_MODEL), lambda i: (i, 0, 0)),
            pl.BlockSpec((1, BATCH, D_MODEL),
                         lambda i: (jnp.minimum((i + 1) * FF_TS, SEQ - 1), 0, 0)),
            _resident((1, D_MODEL)), _resident((BATCH, D_MODEL)), _resident((BATCH, D_MODEL)),
            _resident((BATCH, D_MODEL)), _resident((1, D_MODEL)),
            _resident((D_MODEL, 2 * D_FF)),
            _resident((3, 2 * D_FF)),
            _resident((1, 2 * D_FF)),
            _resident((D_FF, D_MODEL)),
        ],
        out_specs=pl.BlockSpec(memory_space=pl.ANY),
        out_shape=jax.ShapeDtypeStruct((BATCH, SEQ, D_MODEL), F32),
        scratch_shapes=[pltpu.VMEM((FF_TS + 2, BATCH, D_MODEL), BF16),
                        pltpu.VMEM((FF_TS * BATCH, D_FF), BF16),
                        pltpu.VMEM((2, FF_TS, BATCH, D_MODEL), F32),
                        pltpu.SemaphoreType.DMA((2, BATCH))],
        compiler_params=_params(("arbitrary",)),
        name="ffn",
    )(x1, x1, x1, g, sc, sh, g2, fg, wu, cw, cb, wd)


def _gate_tiles(w_a, w_i):
    per = RG_TC // RNN_BLOCK
    nt = D_RNN // RG_TC
    both = jnp.stack([w_a.reshape(2, nt, per, RNN_BLOCK, RNN_BLOCK),
                      w_i.reshape(2, nt, per, RNN_BLOCK, RNN_BLOCK)], axis=4)
    eye = jnp.eye(per, dtype=w_a.dtype)
    full = jnp.einsum('dcjkgn,jm->dcjkgmn', both, eye)
    return full.reshape(2, nt, RG_TC, 2 * RG_TC)


def kernel(x, c, positions, w_ada, b_ada, norm1_g, w_in, conv_rnn_w, conv_rnn_b, w_rg_a, b_rg_a,
           w_rg_i, b_rg_i, rg_lambda, w_rnn_o, lam_q1, lam_k1, lam_q2, lam_k2, subln_g, w_attn_o,
           w_out, norm2_g, w_up, conv_ffn_w, conv_ffn_b, w_down, final_g):
    l = 0
    inv_freq = ROPE_THETA ** (-jnp.arange(0, ROPE_DIM, 2, dtype=F32) / ROPE_DIM)
    cos8, sin8 = _rope_tables(positions.astype(F32), inv_freq)
    cos_t = jnp.tile(cos8.transpose(1, 2, 0), (1, 1, LANES // ROPE_HALF))
    sin_t = jnp.tile(sin8.transpose(1, 2, 0), (1, 1, LANES // ROPE_HALF))

    mod = _ada(c, w_ada[l], b_ada[l])
    sh1, sc1, g1, sh2, sc2, g2 = [mod[:, m * D_MODEL:(m + 1) * D_MODEL] for m in range(N_MOD)]

    w_in_bf = w_in[l].astype(BF16)
    g_n1 = norm1_g[l].reshape(1, D_MODEL)
    proj, x3 = _inproj_sb(x, g_n1, sc1, sh1, w_in_bf)
    proj3 = proj.reshape(SEQ, BATCH, 4 * D_MODEL)
    qkv = _inproj_bs(x, g_n1, sc1.reshape(BATCH, 1, D_MODEL), sh1.reshape(BATCH, 1, D_MODEL),
                     cos_t, sin_t, w_in_bf)

    wg = (0.5 * _gate_tiles(w_rg_a[l], w_rg_i[l])).astype(BF16)
    hfb = _rglru(proj3, conv_rnn_w[l], conv_rnn_b[l].reshape(1, D_RNN), wg,
                 0.5 * b_rg_a[l].reshape(2, 1, D_RNN), 0.5 * b_rg_i[l].reshape(2, 1, D_RNN),
                 rg_lambda[l].reshape(2, 1, D_RNN))

    lamv = jnp.stack([lam_q1[l], lam_k1[l], lam_q2[l], lam_k2[l]]).astype(F32)
    attn = _attn(qkv, lamv, subln_g[l].reshape(V_DIM, 1))

    x1 = _merge(hfb, proj3, attn, x3, g1, w_rnn_o[l].astype(BF16), w_attn_o[l].astype(BF16),
                w_out[l].astype(BF16))

    return _ffn(x1, norm2_g[l].reshape(1, D_MODEL), sc2, sh2, g2, final_g.reshape(1, D_MODEL),
                w_up[l].astype(BF16), conv_ffn_w[l], conv_ffn_b[l].reshape(1, 2 * D_FF),
                w_down[l].astype(BF16))
```
